```python
import jax
import jax.numpy as jnp
from jax import lax
import numpy as np

D_MODEL = 4096
BATCH = 4
SEQ = 4096
DEPTH = 4

GRID_W = 64
CTX_LEN = 256
NORM_EPS = 1e-6
GN_EPS = 64e-5
L2_EPS = 1e-12
ADA_RANK = D_MODEL // 8
N_MOD = 6

FNET_WIDTH = D_MODEL // 2
FNET_GROUPS = 4
FNET_GROUP_CH = FNET_WIDTH // FNET_GROUPS
RWKV_WIDTH = D_MODEL - FNET_WIDTH
RWKV_HEAD = 64
RWKV_HEADS = RWKV_WIDTH // RWKV_HEAD
DECAY_LORA = 96
ICLR_LORA = 96
GATE_LORA = 256
RWKV_IN = 3 * RWKV_WIDTH + DECAY_LORA + ICLR_LORA + GATE_LORA
RWKV_SPLITS = (RWKV_WIDTH, 2 * RWKV_WIDTH, 3 * RWKV_WIDTH, 3 * RWKV_WIDTH + DECAY_LORA, 3 * RWKV_WIDTH + DECAY_LORA + ICLR_LORA)
MIX_IN = FNET_WIDTH + RWKV_IN

ATT_HEAD = 64
ATT_Q_HEADS = D_MODEL // ATT_HEAD
ATT_KV_HEADS = ATT_Q_HEADS // 8
ATT_GROUP = ATT_Q_HEADS // ATT_KV_HEADS
Q_COLS = ATT_Q_HEADS * ATT_HEAD
KV_COLS = ATT_KV_HEADS * ATT_HEAD
ATT_SCALE = ATT_HEAD ** -0.5
WINDOW = 128
QBLOCK = 128
KEYSPAN = QBLOCK + 2 * WINDOW
ROPE_THETA = 10000.0
ROPE_AXIS_DIM = ATT_HEAD // 2

N_EXPERTS = 16
N_EXPERT_GROUPS = 4
EXPERTS_PER_GROUP = N_EXPERTS // N_EXPERT_GROUPS
GROUP_SCORE_K = 2
TOP_K = 2
D_EXPERT = 3 * D_MODEL // 16

N_EVEN = (DEPTH + 1) // 2
N_ODD = DEPTH // 2

kernel_name = 'hybrid_fnet_rwkv7_swa_sink_grouped_moe_dit'

F32 = jnp.float32


def rmsnorm(x, g):
    xf = x.astype(F32)
    y = xf * lax.rsqrt(jnp.mean(xf * xf, axis=-1, keepdims=True) + NORM_EPS)
    return (y * g.astype(F32)).astype(x.dtype)


def modulate(h, shift, scale):
    return h * (1 + scale) + shift


def adaln(cvec, down, up, bias):
    m = (jax.nn.silu(cvec) @ down) @ up + bias
    return jnp.split(m, N_MOD, axis=-1)


def axial_rope_tables(n_tokens):
    rows = n_tokens // GRID_W
    row_pos = jnp.broadcast_to(jnp.arange(rows, dtype=F32)[:, None], (rows, GRID_W)).reshape(-1)
    col_pos = jnp.broadcast_to(jnp.arange(GRID_W, dtype=F32)[None, :], (rows, GRID_W)).reshape(-1)
    inv_freq = ROPE_THETA ** (-jnp.arange(0, ROPE_AXIS_DIM, 2, dtype=F32) / ROPE_AXIS_DIM)
    ang_r = row_pos[:, None] * inv_freq
    ang_c = col_pos[:, None] * inv_freq
    return (jnp.cos(ang_r), jnp.sin(ang_r), jnp.cos(ang_c), jnp.sin(ang_c))


def _rotate(x, cos, sin):
    x1, x2 = jnp.split(x, 2, axis=-1)
    cos = cos[None, :, None, :]
    sin = sin[None, :, None, :]
    return jnp.concatenate([x1 * cos - x2 * sin, x2 * cos + x1 * sin], axis=-1)


def axial_rope(x, tables):
    cos_r, sin_r, cos_c, sin_c = tables
    x_row, x_col = jnp.split(x.astype(F32), 2, axis=-1)
    return jnp.concatenate([_rotate(x_row, cos_r, sin_r), _rotate(x_col, cos_c, sin_c)], axis=-1).astype(x.dtype)


def fourier_mix(f):
    b, n, _ = f.shape
    fg = f.astype(F32).reshape(b, n, FNET_GROUPS, FNET_GROUP_CH)
    y = jnp.fft.fft2(fg, axes=(1, 3), norm='ortho').real
    return y.reshape(b, n, FNET_WIDTH).astype(f.dtype)


def centred_token_shift(p, mu):
    pp = jnp.pad(p, ((0, 0), (1, 1), (0, 0)))
    return p + mu * (0.5 * (pp[:, :-2] + pp[:, 2:]) - p)


def wkv_scan(state0, r, decay, k, v, kk, a, reverse):
    xs = tuple(jnp.moveaxis(t, 1, 0) for t in (r, decay, k, v, kk, a))

    def step(S, inp):
        r_t, w_t, k_t, v_t, kk_t, a_t = inp
        sa = jnp.einsum('bhvk,bhk->bhv', S, kk_t)
        S = (S * w_t[:, :, None, :]
             - sa[..., None] * (kk_t * a_t)[:, :, None, :]
             + v_t[..., None] * k_t[:, :, None, :])
        return S, jnp.einsum('bhvk,bhk->bhv', S, r_t)

    s_last, out = lax.scan(step, state0, xs, reverse=reverse)
    return s_last, jnp.moveaxis(out, 0, 1)


def rwkv7_terms(p, w0, w_up, a0, a_up, g_up, k_k, k_a):
    b, n, _ = p.shape

    def heads(t):
        return t.reshape(b, n, RWKV_HEADS, RWKV_HEAD)

    r, k, v, w_in, a_in, g_in = jnp.split(p.astype(F32), RWKV_SPLITS, axis=-1)
    kk = heads(k * k_k)
    kk = kk * lax.rsqrt(jnp.sum(kk * kk, axis=-1, keepdims=True) + L2_EPS)
    per_dir = []
    for d in range(2):
        w_logit = w0[d] + jnp.tanh(w_in) @ w_up[d]
        decay = jnp.exp(-jnp.exp(-jax.nn.softplus(-w_logit) - 0.5))
        a = jax.nn.sigmoid(a0[d] + a_in @ a_up[d])
        k_d = k * (1 + (a - 1) * k_a)
        per_dir.append((heads(decay), heads(k_d), heads(a)))
    g = jax.nn.sigmoid(g_in) @ g_up
    return heads(r), heads(v), kk, g, per_dir


def rwkv7_readout(o, r, v, k_fwd, k_bwd, g, r_k, lnx_g, lnx_b):
    b, n = o.shape[:2]
    mu = jnp.mean(o, axis=-1, keepdims=True)
    var = jnp.mean(jnp.square(o - mu), axis=-1, keepdims=True)
    on = (o - mu) * lax.rsqrt(var + GN_EPS) * lnx_g.reshape(RWKV_HEADS, RWKV_HEAD) + lnx_b.reshape(RWKV_HEADS, RWKV_HEAD)
    bonus = (jnp.sum(r * k_fwd * r_k, axis=-1, keepdims=True) + jnp.sum(r * k_bwd * r_k, axis=-1, keepdims=True)) * v
    return (on + bonus).reshape(b, n, RWKV_WIDTH) * g


def rwkv7_bidirectional(pc, pl, w0, w_up, a0, a_up, g_up, k_k, k_a, r_k, lnx_g, lnx_b, with_ctx_out):
    rc, vc, kkc, gc, dir_c = rwkv7_terms(pc, w0, w_up, a0, a_up, g_up, k_k, k_a)
    rl, vl, kkl, gl, dir_l = rwkv7_terms(pl, w0, w_up, a0, a_up, g_up, k_k, k_a)
    state0 = jnp.zeros((pl.shape[0], RWKV_HEADS, RWKV_HEAD, RWKV_HEAD), F32)
    outs_c = []
    outs_l = []
    for d, reverse in enumerate((False, True)):
        decay_c, k_c, a_c = dir_c[d]
        s_ctx, out_c = wkv_scan(state0, rc, decay_c, k_c, vc, kkc, a_c, reverse)
        decay_l, k_l, a_l = dir_l[d]
        _, out_l = wkv_scan(s_ctx, rl, decay_l, k_l, vl, kkl, a_l, reverse)
        outs_c.append(out_c)
        outs_l.append(out_l)
    yl = rwkv7_readout(outs_l[0] + outs_l[1], rl, vl, dir_l[0][1], dir_l[1][1], gl, r_k, lnx_g, lnx_b)
    yc = None
    if with_ctx_out:
        yc = rwkv7_readout(outs_c[0] + outs_c[1], rc, vc, dir_c[0][1], dir_c[1][1], gc, r_k, lnx_g, lnx_b)
    return yc, yl


def fourier_rwkv_mixer(hc, hl, w_in, w_out, shift_mu, w0, w_up, a0, a_up, g_up, k_k, k_a, r_k, lnx_g, lnx_b, with_ctx_out):
    pc = hc @ w_in
    pl = hl @ w_in
    rw_c = centred_token_shift(pc[..., FNET_WIDTH:], shift_mu)
    rw_l = centred_token_shift(pl[..., FNET_WIDTH:], shift_mu)
    oc, ol = rwkv7_bidirectional(rw_c, rw_l, w0, w_up, a0, a_up, g_up, k_k, k_a, r_k, lnx_g, lnx_b, with_ctx_out)
    yl = jnp.concatenate([fourier_mix(pl[..., :FNET_WIDTH]), ol.astype(hl.dtype)], axis=-1) @ w_out
    yc = None
    if with_ctx_out:
        yc = jnp.concatenate([fourier_mix(pc[..., :FNET_WIDTH]), oc.astype(hc.dtype)], axis=-1) @ w_out
    return yc, yl


def sink_softmax(scores, sink):
    col = jnp.broadcast_to(sink.astype(F32)[None, :, :, None, None], scores.shape[:-1] + (1,))
    return jax.nn.softmax(jnp.concatenate([scores, col], axis=-1), axis=-1)[..., :-1]


def context_attention(q, k, v, sink):
    s = jnp.einsum('bqkgd,bskd->bkgqs', q, k).astype(F32) * ATT_SCALE
    p = sink_softmax(s, sink).astype(v.dtype)
    return jnp.einsum('bkgqs,bskd->bqkgd', p, v)


def banded_latent_attention(q, k, v, k_ctx, v_ctx, sink):
    b, n = q.shape[:2]
    n_blocks = n // QBLOCK
    pad = ((0, 0), (WINDOW, WINDOW), (0, 0), (0, 0))
    k_pad = jnp.pad(k, pad)
    v_pad = jnp.pad(v, pad)
    offs = jnp.arange(KEYSPAN)[None, :] - WINDOW - jnp.arange(QBLOCK)[:, None]
    in_band = jnp.abs(offs) <= WINDOW

    def block(i):
        start = i * QBLOCK
        qb = lax.dynamic_slice_in_dim(q, start, QBLOCK, axis=1)
        kb = lax.dynamic_slice_in_dim(k_pad, start, KEYSPAN, axis=1)
        vb = lax.dynamic_slice_in_dim(v_pad, start, KEYSPAN, axis=1)
        key_pos = start - WINDOW + jnp.arange(KEYSPAN)
        valid = in_band & ((key_pos >= 0) & (key_pos < n))[None, :]
        s_loc = jnp.einsum('bqkgd,bskd->bkgqs', qb, kb).astype(F32) * ATT_SCALE
        s_loc = jnp.where(valid, s_loc, -jnp.inf)
        s_ctx = jnp.einsum('bqkgd,bskd->bkgqs', qb, k_ctx).astype(F32) * ATT_SCALE
        p = sink_softmax(jnp.concatenate([s_loc, s_ctx], axis=-1), sink).astype(v.dtype)
        return (jnp.einsum('bkgqs,bskd->bqkgd', p[..., :KEYSPAN], vb)
                + jnp.einsum('bkgqs,bskd->bqkgd', p[..., KEYSPAN:], v_ctx))

    out = lax.map(block, jnp.arange(n_blocks))
    return jnp.moveaxis(out, 0, 1).reshape(b, n, Q_COLS)


def window_attention_mixer(hc, hl, w_qkv, w_o, sink, rope, with_ctx_out):
    def project(h):
        b, n, _ = h.shape
        q, k, v = jnp.split(h @ w_qkv, [Q_COLS, Q_COLS + KV_COLS], axis=-1)
        return (q.reshape(b, n, ATT_Q_HEADS, ATT_HEAD),
                k.reshape(b, n, ATT_KV_HEADS, ATT_HEAD),
                v.reshape(b, n, ATT_KV_HEADS, ATT_HEAD))

    def grouped(q):
        return q.reshape(q.shape[0], q.shape[1], ATT_KV_HEADS, ATT_GROUP, ATT_HEAD)

    sink_g = sink.reshape(ATT_KV_HEADS, ATT_GROUP)
    qc, kc, vc = project(hc)
    ql, kl, vl = project(hl)
    ql = axial_rope(ql, rope)
    kl = axial_rope(kl, rope)
    yl = banded_latent_attention(grouped(ql), kl, vl, kc, vc, sink_g) @ w_o
    yc = None
    if with_ctx_out:
        oc = context_attention(grouped(qc), kc, vc, sink_g)
        yc = oc.reshape(oc.shape[0], oc.shape[1], Q_COLS) @ w_o
    return yc, yl


def shared_router_moe(h, router_w, router_b, w_gate, w_up, w_down):
    shape = h.shape
    t = h.reshape(-1, shape[-1])
    n_tok = t.shape[0]
    probs = jax.nn.softmax((t @ router_w).astype(F32), axis=-1)
    sel = (probs + router_b.astype(F32)).reshape(n_tok, N_EXPERT_GROUPS, EXPERTS_PER_GROUP)
    group_score = jnp.sum(lax.top_k(sel, GROUP_SCORE_K)[0], axis=-1)
    best_group = jnp.argmax(group_score, axis=-1)
    in_group = sel[jnp.arange(n_tok), best_group]
    _, local = lax.top_k(in_group, TOP_K)
    expert_idx = best_group[:, None] * EXPERTS_PER_GROUP + local
    gate_w = jnp.take_along_axis(probs, expert_idx, axis=-1)
    gate_w = gate_w / jnp.sum(gate_w, axis=-1, keepdims=True)
    combine = jnp.sum(jax.nn.one_hot(expert_idx, N_EXPERTS, dtype=F32) * gate_w[..., None], axis=1).astype(t.dtype)
    y = jnp.zeros_like(t)
    for e in range(N_EXPERTS):
        hid = jax.nn.silu(t @ w_gate[e]) * (t @ w_up[e])
        y = y + combine[:, e:e + 1] * (hid @ w_down[e])
    return y.reshape(shape)


def setup_inputs(seed: int = 0) -> dict:
    key = jax.random.key(seed)
    ks = jax.random.split(key, 31)
    d = D_MODEL

    def nrm(i, shape, scale):
        return jax.random.normal(ks[i], shape, F32) * scale

    return {
        'x': nrm(0, (BATCH, SEQ, d), 1.0),
        'c': nrm(1, (BATCH, d), 1.0),
        'ctx': nrm(2, (BATCH, CTX_LEN, d), 1.0),
        'c_ctx': nrm(3, (d,), 1.0),
        'ada_down': nrm(4, (DEPTH, d, ADA_RANK), d ** -0.5),
        'ada_up': nrm(5, (DEPTH, ADA_RANK, N_MOD * d), 0.5 * ADA_RANK ** -0.5),
        'ada_bias': nrm(6, (DEPTH, N_MOD * d), 0.01),
        'norm1_g': 1.0 + nrm(7, (DEPTH, d), 0.02),
        'norm2_g': 1.0 + nrm(8, (DEPTH, d), 0.02),
        'final_g': 1.0 + nrm(9, (d,), 0.02),
        'mix_w_in': nrm(10, (N_EVEN, d, MIX_IN), d ** -0.5),
        'mix_w_out': nrm(11, (N_EVEN, FNET_WIDTH + RWKV_WIDTH, d), (FNET_WIDTH + RWKV_WIDTH) ** -0.5),
        'shift_mu': jax.random.uniform(ks[12], (N_EVEN, RWKV_IN), F32),
        'decay_w0': nrm(13, (N_EVEN, 2, RWKV_WIDTH), 1.0),
        'decay_up': nrm(14, (N_EVEN, 2, DECAY_LORA, RWKV_WIDTH), 0.5 * DECAY_LORA ** -0.5),
        'iclr_a0': nrm(15, (N_EVEN, 2, RWKV_WIDTH), 0.5),
        'iclr_up': nrm(16, (N_EVEN, 2, ICLR_LORA, RWKV_WIDTH), 0.5 * ICLR_LORA ** -0.5),
        'gate_up': nrm(17, (N_EVEN, GATE_LORA, RWKV_WIDTH), GATE_LORA ** -0.5),
        'k_k': 0.85 + nrm(18, (N_EVEN, RWKV_WIDTH), 0.05),
        'k_a': 1.0 + nrm(19, (N_EVEN, RWKV_WIDTH), 0.05),
        'r_k': nrm(20, (N_EVEN, RWKV_HEADS, RWKV_HEAD), 0.1),
        'lnx_g': 1.0 + nrm(21, (N_EVEN, RWKV_WIDTH), 0.02),
        'lnx_b': nrm(22, (N_EVEN, RWKV_WIDTH), 0.01),
        'att_w_qkv': nrm(23, (N_ODD, d, Q_COLS + 2 * KV_COLS), d ** -0.5),
        'att_w_o': nrm(24, (N_ODD, Q_COLS, d), Q_COLS ** -0.5),
        'att_sink': nrm(25, (N_ODD, ATT_Q_HEADS), 0.5),
        'router_w': nrm(26, (d, N_EXPERTS), d ** -0.5),
        'router_b': nrm(27, (N_EXPERTS,), 0.01),
        'exp_w_gate': nrm(28, (DEPTH, N_EXPERTS, d, D_EXPERT), d ** -0.5),
        'exp_w_up': nrm(29, (DEPTH, N_EXPERTS, d, D_EXPERT), d ** -0.5),
        'exp_w_down': nrm(30, (DEPTH, N_EXPERTS, D_EXPERT, d), D_EXPERT ** -0.5),
    }


def reference(x, c, ctx, c_ctx, ada_down, ada_up, ada_bias, norm1_g, norm2_g, final_g,
              mix_w_in, mix_w_out, shift_mu, decay_w0, decay_up, iclr_a0, iclr_up, gate_up,
              k_k, k_a, r_k, lnx_g, lnx_b, att_w_qkv, att_w_o, att_sink,
              router_w, router_b, exp_w_gate, exp_w_up, exp_w_down):
    rope = axial_rope_tables(x.shape[1])
    n_ctx = ctx.shape[1]
    xl, xc = x, ctx
    for layer in range(DEPTH):
        last = layer == DEPTH - 1
        j = layer // 2
        sh1, sc1, g1, sh2, sc2, g2 = [m[:, None, :] for m in adaln(c, ada_down[layer], ada_up[layer], ada_bias[layer])]
        csh1, csc1, cg1, csh2, csc2, cg2 = adaln(c_ctx, ada_down[layer], ada_up[layer], ada_bias[layer])
        hl = modulate(rmsnorm(xl, norm1_g[layer]), sh1, sc1)
        hc = modulate(rmsnorm(xc, norm1_g[layer]), csh1, csc1)
        if layer % 2 == 0:
            yc, yl = fourier_rwkv_mixer(hc, hl, mix_w_in[j], mix_w_out[j], shift_mu[j], decay_w0[j], decay_up[j],
                                        iclr_a0[j], iclr_up[j], gate_up[j], k_k[j], k_a[j], r_k[j],
                                        lnx_g[j], lnx_b[j], not last)
        else:
            yc, yl = window_attention_mixer(hc, hl, att_w_qkv[j], att_w_o[j], att_sink[j], rope, not last)
        xl = xl + g1 * yl
        hl2 = modulate(rmsnorm(xl, norm2_g[layer]), sh2, sc2)
        if last:
            xl = xl + g2 * shared_router_moe(hl2, router_w, router_b, exp_w_gate[layer], exp_w_up[layer], exp_w_down[layer])
        else:
            xc = xc + cg1 * yc
            hc2 = modulate(rmsnorm(xc, norm2_g[layer]), csh2, csc2)
            f = shared_router_moe(jnp.concatenate([hc2, hl2], axis=1), router_w, router_b,
                                  exp_w_gate[layer], exp_w_up[layer], exp_w_down[layer])
            xc = xc + cg2 * f[:, :n_ctx]
            xl = xl + g2 * f[:, n_ctx:]
    return rmsnorm(xl, final_g)
```

```python
import functools
import math

import jax
import jax.numpy as jnp
from jax import lax
from jax.experimental import pallas as pl
from jax.experimental.pallas import tpu as pltpu

F32 = jnp.float32
BF16 = jnp.bfloat16
HIGHEST = lax.Precision.HIGHEST

GRID_W = 64
NORM_EPS = 1e-6
GN_EPS = 64e-5
L2_EPS = 1e-12
N_MOD = 6
FNET_GROUPS = 4
HEAD = 64
ATT_GROUP = 8
WINDOW = 128
QBLOCK = 128
ROPE_THETA = 10000.0
N_EXPERT_GROUPS = 4
EXPERTS_PER_GROUP = 4
CHUNK = 64
MOD_ROWS = 8

VMEM_LIMIT = 56 * 1024 * 1024


def _cparams(n_axes):
    return pltpu.CompilerParams(dimension_semantics=("arbitrary",) * n_axes,
                                vmem_limit_bytes=VMEM_LIMIT)


def _tile(n, pref):
    t = min(n, pref)
    while n % t:
        t //= 2
    return t


def _row_id(r0, n_ctx, seq):
    return jnp.where(r0 < n_ctx, 0, 1 + (r0 - n_ctx) // seq)


def _dot(a, b):
    return jnp.dot(a, b, preferred_element_type=F32)


def _dot_hi(a, b):
    return jnp.dot(a, b, preferred_element_type=F32, precision=HIGHEST)


def _dot_t0(a, b):
    return lax.dot_general(a, b, (((0,), (0,)), ((), ())), preferred_element_type=F32)


def _dot_t1(a, b):
    return lax.dot_general(a, b, (((1,), (1,)), ((), ())), preferred_element_type=F32)


def _sigmoid(x):
    return 1.0 / (1.0 + jnp.exp(-x))


def _norm_mod(x, g, sh, sc):
    ms = jnp.mean(x * x, axis=-1, keepdims=True)
    y = x * lax.rsqrt(ms + NORM_EPS) * g
    return y * (1.0 + sc) + sh


def _adaln_kernel(cv_ref, down_ref, up_ref, bias_ref, o_ref, t_scr):
    @pl.when(pl.program_id(1) == 0)
    def _():
        cv = cv_ref[...]
        s = cv * _sigmoid(cv)
        t_scr[...] = _dot(s.astype(BF16), down_ref[...].astype(BF16)).astype(BF16)
    o_ref[...] = _dot(t_scr[...], up_ref[...].astype(BF16)) + bias_ref[...]


def adaln_all(cvec, ada_down, ada_up, ada_bias):
    depth, d, rank = ada_down.shape
    n = ada_up.shape[2]
    tn = _tile(n, 2048)
    bias = ada_bias.reshape(depth, 1, n)
    return pl.pallas_call(
        _adaln_kernel,
        grid=(depth, n // tn),
        in_specs=[pl.BlockSpec((MOD_ROWS, d), lambda l, j: (0, 0)),
                  pl.BlockSpec((None, d, rank), lambda l, j: (l, 0, 0)),
                  pl.BlockSpec((None, rank, tn), lambda l, j: (l, 0, j)),
                  pl.BlockSpec((None, 1, tn), lambda l, j: (l, 0, j))],
        out_specs=pl.BlockSpec((None, MOD_ROWS, tn), lambda l, j: (l, 0, j)),
        out_shape=jax.ShapeDtypeStruct((depth, MOD_ROWS, n), F32),
        scratch_shapes=[pltpu.VMEM((MOD_ROWS, rank), BF16)],
        compiler_params=_cparams(2), name="adaln",
    )(cvec, ada_down, ada_up, bias)


def _nmm_kernel(*refs, tm, n_ctx, seq, rope_cols):
    if rope_cols:
        x_ref, g_ref, sh_ref, sc_ref, w_ref, cos_ref, sin_ref, o_ref, h_scr = refs
    else:
        x_ref, g_ref, sh_ref, sc_ref, w_ref, o_ref, h_scr = refs
    i = pl.program_id(0)
    j = pl.program_id(1)
    r0 = i * tm

    @pl.when(j == 0)
    def _():
        rid = _row_id(r0, n_ctx, seq)
        h = _norm_mod(x_ref[...], g_ref[...], sh_ref[pl.ds(rid, 1), :], sc_ref[pl.ds(rid, 1), :])
        h_scr[...] = h.astype(BF16)

    acc = _dot(h_scr[...], w_ref[...])
    if rope_cols:
        tn = acc.shape[1]
        do_rope = jnp.logical_and(r0 >= n_ctx, j * tn < rope_cols)

        @pl.when(do_rope)
        def _():
            lane = lax.broadcasted_iota(jnp.int32, acc.shape, 1)
            swapped = jnp.where(lane % 32 < 16, pltpu.roll(acc, tn - 16, 1), pltpu.roll(acc, 16, 1))
            o_ref[...] = (acc * cos_ref[...] + swapped * sin_ref[...]).astype(o_ref.dtype)

        @pl.when(jnp.logical_not(do_rope))
        def _():
            o_ref[...] = acc.astype(o_ref.dtype)
    else:
        o_ref[...] = acc.astype(o_ref.dtype)


def norm_mod_matmul(x, g, mods, layer, k_shift, k_scale, w, out_dtype, dims, rope=None, tm_pref=512, tn_pref=512):
    t, d = x.shape
    n = w.shape[1]
    n_ctx, seq = dims
    tm = _tile(math.gcd(n_ctx, seq), tm_pref)
    tn = _tile(n, tn_pref)
    in_specs = [pl.BlockSpec((tm, d), lambda i, j: (i, 0)),
                pl.BlockSpec((1, d), lambda i, j: (0, 0)),
                pl.BlockSpec((None, None, MOD_ROWS, d), lambda i, j: (layer, k_shift, 0, 0)),
                pl.BlockSpec((None, None, MOD_ROWS, d), lambda i, j: (layer, k_scale, 0, 0)),
                pl.BlockSpec((d, tn), lambda i, j: (0, j))]
    args = [x, g.reshape(1, d), mods, mods, w]
    rope_cols = 0
    if rope is not None:
        rope_cols, cos_t, sin_t = rope
        assert rope_cols % tn == 0 and cos_t.shape == (seq, tn)
        pos_map = lambda i, j: (jnp.maximum(i * tm - n_ctx, 0) % seq // tm, 0)
        in_specs += [pl.BlockSpec((tm, tn), pos_map), pl.BlockSpec((tm, tn), pos_map)]
        args += [cos_t, sin_t]
    return pl.pallas_call(
        functools.partial(_nmm_kernel, tm=tm, n_ctx=n_ctx, seq=seq, rope_cols=rope_cols),
        grid=(t // tm, n // tn),
        in_specs=in_specs,
        out_specs=pl.BlockSpec((tm, tn), lambda i, j: (i, j)),
        out_shape=jax.ShapeDtypeStruct((t, n), out_dtype),
        scratch_shapes=[pltpu.VMEM((tm, d), BF16)],
        compiler_params=_cparams(2), name="norm_mod_matmul",
    )(*args)


def _mmres_kernel(*refs, n_a, tm, n_ctx, seq):
    a_refs = refs[:n_a]
    w_refs = refs[n_a:2 * n_a]
    x_ref, gate_ref, o_ref = refs[2 * n_a:]
    acc = _dot(a_refs[0][...], w_refs[0][...])
    for a_ref, w_ref in zip(a_refs[1:], w_refs[1:]):
        acc += _dot(a_ref[...], w_ref[...])
    rid = _row_id(pl.program_id(0) * tm, n_ctx, seq)
    o_ref[...] = x_ref[...] + gate_ref[pl.ds(rid, 1), :] * acc


def matmul_gated_residual(a_list, w_list, x, mods, layer, k_gate, dims, tm_pref=512, tn_pref=512):
    t, d = x.shape
    n_ctx, seq = dims
    tm = _tile(math.gcd(n_ctx, seq), tm_pref)
    tn = _tile(d, tn_pref)
    n_a = len(a_list)
    in_specs = [pl.BlockSpec((tm, a.shape[1]), lambda i, j: (i, 0)) for a in a_list]
    in_specs += [pl.BlockSpec((w.shape[0], tn), lambda i, j: (0, j)) for w in w_list]
    in_specs += [pl.BlockSpec((tm, tn), lambda i, j: (i, j)),
                 pl.BlockSpec((None, None, MOD_ROWS, tn), lambda i, j: (layer, k_gate, 0, j))]
    return pl.pallas_call(
        functools.partial(_mmres_kernel, n_a=n_a, tm=tm, n_ctx=n_ctx, seq=seq),
        grid=(t // tm, d // tn),
        in_specs=in_specs,
        out_specs=pl.BlockSpec((tm, tn), lambda i, j: (i, j)),
        out_shape=jax.ShapeDtypeStruct((t, d), F32),
        input_output_aliases={2 * n_a: 0},
        compiler_params=_cparams(2), name="matmul_gated_residual",
    )(*a_list, *w_list, x, mods)


def _final_norm_kernel(x_ref, g_ref, o_ref):
    x = x_ref[...]
    ms = jnp.mean(x * x, axis=-1, keepdims=True)
    o_ref[...] = x * lax.rsqrt(ms + NORM_EPS) * g_ref[...]


def final_norm(x, g, n_ctx, tm_pref=512):
    t, d = x.shape
    tm = _tile(math.gcd(n_ctx, t - n_ctx), tm_pref)
    off = n_ctx // tm
    return pl.pallas_call(
        _final_norm_kernel,
        grid=((t - n_ctx) // tm,),
        in_specs=[pl.BlockSpec((tm, d), lambda i: (i + off, 0)),
                  pl.BlockSpec((1, d), lambda i: (0, 0))],
        out_specs=pl.BlockSpec((tm, d), lambda i: (i, 0)),
        out_shape=jax.ShapeDtypeStruct((t - n_ctx, d), F32),
        compiler_params=_cparams(1), name="final_norm",
    )(x, g.reshape(1, d))


def _dft_tables(n, scale=1.0):
    j = jnp.arange(n, dtype=jnp.int32)
    m = (j[:, None] * j[None, :]) % n
    ang = m.astype(F32) * (2.0 * math.pi / n)
    return (jnp.cos(ang) * scale).astype(BF16), (jnp.sin(ang) * scale).astype(BF16)


def _fnet_ch_kernel(x_ref, c_ref, s_ref, y1_ref, y2_ref):
    x = x_ref[...]
    y1_ref[...] = _dot(x, c_ref[...]).astype(y1_ref.dtype)
    y2_ref[...] = _dot(x, s_ref[...]).astype(y2_ref.dtype)


def fnet_channel_dft(f, cos_c, sin_c, tm_pref=1024):
    t, width = f.shape
    gc = width // FNET_GROUPS
    tm = _tile(t, tm_pref)
    spec = pl.BlockSpec((tm, gc), lambda i, g: (i, g))
    tab = pl.BlockSpec((gc, gc), lambda i, g: (0, 0))
    return pl.pallas_call(
        _fnet_ch_kernel,
        grid=(t // tm, FNET_GROUPS),
        in_specs=[spec, tab, tab],
        out_specs=[spec, spec],
        out_shape=[jax.ShapeDtypeStruct((t, width), BF16)] * 2,
        compiler_params=_cparams(2), name="fnet_channel_dft",
    )(f, cos_c, sin_c)


def _fnet_seq_kernel(c_ref, s_ref, y1_ref, y2_ref, o_ref, acc_ref, *, scale):
    k = pl.program_id(2)

    @pl.when(k == 0)
    def _():
        acc_ref[...] = jnp.zeros_like(acc_ref)

    acc_ref[...] += _dot(c_ref[...], y1_ref[...]) - _dot(s_ref[...], y2_ref[...])

    @pl.when(k == pl.num_programs(2) - 1)
    def _():
        o_ref[...] = (acc_ref[...] * scale).astype(o_ref.dtype)


def fnet_sequence_dft(y1, y2, row0, n_seq, length, cos_l, sin_l, gc, t_pref=512):
    width = y1.shape[1]
    tm = _tile(length, t_pref)
    tk = _tile(math.gcd(length, row0) if row0 else length, t_pref)
    nb = length // tm
    nk = length // tk
    scale = 1.0 / math.sqrt(length * gc)
    rb0 = row0 // tk
    y_spec = pl.BlockSpec((tk, width), lambda b, i, k: (rb0 + b * nk + k, 0))
    return pl.pallas_call(
        functools.partial(_fnet_seq_kernel, scale=scale),
        grid=(n_seq, nb, nk),
        in_specs=[pl.BlockSpec((tm, tk), lambda b, i, k: (i, k)),
                  pl.BlockSpec((tm, tk), lambda b, i, k: (i, k)),
                  y_spec, y_spec],
        out_specs=pl.BlockSpec((tm, width), lambda b, i, k: (b * nb + i, 0)),
        out_shape=jax.ShapeDtypeStruct((n_seq * length, width), BF16),
        scratch_shapes=[pltpu.VMEM((tm, width), F32)],
        compiler_params=_cparams(3), name="fnet_sequence_dft",
    )(cos_l, sin_l, y1, y2)


def _head_sum(x, ones_bd):
    hi = x.astype(BF16)
    lo = (x - hi.astype(F32)).astype(BF16)
    return _dot(hi, ones_bd) + _dot(lo, ones_bd)


def _shift(x, prev_row, next_row, mu, pos, length):
    rows = x.shape[0]
    ridx = lax.broadcasted_iota(jnp.int32, (rows, 1), 0)
    prev = jnp.where(ridx == 0, prev_row, pltpu.roll(x, 1, 0))
    nxt = jnp.where(ridx == rows - 1, next_row, pltpu.roll(x, rows - 1, 0))
    prev = jnp.where(pos == 0, 0.0, prev)
    nxt = jnp.where(pos == length - 1, 0.0, nxt)
    return x + mu * (0.5 * (prev + nxt) - x)


def _rwkv_terms_kernel(r_ref, k_ref, v_ref, rp_ref, kp_ref, vp_ref, rn_ref, kn_ref, vn_ref,
                       lo_ref, lop_ref, lon_ref, mur_ref, muk_ref, muv_ref, mulo_ref,
                       w0_ref, wup_ref, a0_ref, aup_ref, gup_ref, kk_ref, ka_ref, rk_ref, ones_ref,
                       r_o, v_o, kk_o, lw_o, kd_o, bd_o, g_o, bv_o, *, tm, n_ctx, ctx_len, seq, lora_pad):
    i = pl.program_id(0)
    r0 = i * tm
    ridx = lax.broadcasted_iota(jnp.int32, (tm, 1), 0) + r0
    in_ctx = r0 < n_ctx
    length = jnp.where(in_ctx, ctx_len, seq)
    pos = jnp.where(in_ctx, ridx % ctx_len, (ridx - n_ctx) % seq)

    def sh(ref, pref, nref, mu_ref):
        return _shift(ref[...], pref[7:8, :], nref[0:1, :], mu_ref[...], pos, length)

    r = sh(r_ref, rp_ref, rn_ref, mur_ref)
    k = sh(k_ref, kp_ref, kn_ref, muk_ref)
    v = sh(v_ref, vp_ref, vn_ref, muv_ref)
    lo = sh(lo_ref, lop_ref, lon_ref, mulo_ref)
    w_in = jnp.tanh(lo[:, :lora_pad]).astype(BF16)
    a_in = lo[:, lora_pad:2 * lora_pad].astype(BF16)
    g_in = _sigmoid(lo[:, 2 * lora_pad:]).astype(BF16)
    ones_bd = ones_ref[...]

    kk = k * kk_ref[...]
    kk = kk * lax.rsqrt(_head_sum(kk * kk, ones_bd) + L2_EPS)
    ksum = jnp.zeros_like(k)
    for d in range(2):
        w_logit = w0_ref[d:d + 1, :] + _dot(w_in, wup_ref[d])
        lw_o[d] = -math.exp(-0.5) * _sigmoid(w_logit)
        a = _sigmoid(a0_ref[d:d + 1, :] + _dot(a_in, aup_ref[d]))
        k_d = k * (1.0 + (a - 1.0) * ka_ref[...])
        kd_o[d] = k_d
        bd_o[d] = kk * a
        ksum = ksum + k_d
    g_o[...] = _dot(g_in, gup_ref[...])
    bv_o[...] = _head_sum(r * ksum * rk_ref[...], ones_bd) * v
    r_o[...] = r
    v_o[...] = v
    kk_o[...] = kk


def rwkv_terms(prkv, plo, mu_rkv, mu_lo, w0, w_up, a0, a_up, g_up, k_k, k_a, r_k, dims, lora_pad, tm_pref=256):
    t = prkv.shape[0]
    rw = prkv.shape[1] // 3
    n_ctx, ctx_len, seq = dims
    tm = _tile(math.gcd(n_ctx, seq), tm_pref)
    tn = _tile(rw, 512)
    nj = rw // tn
    lo_w = plo.shape[1]
    hb = tm // 8
    last8 = t // 8 - 1

    def main(c):
        return pl.BlockSpec((tm, tn), lambda i, j: (i, c * nj + j))

    def prev(c):
        return pl.BlockSpec((8, tn), lambda i, j: (jnp.maximum(i * hb - 1, 0), c * nj + j))

    def nxt(c):
        return pl.BlockSpec((8, tn), lambda i, j: (jnp.minimum((i + 1) * hb, last8), c * nj + j))

    def vec(c=0):
        return pl.BlockSpec((1, tn), lambda i, j: (0, c * nj + j))

    ones_bd = (jnp.arange(tn)[:, None] // HEAD == jnp.arange(tn)[None, :] // HEAD).astype(BF16)
    out_tok = pl.BlockSpec((tm, tn), lambda i, j: (i, j))
    out_dir = pl.BlockSpec((2, tm, tn), lambda i, j: (0, i, j))
    tok = jax.ShapeDtypeStruct((t, rw), F32)
    tok2 = jax.ShapeDtypeStruct((2, t, rw), F32)
    return pl.pallas_call(
        functools.partial(_rwkv_terms_kernel, tm=tm, n_ctx=n_ctx, ctx_len=ctx_len, seq=seq, lora_pad=lora_pad),
        grid=(t // tm, nj),
        in_specs=[main(0), main(1), main(2), prev(0), prev(1), prev(2), nxt(0), nxt(1), nxt(2),
                  pl.BlockSpec((tm, lo_w), lambda i, j: (i, 0)),
                  pl.BlockSpec((8, lo_w), lambda i, j: (jnp.maximum(i * hb - 1, 0), 0)),
                  pl.BlockSpec((8, lo_w), lambda i, j: (jnp.minimum((i + 1) * hb, last8), 0)),
                  vec(0), vec(1), vec(2),
                  pl.BlockSpec((1, lo_w), lambda i, j: (0, 0)),
                  pl.BlockSpec((2, tn), lambda i, j: (0, j)),
                  pl.BlockSpec((2, lora_pad, tn), lambda i, j: (0, 0, j)),
                  pl.BlockSpec((2, tn), lambda i, j: (0, j)),
                  pl.BlockSpec((2, lora_pad, tn), lambda i, j: (0, 0, j)),
                  pl.BlockSpec((g_up.shape[0], tn), lambda i, j: (0, j)),
                  vec(), vec(), vec(),
                  pl.BlockSpec((tn, tn), lambda i, j: (0, 0))],
        out_specs=[out_tok, out_tok, out_tok, out_dir, out_dir, out_dir, out_tok, out_tok],
        out_shape=[tok, tok, tok, tok2, tok2, tok2, tok, tok],
        compiler_params=_cparams(2), name="rwkv_terms",
    )(prkv, prkv, prkv, prkv, prkv, prkv, prkv, prkv, prkv, plo, plo, plo,
      mu_rkv, mu_rkv, mu_rkv, mu_lo, w0, w_up, a0, a_up, g_up, k_k, k_a, r_k, ones_bd)


def _rwkv_scan_kernel(r_ref, v_ref, kk_ref, lw_ref, kd_ref, bd_ref, o_ref, s_scr, *, n_pairs):
    d = pl.program_id(1)
    step = pl.program_id(2)
    c = CHUNK

    @pl.when(step == 0)
    def _():
        s_scr[...] = jnp.zeros_like(s_scr)

    row = lax.broadcasted_iota(jnp.int32, (c, c), 0)
    col = lax.broadcasted_iota(jnp.int32, (c, c), 1)
    fwd = d == 0
    ahead = (row - col) * jnp.where(fwd, 1, -1)
    incl = ahead >= 0
    strict = ahead > 0
    tri = incl.astype(F32)
    eye = row == col
    last = jnp.where(fwd, c - 1, 0)

    def pair_body(hp, carry):
        sl = pl.ds(pl.multiple_of(hp * 2 * HEAD, 2 * HEAD), 2 * HEAD)
        r2 = r_ref[:, sl]
        v2 = v_ref[:, sl]
        kk2 = kk_ref[:, sl]
        lw2 = lw_ref[:, sl]
        kd2 = kd_ref[:, sl]
        bd2 = bd_ref[:, sl]
        l_inc2 = _dot_hi(tri, lw2)
        outs = []
        for q in range(2):
            hs = slice(q * HEAD, (q + 1) * HEAD)
            r, v, kk, lw, kd, bd, l_inc = (a[:, hs] for a in (r2, v2, kk2, lw2, kd2, bd2, l_inc2))
            l_exc = l_inc - lw
            l_tot = jnp.sum(jnp.where(lax.broadcasted_iota(jnp.int32, (c, 1), 0) == last, l_inc, 0.0),
                            axis=0, keepdims=True)
            e_inc = jnp.exp(l_inc)
            e_neg = jnp.exp(-l_inc)
            e_rem = jnp.exp(l_tot - l_inc)
            rt = r * e_inc
            at = -kk * jnp.exp(l_exc)
            kt = kd * e_neg
            bt = bd * e_neg
            a_ab = jnp.where(strict, _dot_hi_t1(at, bt), 0.0)
            a_ak = jnp.where(strict, _dot_hi_t1(at, kt), 0.0)
            a_rb = jnp.where(incl, _dot_hi_t1(rt, bt), 0.0)
            a_rk = jnp.where(incl, _dot_hi_t1(rt, kt), 0.0)
            x = a_ab
            tinv = jnp.where(eye, 1.0, 0.0) + x
            for _ in range(int(math.log2(c)) - 1):
                x = _dot_hi(x, x)
                tinv = tinv + _dot_hi(tinv, x)
            wt = _dot_hi(tinv, at)
            u0 = _dot_hi(tinv, _dot_hi(a_ak, v))
            qm = rt + _dot_hi(a_rb, wt)
            o0 = _dot_hi(a_rb, u0) + _dot_hi(a_rk, v)
            bh = bd * e_rem
            kh = kd * e_rem
            m = jnp.where(eye, jnp.exp(l_tot), 0.0) + _dot_hi_t0(bh, wt)
            nn = _dot_hi_t0(bh, u0) + _dot_hi_t0(kh, v)
            h = hp * 2 + q
            st = s_scr[h]
            outs.append(_dot_hi(qm, st) + o0)
            s_scr[h] = _dot_hi(m, st) + nn
        o_ref[:, sl] = jnp.concatenate(outs, axis=1)
        return carry

    lax.fori_loop(0, n_pairs, pair_body, 0)


def _dot_hi_t1(a, b):
    return lax.dot_general(a, b, (((1,), (1,)), ((), ())), preferred_element_type=F32, precision=HIGHEST)


def _dot_hi_t0(a, b):
    return lax.dot_general(a, b, (((0,), (0,)), ((), ())), preferred_element_type=F32, precision=HIGHEST)


def rwkv_scan(r, v, kk, lw, kd, bd, batch, dims):
    t, rw = r.shape
    n_ctx, ctx_len, seq = dims
    c = CHUNK
    n_cc = ctx_len // c
    n_lc = seq // c
    steps = n_cc + n_lc

    def blk(b, d, s):
        ctx_i = jnp.where(d == 0, s, n_cc - 1 - s)
        lat_i = jnp.where(d == 0, s - n_cc, n_lc - 1 - (s - n_cc))
        return jnp.where(s < n_cc, b * n_cc + ctx_i, n_ctx // c + b * n_lc + lat_i)

    shared = pl.BlockSpec((c, rw), lambda b, d, s: (blk(b, d, s), 0))
    per_dir = pl.BlockSpec((None, c, rw), lambda b, d, s: (d, blk(b, d, s), 0))
    n_heads = rw // HEAD
    return pl.pallas_call(
        functools.partial(_rwkv_scan_kernel, n_pairs=n_heads // 2),
        grid=(batch, 2, steps),
        in_specs=[shared, shared, shared, per_dir, per_dir, per_dir],
        out_specs=per_dir,
        out_shape=jax.ShapeDtypeStruct((2, t, rw), F32),
        scratch_shapes=[pltpu.VMEM((n_heads, HEAD, HEAD), F32)],
        compiler_params=_cparams(3), name="rwkv_scan",
    )(r, v, kk, lw, kd, bd)


def _rwkv_readout_kernel(o_ref, bv_ref, g_ref, lg_ref, lb_ref, ones_ref, y_ref):
    o = o_ref[0] + o_ref[1]
    ones_bd = ones_ref[...]
    mu = _head_sum(o, ones_bd) * (1.0 / HEAD)
    dev = o - mu
    var = _head_sum(dev * dev, ones_bd) * (1.0 / HEAD)
    on = dev * lax.rsqrt(var + GN_EPS) * lg_ref[...] + lb_ref[...]
    y_ref[...] = ((on + bv_ref[...]) * g_ref[...]).astype(y_ref.dtype)


def rwkv_readout(o2, bv, g, lnx_g, lnx_b, tm_pref=512):
    _, t, rw = o2.shape
    tm = _tile(t, tm_pref)
    tn = _tile(rw, 512)
    ones_bd = (jnp.arange(tn)[:, None] // HEAD == jnp.arange(tn)[None, :] // HEAD).astype(BF16)
    tok = pl.BlockSpec((tm, tn), lambda i, j: (i, j))
    vec = pl.BlockSpec((1, tn), lambda i, j: (0, j))
    return pl.pallas_call(
        _rwkv_readout_kernel,
        grid=(t // tm, rw // tn),
        in_specs=[pl.BlockSpec((2, tm, tn), lambda i, j: (0, i, j)), tok, tok, vec, vec,
                  pl.BlockSpec((tn, tn), lambda i, j: (0, 0))],
        out_specs=tok,
        out_shape=jax.ShapeDtypeStruct((t, rw), BF16),
        compiler_params=_cparams(2), name="rwkv_readout",
    )(o2, bv, g, lnx_g.reshape(1, rw), lnx_b.reshape(1, rw), ones_bd)


def _attend(q, k_all, v_all, valid, sink):
    s = _dot_t1(q, k_all) * (HEAD ** -0.5)
    if valid is not None:
        s = jnp.where(valid, s, -jnp.inf)
    m = jnp.maximum(jnp.max(s, axis=-1, keepdims=True), sink)
    p = jnp.exp(s - m)
    denom = jnp.sum(p, axis=-1, keepdims=True) + jnp.exp(sink - m)
    return _dot(p.astype(BF16), v_all) / denom


def _attn_kernel(sink_ref, q_ref, kp_ref, kc_ref, kn_ref, vp_ref, vc_ref, vn_ref, kx_ref, vx_ref, o_ref,
                 *, seq, n_ctx_keys, n_ctx_blocks):
    pair = pl.program_id(2)
    qi = pl.program_id(1) - n_ctx_blocks
    span = QBLOCK + 2 * WINDOW
    rr = lax.broadcasted_iota(jnp.int32, (QBLOCK, span + n_ctx_keys), 0)
    cc = lax.broadcasted_iota(jnp.int32, (QBLOCK, span + n_ctx_keys), 1)
    key_pos = qi * QBLOCK - WINDOW + cc
    off = cc - WINDOW - rr
    in_band = (jnp.abs(off) <= WINDOW) & (key_pos >= 0) & (key_pos < seq) & (qi >= 0)
    valid = jnp.logical_or(cc >= span, in_band)
    outs = []
    for kv in range(2):
        hs = slice(kv * HEAD, (kv + 1) * HEAD)
        k_all = jnp.concatenate([kp_ref[:, hs], kc_ref[:, hs], kn_ref[:, hs], kx_ref[:, hs]], axis=0)
        v_all = jnp.concatenate([vp_ref[:, hs], vc_ref[:, hs], vn_ref[:, hs], vx_ref[:, hs]], axis=0)
        for g in range(ATT_GROUP):
            hq = kv * ATT_GROUP + g
            q = q_ref[:, hq * HEAD:(hq + 1) * HEAD]
            outs.append(_attend(q, k_all, v_all, valid, sink_ref[pair * 2 * ATT_GROUP + hq]))
    o_ref[...] = jnp.concatenate(outs, axis=1).astype(o_ref.dtype)


def attention(qkv, sink, batch, dims, q_cols, kv_cols):
    t = qkv.shape[0]
    n_ctx, ctx_len, seq = dims
    n_pairs = kv_cols // (2 * HEAD)
    qw = 2 * ATT_GROUP * HEAD
    kw = 2 * HEAD
    nqb = seq // QBLOCK
    ncb = ctx_len // QBLOCK
    lat0 = n_ctx // QBLOCK
    kcol = q_cols // kw
    vcol = (q_cols + kv_cols) // kw
    smem = pl.BlockSpec(memory_space=pltpu.SMEM)

    def q_map(b, i, p):
        return (jnp.where(i < ncb, b * ncb + i, lat0 + b * nqb + i - ncb), p)

    def band(col0, shift):
        def imap(b, i, p):
            return (lat0 + b * nqb + jnp.clip(i - ncb + shift, 0, nqb - 1), col0 + p)
        return pl.BlockSpec((QBLOCK, kw), imap)

    return pl.pallas_call(
        functools.partial(_attn_kernel, seq=seq, n_ctx_keys=ctx_len, n_ctx_blocks=ncb),
        grid=(batch, ncb + nqb, n_pairs),
        in_specs=[smem,
                  pl.BlockSpec((QBLOCK, qw), q_map),
                  band(kcol, -1), band(kcol, 0), band(kcol, 1),
                  band(vcol, -1), band(vcol, 0), band(vcol, 1),
                  pl.BlockSpec((ctx_len, kw), lambda b, i, p: (b, kcol + p)),
                  pl.BlockSpec((ctx_len, kw), lambda b, i, p: (b, vcol + p))],
        out_specs=pl.BlockSpec((QBLOCK, qw), q_map),
        out_shape=jax.ShapeDtypeStruct((t, q_cols), BF16),
        compiler_params=_cparams(3), name="attention",
    )(sink, qkv, qkv, qkv, qkv, qkv, qkv, qkv, qkv, qkv)


def _rope_tables(seq, width):
    pos = jnp.arange(seq, dtype=jnp.int32)
    row_pos = (pos // GRID_W).astype(F32)
    col_pos = (pos % GRID_W).astype(F32)
    half = HEAD // 2
    inv_freq = ROPE_THETA ** (-jnp.arange(0, half, 2, dtype=F32) / half)
    ang_r = row_pos[:, None] * inv_freq
    ang_c = col_pos[:, None] * inv_freq
    cos64 = jnp.concatenate([jnp.cos(ang_r), jnp.cos(ang_r), jnp.cos(ang_c), jnp.cos(ang_c)], axis=-1)
    sin64 = jnp.concatenate([-jnp.sin(ang_r), jnp.sin(ang_r), -jnp.sin(ang_c), jnp.sin(ang_c)], axis=-1)
    reps = width // HEAD
    return jnp.tile(cos64, (1, reps)), jnp.tile(sin64, (1, reps))


def _router_kernel(x_ref, g_ref, sh_ref, sc_ref, rw_ref, rb_ref, h_ref, ids_ref, gts_ref, *, tm, n_ctx, seq):
    rid = _row_id(pl.program_id(0) * tm, n_ctx, seq)
    h = _norm_mod(x_ref[...], g_ref[...], sh_ref[pl.ds(rid, 1), :], sc_ref[pl.ds(rid, 1), :])
    h_ref[...] = h
    logits = _dot_hi_t1(rw_ref[...], h)
    e = jnp.exp(logits - jnp.max(logits, axis=0, keepdims=True))
    probs = e / jnp.sum(e, axis=0, keepdims=True)
    sel = probs + rb_ref[...]
    n_g, per = N_EXPERT_GROUPS, EXPERTS_PER_GROUP

    def row(a, r):
        return a[r:r + 1, :]

    scores = []
    for gi in range(n_g):
        a, b, c, d = (row(sel, gi * per + r) for r in range(per))
        hi1, lo1 = jnp.maximum(a, b), jnp.minimum(a, b)
        hi2, lo2 = jnp.maximum(c, d), jnp.minimum(c, d)
        scores.append(jnp.maximum(hi1, hi2) + jnp.maximum(jnp.minimum(hi1, hi2), jnp.maximum(lo1, lo2)))
    best = scores[0]
    bg = jnp.zeros_like(best, dtype=jnp.int32)
    for gi in range(1, n_g):
        better = scores[gi] > best
        best = jnp.where(better, scores[gi], best)
        bg = jnp.where(better, gi, bg)
    in_sel = []
    in_prob = []
    for r in range(per):
        s_r = row(sel, r)
        p_r = row(probs, r)
        for gi in range(1, n_g):
            s_r = jnp.where(bg == gi, row(sel, gi * per + r), s_r)
            p_r = jnp.where(bg == gi, row(probs, gi * per + r), p_r)
        in_sel.append(s_r)
        in_prob.append(p_r)

    def argmax_first(vals, exclude):
        bv = None
        for r in range(per):
            v = vals[r] if exclude is None else jnp.where(exclude == r, -jnp.inf, vals[r])
            if bv is None:
                bv, bi = v, jnp.zeros_like(bg)
            else:
                better = v > bv
                bv = jnp.where(better, v, bv)
                bi = jnp.where(better, r, bi)
        return bi

    i1 = argmax_first(in_sel, None)
    i2 = argmax_first(in_sel, i1)

    def pick(vals, idx):
        out = vals[0]
        for r in range(1, per):
            out = jnp.where(idx == r, vals[r], out)
        return out

    p1 = pick(in_prob, i1)
    p2 = pick(in_prob, i2)
    tot = p1 + p2
    ids_ref[...] = jnp.concatenate([bg * per + i1, bg * per + i2], axis=0)
    gts_ref[...] = jnp.concatenate([p1 / tot, p2 / tot], axis=0)


def router(x, g, mods, layer, router_w_t, router_b, dims, tm_pref=256):
    t, d = x.shape
    n_ctx, seq = dims
    n_e = router_w_t.shape[0]
    tm = _tile(math.gcd(n_ctx, seq), tm_pref)
    return pl.pallas_call(
        functools.partial(_router_kernel, tm=tm, n_ctx=n_ctx, seq=seq),
        grid=(t // tm,),
        in_specs=[pl.BlockSpec((tm, d), lambda i: (i, 0)),
                  pl.BlockSpec((1, d), lambda i: (0, 0)),
                  pl.BlockSpec((None, None, MOD_ROWS, d), lambda i: (layer, 3, 0, 0)),
                  pl.BlockSpec((None, None, MOD_ROWS, d), lambda i: (layer, 4, 0, 0)),
                  pl.BlockSpec((n_e, d), lambda i: (0, 0)),
                  pl.BlockSpec((n_e, 1), lambda i: (0, 0))],
        out_specs=[pl.BlockSpec((tm, d), lambda i: (i, 0)),
                   pl.BlockSpec((2, tm), lambda i: (0, i)),
                   pl.BlockSpec((2, tm), lambda i: (0, i))],
        out_shape=[jax.ShapeDtypeStruct((t, d), F32),
                   jax.ShapeDtypeStruct((2, t), jnp.int32),
                   jax.ShapeDtypeStruct((2, t), F32)],
        compiler_params=_cparams(1), name="router",
    )(x, g.reshape(1, d), mods, mods, router_w_t, router_b.reshape(n_e, 1))


def _gather_rows(src_hbm, idx_ref, base, n_rows, dst, sem):
    def copy(r):
        return pltpu.make_async_copy(src_hbm.at[pl.ds(idx_ref[base + r], 1), :], dst.at[pl.ds(r, 1), :], sem)

    def start(r, c):
        copy(r).start()
        return c

    def wait(r, c):
        copy(r).wait()
        return c

    lax.fori_loop(0, n_rows, start, 0)
    lax.fori_loop(0, n_rows, wait, 0)


def _expert_kernel(te_ref, act_ref, tok_ref, h_hbm, gate_ref, wg_ref, wu_ref, wd_ref, y_ref, xbuf, sem, *, tm):
    i = pl.program_id(0)

    @pl.when(act_ref[i] == 1)
    def _():
        _gather_rows(h_hbm, tok_ref, i * tm, tm, xbuf, sem)
        x = xbuf[...].astype(BF16)
        a = _dot(x, wg_ref[...])
        hid = (a * _sigmoid(a)) * _dot(x, wu_ref[...])
        y_ref[...] = _dot(hid.astype(BF16), wd_ref[...]) * gate_ref[...]

    @pl.when(act_ref[i] == 0)
    def _():
        y_ref[...] = jnp.zeros_like(y_ref)


def expert_ffn(h, tile_expert, tile_active, slot_token, slot_gate, w_gate, w_up, w_down, tm):
    t, d = h.shape
    n_e, _, f = w_gate.shape
    n_tiles = tile_expert.shape[0]
    grid_spec = pltpu.PrefetchScalarGridSpec(
        num_scalar_prefetch=3,
        grid=(n_tiles,),
        in_specs=[pl.BlockSpec(memory_space=pl.ANY),
                  pl.BlockSpec((tm, 1), lambda i, te, act, tok: (i, 0)),
                  pl.BlockSpec((None, d, f), lambda i, te, act, tok: (te[i], 0, 0)),
                  pl.BlockSpec((None, d, f), lambda i, te, act, tok: (te[i], 0, 0)),
                  pl.BlockSpec((None, f, d), lambda i, te, act, tok: (te[i], 0, 0))],
        out_specs=pl.BlockSpec((tm, d), lambda i, te, act, tok: (i, 0)),
        scratch_shapes=[pltpu.VMEM((tm, d), F32), pltpu.SemaphoreType.DMA(())],
    )
    return pl.pallas_call(
        functools.partial(_expert_kernel, tm=tm),
        grid_spec=grid_spec,
        out_shape=jax.ShapeDtypeStruct((n_tiles * tm, d), F32),
        compiler_params=_cparams(1), name="expert_ffn",
    )(tile_expert, tile_active, slot_token, h, slot_gate, w_gate, w_up, w_down)


def _combine_kernel(slots_ref, x_ref, gate_ref, y_hbm, o_ref, ybuf, sem, *, tm, n_tok, n_ctx, seq):
    i = pl.program_id(0)
    _gather_rows(y_hbm, slots_ref, i * tm, tm, ybuf.at[0], sem)
    _gather_rows(y_hbm, slots_ref, n_tok + i * tm, tm, ybuf.at[1], sem)
    rid = _row_id(i * tm, n_ctx, seq)
    o_ref[...] = x_ref[...] + gate_ref[pl.ds(rid, 1), :] * (ybuf[0] + ybuf[1])


def moe_combine(x, y_sorted, tok_slots, mods, layer, dims, tm_pref=256):
    t, d = x.shape
    n_ctx, seq = dims
    tm = _tile(math.gcd(n_ctx, seq), tm_pref)
    grid_spec = pltpu.PrefetchScalarGridSpec(
        num_scalar_prefetch=1,
        grid=(t // tm,),
        in_specs=[pl.BlockSpec((tm, d), lambda i, s: (i, 0)),
                  pl.BlockSpec((None, None, MOD_ROWS, d), lambda i, s: (layer, 5, 0, 0)),
                  pl.BlockSpec(memory_space=pl.ANY)],
        out_specs=pl.BlockSpec((tm, d), lambda i, s: (i, 0)),
        scratch_shapes=[pltpu.VMEM((2, tm, d), F32), pltpu.SemaphoreType.DMA(())],
    )
    return pl.pallas_call(
        functools.partial(_combine_kernel, tm=tm, n_tok=t, n_ctx=n_ctx, seq=seq),
        grid_spec=grid_spec,
        out_shape=jax.ShapeDtypeStruct((t, d), F32),
        input_output_aliases={1: 0},
        compiler_params=_cparams(1), name="moe_combine",
    )(tok_slots, x, mods, y_sorted)


def _moe_plan(ids, gates, n_e, tm):
    n_tok = ids.shape[1]
    e_flat = ids.reshape(-1)
    n_pairs = e_flat.shape[0]
    onehot = (e_flat[:, None] == jnp.arange(n_e, dtype=jnp.int32)[None, :]).astype(jnp.int32)
    rank = jnp.take_along_axis(jnp.cumsum(onehot, axis=0), e_flat[:, None], axis=1)[:, 0] - 1
    counts = jnp.sum(onehot, axis=0)
    padded = (counts + tm - 1) // tm * tm
    ends = jnp.cumsum(padded)
    starts = ends - padded
    dest = starts[e_flat] + rank
    n_tiles = n_pairs // tm + n_e
    n_slots = n_tiles * tm
    pair_tok = jnp.arange(n_pairs, dtype=jnp.int32) % n_tok
    slot_token = jnp.zeros((n_slots,), jnp.int32).at[dest].set(pair_tok)
    slot_gate = jnp.zeros((n_slots,), F32).at[dest].set(gates.reshape(-1))
    tile_start = jnp.arange(n_tiles, dtype=jnp.int32) * tm
    tile_expert = jnp.minimum(jnp.searchsorted(ends, tile_start, side="right"), n_e - 1).astype(jnp.int32)
    tile_active = (tile_start < ends[-1]).astype(jnp.int32)
    last_expert = tile_expert[jnp.maximum(ends[-1] // tm - 1, 0)]
    tile_expert = jnp.where(tile_active == 1, tile_expert, last_expert)
    return tile_expert, tile_active, slot_token, slot_gate.reshape(n_slots, 1), dest.astype(jnp.int32)


def moe_layer(x, norm_g, mods, layer, router_w_t, router_b, w_gate, w_up, w_down, dims, tm_pref=256):
    n_e = w_gate.shape[0]
    h, ids, gates = router(x, norm_g, mods, layer, router_w_t, router_b, dims)
    tm = _tile(2 * x.shape[0], tm_pref)
    tile_expert, tile_active, slot_token, slot_gate, dest = _moe_plan(ids, gates, n_e, tm)
    y_sorted = expert_ffn(h, tile_expert, tile_active, slot_token, slot_gate, w_gate, w_up, w_down, tm)
    return moe_combine(x, y_sorted, dest, mods, layer, dims)


def fourier_rwkv_layer(x, mods, layer, norm_g, w_in, w_out, shift_mu, w0, w_up, a0, a_up, g_up,
                       k_k, k_a, r_k, lnx_g, lnx_b, batch, dims, fnet_tabs):
    n_ctx, ctx_len, seq = dims
    rw = w0.shape[1]
    fw = w_in.shape[1] - 3 * rw - w_up.shape[1] - a_up.shape[1] - g_up.shape[0]
    dl, il, gl = w_up.shape[1], a_up.shape[1], g_up.shape[0]
    lora_pad = 128
    rdims = (n_ctx, seq)

    def pad_cols(a, n):
        return jnp.pad(a, ((0, 0), (0, n - a.shape[1])))

    o_lo = fw + 3 * rw
    w_lo = jnp.concatenate([pad_cols(w_in[:, o_lo:o_lo + dl], lora_pad),
                            pad_cols(w_in[:, o_lo + dl:o_lo + dl + il], lora_pad),
                            w_in[:, o_lo + dl + il:]], axis=1).astype(BF16)
    mu = shift_mu.reshape(1, -1)
    mu_lo = jnp.concatenate([pad_cols(mu[:, 3 * rw:3 * rw + dl], lora_pad),
                             pad_cols(mu[:, 3 * rw + dl:3 * rw + dl + il], lora_pad),
                             mu[:, 3 * rw + dl + il:]], axis=1)
    w_up_p = jnp.pad(w_up, ((0, 0), (0, lora_pad - dl), (0, 0))).astype(BF16)
    a_up_p = jnp.pad(a_up, ((0, 0), (0, lora_pad - il), (0, 0))).astype(BF16)

    f = norm_mod_matmul(x, norm_g, mods, layer, 0, 1, w_in[:, :fw].astype(BF16), BF16, rdims)
    prkv = norm_mod_matmul(x, norm_g, mods, layer, 0, 1, w_in[:, fw:o_lo].astype(BF16), F32, rdims)
    plo = norm_mod_matmul(x, norm_g, mods, layer, 0, 1, w_lo, F32, rdims)

    r, v, kk, lw, kd, bd, g, bv = rwkv_terms(
        prkv, plo, mu[:, :3 * rw], mu_lo, w0, w_up_p, a0, a_up_p, g_up.astype(BF16),
        k_k.reshape(1, rw), k_a.reshape(1, rw), r_k.reshape(1, rw), dims, lora_pad)
    o2 = rwkv_scan(r, v, kk, lw, kd, bd, batch, dims)
    y_rw = rwkv_readout(o2, bv, g, lnx_g, lnx_b)

    cos_c, sin_c, cos_x, sin_x, cos_l, sin_l = fnet_tabs
    gc = fw // FNET_GROUPS
    y1, y2 = fnet_channel_dft(f, cos_c, sin_c)
    f_ctx = fnet_sequence_dft(y1, y2, 0, batch, ctx_len, cos_x, sin_x, gc)
    f_lat = fnet_sequence_dft(y1, y2, n_ctx, batch, seq, cos_l, sin_l, gc)
    f_mix = jnp.concatenate([f_ctx, f_lat], axis=0)
    w_out_b = w_out.astype(BF16)
    return matmul_gated_residual([f_mix, y_rw], [w_out_b[:fw], w_out_b[fw:]], x, mods, layer, 2, rdims)


def attention_layer(x, mods, layer, norm_g, w_qkv, w_o, sink, batch, dims, rope_tabs):
    n_ctx, ctx_len, seq = dims
    q_cols = w_o.shape[0]
    kv_cols = (w_qkv.shape[1] - q_cols) // 2
    cos_t, sin_t = rope_tabs
    qkv = norm_mod_matmul(x, norm_g, mods, layer, 0, 1, w_qkv.astype(BF16), BF16, (n_ctx, seq),
                          rope=(q_cols + kv_cols, cos_t, sin_t), tn_pref=cos_t.shape[1])
    att = attention(qkv, sink, batch, dims, q_cols, kv_cols)
    return matmul_gated_residual([att], [w_o.astype(BF16)], x, mods, layer, 2, (n_ctx, seq))


def kernel(x, c, ctx, c_ctx, ada_down, ada_up, ada_bias, norm1_g, norm2_g, final_g, mix_w_in, mix_w_out, shift_mu, decay_w0, decay_up, iclr_a0, iclr_up, gate_up, k_k, k_a, r_k, lnx_g, lnx_b, att_w_qkv, att_w_o, att_sink, router_w, router_b, exp_w_gate, exp_w_up, exp_w_down):
    batch, seq, d = x.shape
    ctx_len = ctx.shape[1]
    depth = ada_down.shape[0]
    n_ctx = batch * ctx_len
    assert batch + 1 <= MOD_ROWS
    dims = (n_ctx, ctx_len, seq)
    xs = jnp.concatenate([ctx.reshape(n_ctx, d), x.reshape(batch * seq, d)], axis=0)

    cvec = jnp.concatenate([c_ctx[None, :], c, jnp.zeros((MOD_ROWS - 1 - batch, d), F32)], axis=0)
    mods = adaln_all(cvec, ada_down, ada_up, ada_bias)
    mods = mods.reshape(depth, MOD_ROWS, N_MOD, d).transpose(0, 2, 1, 3)

    fw = mix_w_in.shape[2] - 3 * decay_w0.shape[2] - decay_up.shape[2] - iclr_up.shape[2] - gate_up.shape[1]
    gc = fw // FNET_GROUPS
    fnet_tabs = _dft_tables(gc) + _dft_tables(ctx_len) + _dft_tables(seq)
    kv_cols = (att_w_qkv.shape[2] - att_w_o.shape[1]) // 2
    rope_tabs = _rope_tables(seq, min(512, kv_cols))
    router_w_t = router_w.T

    for layer in range(depth):
        j = layer // 2
        if layer % 2 == 0:
            xs = fourier_rwkv_layer(xs, mods, layer, norm1_g[layer], mix_w_in[j], mix_w_out[j], shift_mu[j],
                                    decay_w0[j], decay_up[j], iclr_a0[j], iclr_up[j], gate_up[j],
                                    k_k[j], k_a[j], r_k[j], lnx_g[j], lnx_b[j], batch, dims, fnet_tabs)
        else:
            xs = attention_layer(xs, mods, layer, norm1_g[layer], att_w_qkv[j], att_w_o[j], att_sink[j],
                                 batch, dims, rope_tabs)
        xs = moe_layer(xs, norm2_g[layer], mods, layer, router_w_t, router_b,
                       exp_w_gate[layer].astype(BF16), exp_w_up[layer].astype(BF16),
                       exp_w_down[layer].astype(BF16), (n_ctx, seq))
    return final_norm(xs, final_g, n_ctx).reshape(batch, seq, d)
```

```python
import functools
import math

import jax
import jax.numpy as jnp
from jax import lax
from jax.experimental import pallas as pl
from jax.experimental.pallas import tpu as pltpu

F32 = jnp.float32
BF16 = jnp.bfloat16
HIGHEST = lax.Precision.HIGHEST

GRID_W = 64
NORM_EPS = 1e-6
GN_EPS = 64e-5
L2_EPS = 1e-12
N_MOD = 6
FNET_GROUPS = 4
HEAD = 64
ATT_GROUP = 8
WINDOW = 128
QBLOCK = 128
ROPE_THETA = 10000.0
N_EXPERT_GROUPS = 4
EXPERTS_PER_GROUP = 4
CHUNK = 64
MOD_ROWS = 8

VMEM_LIMIT = 56 * 1024 * 1024


def _cparams(n_axes):
    return pltpu.CompilerParams(dimension_semantics=("arbitrary",) * n_axes,
                                vmem_limit_bytes=VMEM_LIMIT)


def _tile(n, pref):
    t = min(n, pref)
    while n % t:
        t //= 2
    return t


def _row_id(r0, n_ctx, seq):
    return jnp.where(r0 < n_ctx, 0, 1 + (r0 - n_ctx) // seq)


def _dot(a, b):
    return jnp.dot(a, b, preferred_element_type=F32)


def _dot_hi(a, b):
    return jnp.dot(a, b, preferred_element_type=F32, precision=HIGHEST)


def _dot_t0(a, b):
    return lax.dot_general(a, b, (((0,), (0,)), ((), ())), preferred_element_type=F32)


def _dot_t1(a, b):
    return lax.dot_general(a, b, (((1,), (1,)), ((), ())), preferred_element_type=F32)


def _dot_hi_t1(a, b):
    return lax.dot_general(a, b, (((1,), (1,)), ((), ())), preferred_element_type=F32, precision=HIGHEST)


def _sigmoid(x):
    return 1.0 / (1.0 + jnp.exp(-x))


def _norm_mod(x, g, sh, sc):
    ms = jnp.mean(x * x, axis=-1, keepdims=True)
    y = x * lax.rsqrt(ms + NORM_EPS) * g
    return y * (1.0 + sc) + sh


def _adaln_kernel(cv_ref, down_ref, up_ref, bias_ref, o_ref, t_scr):
    @pl.when(pl.program_id(1) == 0)
    def _():
        cv = cv_ref[...]
        s = cv * _sigmoid(cv)
        t_scr[...] = _dot(s.astype(BF16), down_ref[...].astype(BF16)).astype(BF16)
    o_ref[...] = _dot(t_scr[...], up_ref[...].astype(BF16)) + bias_ref[...]


def adaln_all(cvec, ada_down, ada_up, ada_bias):
    depth, d, rank = ada_down.shape
    n = ada_up.shape[2]
    tn = _tile(n, 2048)
    bias = ada_bias.reshape(depth, 1, n)
    return pl.pallas_call(
        _adaln_kernel,
        grid=(depth, n // tn),
        in_specs=[pl.BlockSpec((MOD_ROWS, d), lambda l, j: (0, 0)),
                  pl.BlockSpec((None, d, rank), lambda l, j: (l, 0, 0)),
                  pl.BlockSpec((None, rank, tn), lambda l, j: (l, 0, j)),
                  pl.BlockSpec((None, 1, tn), lambda l, j: (l, 0, j))],
        out_specs=pl.BlockSpec((None, MOD_ROWS, tn), lambda l, j: (l, 0, j)),
        out_shape=jax.ShapeDtypeStruct((depth, MOD_ROWS, n), F32),
        scratch_shapes=[pltpu.VMEM((MOD_ROWS, rank), BF16)],
        compiler_params=_cparams(2), name="adaln",
    )(cvec, ada_down, ada_up, bias)


def _nmm_kernel(*refs, tm, n_ctx, seq, rope_cols):
    if rope_cols:
        x_ref, g_ref, sh_ref, sc_ref, w_ref, cos_ref, sin_ref, o_ref, h_scr = refs
    else:
        x_ref, g_ref, sh_ref, sc_ref, w_ref, o_ref, h_scr = refs
    i = pl.program_id(0)
    j = pl.program_id(1)
    r0 = i * tm

    @pl.when(j == 0)
    def _():
        rid = _row_id(r0, n_ctx, seq)
        h = _norm_mod(x_ref[...], g_ref[...], sh_ref[pl.ds(rid, 1), :], sc_ref[pl.ds(rid, 1), :])
        h_scr[...] = h.astype(BF16)

    acc = _dot(h_scr[...], w_ref[...])
    if rope_cols:
        tn = acc.shape[1]
        do_rope = jnp.logical_and(r0 >= n_ctx, j * tn < rope_cols)

        @pl.when(do_rope)
        def _():
            lane = lax.broadcasted_iota(jnp.int32, acc.shape, 1)
            swapped = jnp.where(lane % 32 < 16, pltpu.roll(acc, tn - 16, 1), pltpu.roll(acc, 16, 1))
            o_ref[...] = (acc * cos_ref[...] + swapped * sin_ref[...]).astype(o_ref.dtype)

        @pl.when(jnp.logical_not(do_rope))
        def _():
            o_ref[...] = acc.astype(o_ref.dtype)
    else:
        o_ref[...] = acc.astype(o_ref.dtype)


def norm_mod_matmul(x, g, mods, layer, k_shift, k_scale, w, out_dtype, dims, rope=None, tm_pref=512, tn_pref=512):
    t, d = x.shape
    n = w.shape[1]
    n_ctx, seq = dims
    tm = _tile(math.gcd(n_ctx, seq), tm_pref)
    tn = _tile(n, tn_pref)
    in_specs = [pl.BlockSpec((tm, d), lambda i, j: (i, 0)),
                pl.BlockSpec((1, d), lambda i, j: (0, 0)),
                pl.BlockSpec((None, None, MOD_ROWS, d), lambda i, j: (layer, k_shift, 0, 0)),
                pl.BlockSpec((None, None, MOD_ROWS, d), lambda i, j: (layer, k_scale, 0, 0)),
                pl.BlockSpec((d, tn), lambda i, j: (0, j))]
    args = [x, g.reshape(1, d), mods, mods, w]
    rope_cols = 0
    if rope is not None:
        rope_cols, cos_t, sin_t = rope
        assert rope_cols % tn == 0 and cos_t.shape == (seq, tn)
        pos_map = lambda i, j: (jnp.maximum(i * tm - n_ctx, 0) % seq // tm, 0)
        in_specs += [pl.BlockSpec((tm, tn), pos_map), pl.BlockSpec((tm, tn), pos_map)]
        args += [cos_t, sin_t]
    return pl.pallas_call(
        functools.partial(_nmm_kernel, tm=tm, n_ctx=n_ctx, seq=seq, rope_cols=rope_cols),
        grid=(t // tm, n // tn),
        in_specs=in_specs,
        out_specs=pl.BlockSpec((tm, tn), lambda i, j: (i, j)),
        out_shape=jax.ShapeDtypeStruct((t, n), out_dtype),
        scratch_shapes=[pltpu.VMEM((tm, d), BF16)],
        compiler_params=_cparams(2), name="norm_mod_matmul",
    )(*args)


def _mmres_kernel(*refs, n_a, tm, n_ctx, seq):
    a_refs = refs[:n_a]
    w_refs = refs[n_a:2 * n_a]
    x_ref, gate_ref, o_ref = refs[2 * n_a:]
    acc = _dot(a_refs[0][...], w_refs[0][...])
    for a_ref, w_ref in zip(a_refs[1:], w_refs[1:]):
        acc += _dot(a_ref[...], w_ref[...])
    rid = _row_id(pl.program_id(0) * tm, n_ctx, seq)
    o_ref[...] = x_ref[...] + gate_ref[pl.ds(rid, 1), :] * acc


def matmul_gated_residual(a_list, w_list, x, mods, layer, k_gate, dims, tm_pref=512, tn_pref=512):
    t, d = x.shape
    n_ctx, seq = dims
    tm = _tile(math.gcd(n_ctx, seq), tm_pref)
    tn = _tile(d, tn_pref)
    n_a = len(a_list)
    in_specs = [pl.BlockSpec((tm, a.shape[1]), lambda i, j: (i, 0)) for a in a_list]
    in_specs += [pl.BlockSpec((w.shape[0], tn), lambda i, j: (0, j)) for w in w_list]
    in_specs += [pl.BlockSpec((tm, tn), lambda i, j: (i, j)),
                 pl.BlockSpec((None, None, MOD_ROWS, tn), lambda i, j: (layer, k_gate, 0, j))]
    return pl.pallas_call(
        functools.partial(_mmres_kernel, n_a=n_a, tm=tm, n_ctx=n_ctx, seq=seq),
        grid=(t // tm, d // tn),
        in_specs=in_specs,
        out_specs=pl.BlockSpec((tm, tn), lambda i, j: (i, j)),
        out_shape=jax.ShapeDtypeStruct((t, d), F32),
        input_output_aliases={2 * n_a: 0},
        compiler_params=_cparams(2), name="matmul_gated_residual",
    )(*a_list, *w_list, x, mods)


def _final_norm_kernel(x_ref, g_ref, o_ref):
    x = x_ref[...]
    ms = jnp.mean(x * x, axis=-1, keepdims=True)
    o_ref[...] = x * lax.rsqrt(ms + NORM_EPS) * g_ref[...]


def final_norm(x, g, n_ctx, tm_pref=512):
    t, d = x.shape
    tm = _tile(math.gcd(n_ctx, t - n_ctx), tm_pref)
    off = n_ctx // tm
    return pl.pallas_call(
        _final_norm_kernel,
        grid=((t - n_ctx) // tm,),
        in_specs=[pl.BlockSpec((tm, d), lambda i: (i + off, 0)),
                  pl.BlockSpec((1, d), lambda i: (0, 0))],
        out_specs=pl.BlockSpec((tm, d), lambda i: (i, 0)),
        out_shape=jax.ShapeDtypeStruct((t - n_ctx, d), F32),
        compiler_params=_cparams(1), name="final_norm",
    )(x, g.reshape(1, d))


def _dft_tables(n, scale=1.0):
    j = jnp.arange(n, dtype=jnp.int32)
    m = (j[:, None] * j[None, :]) % n
    ang = m.astype(F32) * (2.0 * math.pi / n)
    return (jnp.cos(ang) * scale).astype(BF16), (jnp.sin(ang) * scale).astype(BF16)


def _fnet_ch_kernel(x_ref, c_ref, s_ref, y1_ref, y2_ref):
    x = x_ref[...]
    y1_ref[...] = _dot(x, c_ref[...]).astype(y1_ref.dtype)
    y2_ref[...] = _dot(x, s_ref[...]).astype(y2_ref.dtype)


def fnet_channel_dft(f, cos_c, sin_c, tm_pref=1024):
    t, width = f.shape
    gc = width // FNET_GROUPS
    tm = _tile(t, tm_pref)
    spec = pl.BlockSpec((tm, gc), lambda i, g: (i, g))
    tab = pl.BlockSpec((gc, gc), lambda i, g: (0, 0))
    return pl.pallas_call(
        _fnet_ch_kernel,
        grid=(t // tm, FNET_GROUPS),
        in_specs=[spec, tab, tab],
        out_specs=[spec, spec],
        out_shape=[jax.ShapeDtypeStruct((t, width), BF16)] * 2,
        compiler_params=_cparams(2), name="fnet_channel_dft",
    )(f, cos_c, sin_c)


def _fnet_seq_kernel(c_ref, s_ref, y1_ref, y2_ref, o_ref, acc_ref, *, scale):
    k = pl.program_id(2)

    @pl.when(k == 0)
    def _():
        acc_ref[...] = jnp.zeros_like(acc_ref)

    acc_ref[...] += _dot(c_ref[...], y1_ref[...]) - _dot(s_ref[...], y2_ref[...])

    @pl.when(k == pl.num_programs(2) - 1)
    def _():
        o_ref[...] = (acc_ref[...] * scale).astype(o_ref.dtype)


def fnet_sequence_dft(y1, y2, row0, n_seq, length, cos_l, sin_l, gc, t_pref=512):
    width = y1.shape[1]
    tm = _tile(length, t_pref)
    tk = _tile(math.gcd(length, row0) if row0 else length, t_pref)
    nb = length // tm
    nk = length // tk
    scale = 1.0 / math.sqrt(length * gc)
    rb0 = row0 // tk
    y_spec = pl.BlockSpec((tk, width), lambda b, i, k: (rb0 + b * nk + k, 0))
    return pl.pallas_call(
        functools.partial(_fnet_seq_kernel, scale=scale),
        grid=(n_seq, nb, nk),
        in_specs=[pl.BlockSpec((tm, tk), lambda b, i, k: (i, k)),
                  pl.BlockSpec((tm, tk), lambda b, i, k: (i, k)),
                  y_spec, y_spec],
        out_specs=pl.BlockSpec((tm, width), lambda b, i, k: (b * nb + i, 0)),
        out_shape=jax.ShapeDtypeStruct((n_seq * length, width), BF16),
        scratch_shapes=[pltpu.VMEM((tm, width), F32)],
        compiler_params=_cparams(3), name="fnet_sequence_dft",
    )(cos_l, sin_l, y1, y2)


def _head_sum(x, ones_bd):
    hi = x.astype(BF16)
    lo = (x - hi.astype(F32)).astype(BF16)
    return _dot(hi, ones_bd) + _dot(lo, ones_bd)


def _shift(x, prev_row, next_row, mu, pos, length):
    rows = x.shape[0]
    ridx = lax.broadcasted_iota(jnp.int32, (rows, 1), 0)
    prev = jnp.where(ridx == 0, prev_row, pltpu.roll(x, 1, 0))
    nxt = jnp.where(ridx == rows - 1, next_row, pltpu.roll(x, rows - 1, 0))
    prev = jnp.where(pos == 0, 0.0, prev)
    nxt = jnp.where(pos == length - 1, 0.0, nxt)
    return x + mu * (0.5 * (prev + nxt) - x)


def _rwkv_terms_kernel(r_ref, k_ref, v_ref, rp_ref, kp_ref, vp_ref, rn_ref, kn_ref, vn_ref,
                       lo_ref, lop_ref, lon_ref, mur_ref, muk_ref, muv_ref, mulo_ref,
                       w0_ref, wup_ref, a0_ref, aup_ref, gup_ref, kk_ref, ka_ref, rk_ref, ones_ref,
                       r_o, v_o, kk_o, lw_o, kd_o, bd_o, g_o, bv_o, *, tm, n_ctx, ctx_len, seq, lora_pad):
    i = pl.program_id(0)
    r0 = i * tm
    ridx = lax.broadcasted_iota(jnp.int32, (tm, 1), 0) + r0
    in_ctx = r0 < n_ctx
    length = jnp.where(in_ctx, ctx_len, seq)
    pos = jnp.where(in_ctx, ridx % ctx_len, (ridx - n_ctx) % seq)

    def sh(ref, pref, nref, mu_ref):
        return _shift(ref[...], pref[7:8, :], nref[0:1, :], mu_ref[...], pos, length)

    r = sh(r_ref, rp_ref, rn_ref, mur_ref)
    k = sh(k_ref, kp_ref, kn_ref, muk_ref)
    v = sh(v_ref, vp_ref, vn_ref, muv_ref)
    lo = sh(lo_ref, lop_ref, lon_ref, mulo_ref)
    w_in = jnp.tanh(lo[:, :lora_pad]).astype(BF16)
    a_in = lo[:, lora_pad:2 * lora_pad].astype(BF16)
    g_in = _sigmoid(lo[:, 2 * lora_pad:]).astype(BF16)
    ones_bd = ones_ref[...]

    kk = k * kk_ref[...]
    kk = kk * lax.rsqrt(_head_sum(kk * kk, ones_bd) + L2_EPS)
    ksum = jnp.zeros_like(k)
    for d in range(2):
        w_logit = w0_ref[d:d + 1, :] + _dot(w_in, wup_ref[d])
        lw_o[d] = -math.exp(-0.5) * _sigmoid(w_logit)
        a = _sigmoid(a0_ref[d:d + 1, :] + _dot(a_in, aup_ref[d]))
        k_d = k * (1.0 + (a - 1.0) * ka_ref[...])
        kd_o[d] = k_d
        bd_o[d] = kk * a
        ksum = ksum + k_d
    g_o[...] = _dot(g_in, gup_ref[...])
    bv_o[...] = _head_sum(r * ksum * rk_ref[...], ones_bd) * v
    r_o[...] = r
    v_o[...] = v
    kk_o[...] = kk


def rwkv_terms(prkv, plo, mu_rkv, mu_lo, w0, w_up, a0, a_up, g_up, k_k, k_a, r_k, dims, lora_pad, tm_pref=256):
    t = prkv.shape[0]
    rw = prkv.shape[1] // 3
    n_ctx, ctx_len, seq = dims
    tm = _tile(math.gcd(n_ctx, seq), tm_pref)
    tn = _tile(rw, 512)
    nj = rw // tn
    lo_w = plo.shape[1]
    hb = tm // 8
    last8 = t // 8 - 1

    def main(c):
        return pl.BlockSpec((tm, tn), lambda i, j: (i, c * nj + j))

    def prev(c):
        return pl.BlockSpec((8, tn), lambda i, j: (jnp.maximum(i * hb - 1, 0), c * nj + j))

    def nxt(c):
        return pl.BlockSpec((8, tn), lambda i, j: (jnp.minimum((i + 1) * hb, last8), c * nj + j))

    def vec(c=0):
        return pl.BlockSpec((1, tn), lambda i, j: (0, c * nj + j))

    ones_bd = (jnp.arange(tn)[:, None] // HEAD == jnp.arange(tn)[None, :] // HEAD).astype(BF16)
    out_tok = pl.BlockSpec((tm, tn), lambda i, j: (i, j))
    out_dir = pl.BlockSpec((2, tm, tn), lambda i, j: (0, i, j))
    tok = jax.ShapeDtypeStruct((t, rw), F32)
    tok2 = jax.ShapeDtypeStruct((2, t, rw), F32)
    return pl.pallas_call(
        functools.partial(_rwkv_terms_kernel, tm=tm, n_ctx=n_ctx, ctx_len=ctx_len, seq=seq, lora_pad=lora_pad),
        grid=(t // tm, nj),
        in_specs=[main(0), main(1), main(2), prev(0), prev(1), prev(2), nxt(0), nxt(1), nxt(2),
                  pl.BlockSpec((tm, lo_w), lambda i, j: (i, 0)),
                  pl.BlockSpec((8, lo_w), lambda i, j: (jnp.maximum(i * hb - 1, 0), 0)),
                  pl.BlockSpec((8, lo_w), lambda i, j: (jnp.minimum((i + 1) * hb, last8), 0)),
                  vec(0), vec(1), vec(2),
                  pl.BlockSpec((1, lo_w), lambda i, j: (0, 0)),
                  pl.BlockSpec((2, tn), lambda i, j: (0, j)),
                  pl.BlockSpec((2, lora_pad, tn), lambda i, j: (0, 0, j)),
                  pl.BlockSpec((2, tn), lambda i, j: (0, j)),
                  pl.BlockSpec((2, lora_pad, tn), lambda i, j: (0, 0, j)),
                  pl.BlockSpec((g_up.shape[0], tn), lambda i, j: (0, j)),
                  vec(), vec(), vec(),
                  pl.BlockSpec((tn, tn), lambda i, j: (0, 0))],
        out_specs=[out_tok, out_tok, out_tok, out_dir, out_dir, out_dir, out_tok, out_tok],
        out_shape=[tok, tok, tok, tok2, tok2, tok2, tok, tok],
        compiler_params=_cparams(2), name="rwkv_terms",
    )(prkv, prkv, prkv, prkv, prkv, prkv, prkv, prkv, prkv, plo, plo, plo,
      mu_rkv, mu_rkv, mu_rkv, mu_lo, w0, w_up, a0, a_up, g_up, k_k, k_a, r_k, ones_bd)


def _rwkv_scan_kernel(r_ref, v_ref, kk_ref, lw_ref, kd_ref, bd_ref, o_ref,
                      s_scr, rt_s, at_s, kt_s, bt_s, bh_s, kh_s, vb_s, pd_s, *, n_heads, group):
    d = pl.program_id(1)
    step = pl.program_id(2)
    c = CHUNK

    @pl.when(step == 0)
    def _():
        s_scr[...] = jnp.zeros_like(s_scr)

    row = lax.broadcasted_iota(jnp.int32, (c, c), 0)
    col = lax.broadcasted_iota(jnp.int32, (c, c), 1)
    fwd = d == 0
    ahead = (row - col) * jnp.where(fwd, 1, -1)
    incl = ahead >= 0
    strict = ahead > 0
    eye = row == col
    eye_f = jnp.where(eye, 1.0, 0.0)

    tri = jnp.where(incl, 1.0, 0.0).astype(BF16)
    lw = lw_ref[...]
    lw_hi = lw.astype(BF16)
    lw_lo = (lw - lw_hi.astype(F32)).astype(BF16)
    l_inc = _dot(tri, lw_hi) + _dot(tri, lw_lo)
    l_tot = jnp.where(fwd, l_inc[c - 1:c, :], l_inc[0:1, :])
    e_neg = jnp.exp(-l_inc)
    e_rem = jnp.exp(l_tot - l_inc)
    kd = kd_ref[...]
    bd = bd_ref[...]
    rt_s[...] = (r_ref[...] * jnp.exp(l_inc)).astype(BF16)
    at_s[...] = (-kk_ref[...] * jnp.exp(l_inc - lw)).astype(BF16)
    kt_s[...] = (kd * e_neg).astype(BF16)
    bt_s[...] = (bd * e_neg).astype(BF16)
    bh_s[...] = (bd * e_rem).astype(BF16)
    kh_s[...] = (kd * e_rem).astype(BF16)
    vb_s[...] = v_ref[...].astype(BF16)
    pd_s[...] = jnp.exp(l_tot)

    gw = group * HEAD

    def group_body(gi, carry):
        sl = pl.ds(pl.multiple_of(gi * gw, gw), gw)
        rt_g, at_g, kt_g, bt_g, bh_g, kh_g, v_g = (s[:, sl] for s in (rt_s, at_s, kt_s, bt_s, bh_s, kh_s, vb_s))
        pd_g = pd_s[:, sl]
        heads = range(group)

        def hd(a, q):
            return a[:, q * HEAD:(q + 1) * HEAD]

        lhs = [jnp.concatenate([hd(at_g, q), hd(rt_g, q)], axis=0) for q in heads]
        gb = [_dot_t1(lhs[q], hd(bt_g, q)) for q in heads]
        gk = [_dot_t1(lhs[q], hd(kt_g, q)) for q in heads]
        a_ab = [jnp.where(strict, gb[q][:c], 0.0) for q in heads]
        a_rb = [jnp.where(incl, gb[q][c:], 0.0).astype(BF16) for q in heads]
        a_k = [jnp.concatenate([jnp.where(strict, gk[q][:c], 0.0), jnp.where(incl, gk[q][c:], 0.0)],
                               axis=0).astype(BF16) for q in heads]
        vv = [_dot(a_k[q], hd(v_g, q)) for q in heads]
        kv = [_dot_t0(hd(kh_g, q), hd(v_g, q)) for q in heads]
        x = [a.astype(BF16) for a in a_ab]
        tinv = [eye_f + a for a in a_ab]
        for _ in range(int(math.log2(c)) - 1):
            x = [_dot(x[q], x[q]).astype(BF16) for q in heads]
            tinv = [tinv[q] + _dot(tinv[q].astype(BF16), x[q]) for q in heads]
        tb = [t.astype(BF16) for t in tinv]
        wt = [_dot(tb[q], hd(at_g, q)).astype(BF16) for q in heads]
        u0 = [_dot(tb[q], vv[q][:c].astype(BF16)).astype(BF16) for q in heads]
        qm = [hd(rt_g, q).astype(F32) + _dot(a_rb[q], wt[q]) for q in heads]
        o0 = [_dot(a_rb[q], u0[q]) + vv[q][c:] for q in heads]
        m = [eye_f * hd(pd_g, q) + _dot_t0(hd(bh_g, q), wt[q]) for q in heads]
        nn = [_dot_t0(hd(bh_g, q), u0[q]) + kv[q] for q in heads]
        h0 = gi * group
        st = s_scr[pl.ds(h0, group)]
        res = [_dot(jnp.concatenate([qm[q], m[q]], axis=0).astype(BF16), st[q].astype(BF16)) for q in heads]
        s_scr[pl.ds(h0, group)] = jnp.stack([res[q][c:] + nn[q] for q in heads])
        o_ref[:, sl] = jnp.concatenate([res[q][:c] + o0[q] for q in heads], axis=1)
        return carry

    lax.fori_loop(0, n_heads // group, group_body, 0)


def rwkv_scan(r, v, kk, lw, kd, bd, batch, dims):
    t, rw = r.shape
    n_ctx, ctx_len, seq = dims
    c = CHUNK
    n_cc = ctx_len // c
    n_lc = seq // c
    steps = n_cc + n_lc

    def blk(b, d, s):
        ctx_i = jnp.where(d == 0, s, n_cc - 1 - s)
        lat_i = jnp.where(d == 0, s - n_cc, n_lc - 1 - (s - n_cc))
        return jnp.where(s < n_cc, b * n_cc + ctx_i, n_ctx // c + b * n_lc + lat_i)

    shared = pl.BlockSpec((c, rw), lambda b, d, s: (blk(b, d, s), 0))
    per_dir = pl.BlockSpec((None, c, rw), lambda b, d, s: (d, blk(b, d, s), 0))
    n_heads = rw // HEAD
    group = 16 if n_heads % 16 == 0 else 2
    prep = pltpu.VMEM((c, rw), BF16)
    return pl.pallas_call(
        functools.partial(_rwkv_scan_kernel, n_heads=n_heads, group=group),
        grid=(batch, 2, steps),
        in_specs=[shared, shared, shared, per_dir, per_dir, per_dir],
        out_specs=per_dir,
        out_shape=jax.ShapeDtypeStruct((2, t, rw), F32),
        scratch_shapes=[pltpu.VMEM((n_heads, HEAD, HEAD), F32)] + [prep] * 7 + [pltpu.VMEM((1, rw), F32)],
        compiler_params=_cparams(3), name="rwkv_scan",
    )(r, v, kk, lw, kd, bd)


def _rwkv_readout_kernel(o_ref, bv_ref, g_ref, lg_ref, lb_ref, ones_ref, y_ref):
    o = o_ref[0] + o_ref[1]
    ones_bd = ones_ref[...]
    mu = _head_sum(o, ones_bd) * (1.0 / HEAD)
    dev = o - mu
    var = _head_sum(dev * dev, ones_bd) * (1.0 / HEAD)
    on = dev * lax.rsqrt(var + GN_EPS) * lg_ref[...] + lb_ref[...]
    y_ref[...] = ((on + bv_ref[...]) * g_ref[...]).astype(y_ref.dtype)


def rwkv_readout(o2, bv, g, lnx_g, lnx_b, tm_pref=512):
    _, t, rw = o2.shape
    tm = _tile(t, tm_pref)
    tn = _tile(rw, 512)
    ones_bd = (jnp.arange(tn)[:, None] // HEAD == jnp.arange(tn)[None, :] // HEAD).astype(BF16)
    tok = pl.BlockSpec((tm, tn), lambda i, j: (i, j))
    vec = pl.BlockSpec((1, tn), lambda i, j: (0, j))
    return pl.pallas_call(
        _rwkv_readout_kernel,
        grid=(t // tm, rw // tn),
        in_specs=[pl.BlockSpec((2, tm, tn), lambda i, j: (0, i, j)), tok, tok, vec, vec,
                  pl.BlockSpec((tn, tn), lambda i, j: (0, 0))],
        out_specs=tok,
        out_shape=jax.ShapeDtypeStruct((t, rw), BF16),
        compiler_params=_cparams(2), name="rwkv_readout",
    )(o2, bv, g, lnx_g.reshape(1, rw), lnx_b.reshape(1, rw), ones_bd)


def _attend(q, k_all, v_all, valid, sink):
    s = _dot_t1(q, k_all) * (HEAD ** -0.5)
    if valid is not None:
        s = jnp.where(valid, s, -jnp.inf)
    m = jnp.maximum(jnp.max(s, axis=-1, keepdims=True), sink)
    p = jnp.exp(s - m)
    denom = jnp.sum(p, axis=-1, keepdims=True) + jnp.exp(sink - m)
    return _dot(p.astype(BF16), v_all) / denom


def _attn_kernel(sink_ref, q_ref, kp_ref, kc_ref, kn_ref, vp_ref, vc_ref, vn_ref, kx_ref, vx_ref, o_ref,
                 *, seq, n_ctx_keys, n_ctx_blocks):
    pair = pl.program_id(2)
    qi = pl.program_id(1) - n_ctx_blocks
    span = QBLOCK + 2 * WINDOW
    rr = lax.broadcasted_iota(jnp.int32, (QBLOCK, span + n_ctx_keys), 0)
    cc = lax.broadcasted_iota(jnp.int32, (QBLOCK, span + n_ctx_keys), 1)
    key_pos = qi * QBLOCK - WINDOW + cc
    off = cc - WINDOW - rr
    in_band = (jnp.abs(off) <= WINDOW) & (key_pos >= 0) & (key_pos < seq) & (qi >= 0)
    valid = jnp.logical_or(cc >= span, in_band)
    outs = []
    for kv in range(2):
        hs = slice(kv * HEAD, (kv + 1) * HEAD)
        k_all = jnp.concatenate([kp_ref[:, hs], kc_ref[:, hs], kn_ref[:, hs], kx_ref[:, hs]], axis=0)
        v_all = jnp.concatenate([vp_ref[:, hs], vc_ref[:, hs], vn_ref[:, hs], vx_ref[:, hs]], axis=0)
        for g in range(ATT_GROUP):
            hq = kv * ATT_GROUP + g
            q = q_ref[:, hq * HEAD:(hq + 1) * HEAD]
            outs.append(_attend(q, k_all, v_all, valid, sink_ref[pair * 2 * ATT_GROUP + hq]))
    o_ref[...] = jnp.concatenate(outs, axis=1).astype(o_ref.dtype)


def attention(qkv, sink, batch, dims, q_cols, kv_cols):
    t = qkv.shape[0]
    n_ctx, ctx_len, seq = dims
    n_pairs = kv_cols // (2 * HEAD)
    qw = 2 * ATT_GROUP * HEAD
    kw = 2 * HEAD
    nqb = seq // QBLOCK
    ncb = ctx_len // QBLOCK
    lat0 = n_ctx // QBLOCK
    kcol = q_cols // kw
    vcol = (q_cols + kv_cols) // kw
    smem = pl.BlockSpec(memory_space=pltpu.SMEM)

    def q_map(b, i, p):
        return (jnp.where(i < ncb, b * ncb + i, lat0 + b * nqb + i - ncb), p)

    def band(col0, shift):
        def imap(b, i, p):
            return (lat0 + b * nqb + jnp.clip(i - ncb + shift, 0, nqb - 1), col0 + p)
        return pl.BlockSpec((QBLOCK, kw), imap)

    return pl.pallas_call(
        functools.partial(_attn_kernel, seq=seq, n_ctx_keys=ctx_len, n_ctx_blocks=ncb),
        grid=(batch, ncb + nqb, n_pairs),
        in_specs=[smem,
                  pl.BlockSpec((QBLOCK, qw), q_map),
                  band(kcol, -1), band(kcol, 0), band(kcol, 1),
                  band(vcol, -1), band(vcol, 0), band(vcol, 1),
                  pl.BlockSpec((ctx_len, kw), lambda b, i, p: (b, kcol + p)),
                  pl.BlockSpec((ctx_len, kw), lambda b, i, p: (b, vcol + p))],
        out_specs=pl.BlockSpec((QBLOCK, qw), q_map),
        out_shape=jax.ShapeDtypeStruct((t, q_cols), BF16),
        compiler_params=_cparams(3), name="attention",
    )(sink, qkv, qkv, qkv, qkv, qkv, qkv, qkv, qkv, qkv)


def _rope_tables(seq, width):
    pos = jnp.arange(seq, dtype=jnp.int32)
    row_pos = (pos // GRID_W).astype(F32)
    col_pos = (pos % GRID_W).astype(F32)
    half = HEAD // 2
    inv_freq = ROPE_THETA ** (-jnp.arange(0, half, 2, dtype=F32) / half)
    ang_r = row_pos[:, None] * inv_freq
    ang_c = col_pos[:, None] * inv_freq
    cos64 = jnp.concatenate([jnp.cos(ang_r), jnp.cos(ang_r), jnp.cos(ang_c), jnp.cos(ang_c)], axis=-1)
    sin64 = jnp.concatenate([-jnp.sin(ang_r), jnp.sin(ang_r), -jnp.sin(ang_c), jnp.sin(ang_c)], axis=-1)
    reps = width // HEAD
    return jnp.tile(cos64, (1, reps)), jnp.tile(sin64, (1, reps))


def _router_kernel(x_ref, g_ref, sh_ref, sc_ref, rw_ref, rb_ref, h_ref, ids_ref, gts_ref, *, tm, n_ctx, seq):
    rid = _row_id(pl.program_id(0) * tm, n_ctx, seq)
    h = _norm_mod(x_ref[...], g_ref[...], sh_ref[pl.ds(rid, 1), :], sc_ref[pl.ds(rid, 1), :])
    h_ref[...] = h
    logits = _dot_hi_t1(rw_ref[...], h)
    e = jnp.exp(logits - jnp.max(logits, axis=0, keepdims=True))
    probs = e / jnp.sum(e, axis=0, keepdims=True)
    sel = probs + rb_ref[...]
    n_g, per = N_EXPERT_GROUPS, EXPERTS_PER_GROUP

    def row(a, r):
        return a[r:r + 1, :]

    scores = []
    for gi in range(n_g):
        a, b, c, d = (row(sel, gi * per + r) for r in range(per))
        hi1, lo1 = jnp.maximum(a, b), jnp.minimum(a, b)
        hi2, lo2 = jnp.maximum(c, d), jnp.minimum(c, d)
        scores.append(jnp.maximum(hi1, hi2) + jnp.maximum(jnp.minimum(hi1, hi2), jnp.maximum(lo1, lo2)))
    best = scores[0]
    bg = jnp.zeros_like(best, dtype=jnp.int32)
    for gi in range(1, n_g):
        better = scores[gi] > best
        best = jnp.where(better, scores[gi], best)
        bg = jnp.where(better, gi, bg)
    in_sel = []
    in_prob = []
    for r in range(per):
        s_r = row(sel, r)
        p_r = row(probs, r)
        for gi in range(1, n_g):
            s_r = jnp.where(bg == gi, row(sel, gi * per + r), s_r)
            p_r = jnp.where(bg == gi, row(probs, gi * per + r), p_r)
        in_sel.append(s_r)
        in_prob.append(p_r)

    def argmax_first(vals, exclude):
        bv = None
        for r in range(per):
            v = vals[r] if exclude is None else jnp.where(exclude == r, -jnp.inf, vals[r])
            if bv is None:
                bv, bi = v, jnp.zeros_like(bg)
            else:
                better = v > bv
                bv = jnp.where(better, v, bv)
                bi = jnp.where(better, r, bi)
        return bi

    i1 = argmax_first(in_sel, None)
    i2 = argmax_first(in_sel, i1)

    def pick(vals, idx):
        out = vals[0]
        for r in range(1, per):
            out = jnp.where(idx == r, vals[r], out)
        return out

    p1 = pick(in_prob, i1)
    p2 = pick(in_prob, i2)
    tot = p1 + p2
    ids_ref[...] = jnp.concatenate([bg * per + i1, bg * per + i2], axis=0)
    gts_ref[...] = jnp.concatenate([p1 / tot, p2 / tot], axis=0)


def router(x, g, mods, layer, router_w_t, router_b, dims, tm_pref=256):
    t, d = x.shape
    n_ctx, seq = dims
    n_e = router_w_t.shape[0]
    tm = _tile(math.gcd(n_ctx, seq), tm_pref)
    return pl.pallas_call(
        functools.partial(_router_kernel, tm=tm, n_ctx=n_ctx, seq=seq),
        grid=(t // tm,),
        in_specs=[pl.BlockSpec((tm, d), lambda i: (i, 0)),
                  pl.BlockSpec((1, d), lambda i: (0, 0)),
                  pl.BlockSpec((None, None, MOD_ROWS, d), lambda i: (layer, 3, 0, 0)),
                  pl.BlockSpec((None, None, MOD_ROWS, d), lambda i: (layer, 4, 0, 0)),
                  pl.BlockSpec((n_e, d), lambda i: (0, 0)),
                  pl.BlockSpec((n_e, 1), lambda i: (0, 0))],
        out_specs=[pl.BlockSpec((tm, d), lambda i: (i, 0)),
                   pl.BlockSpec((2, tm), lambda i: (0, i)),
                   pl.BlockSpec((2, tm), lambda i: (0, i))],
        out_shape=[jax.ShapeDtypeStruct((t, d), F32),
                   jax.ShapeDtypeStruct((2, t), jnp.int32),
                   jax.ShapeDtypeStruct((2, t), F32)],
        compiler_params=_cparams(1), name="router",
    )(x, g.reshape(1, d), mods, mods, router_w_t, router_b.reshape(n_e, 1))


def _gather_rows(src_hbm, idx_ref, base, n_rows, dst, sem):
    def copy(r):
        return pltpu.make_async_copy(src_hbm.at[pl.ds(idx_ref[base + r], 1), :], dst.at[pl.ds(r, 1), :], sem)

    def start(r, c):
        copy(r).start()
        return c

    def wait(r, c):
        copy(r).wait()
        return c

    lax.fori_loop(0, n_rows, start, 0)
    lax.fori_loop(0, n_rows, wait, 0)


def _expert_kernel(te_ref, act_ref, tok_ref, h_hbm, gate_ref, wg_ref, wu_ref, wd_ref, y_ref, xbuf, sem, *, tm):
    i = pl.program_id(0)

    @pl.when(act_ref[i] == 1)
    def _():
        _gather_rows(h_hbm, tok_ref, i * tm, tm, xbuf, sem)
        x = xbuf[...].astype(BF16)
        a = _dot(x, wg_ref[...])
        hid = (a * _sigmoid(a)) * _dot(x, wu_ref[...])
        y_ref[...] = _dot(hid.astype(BF16), wd_ref[...]) * gate_ref[...]

    @pl.when(act_ref[i] == 0)
    def _():
        y_ref[...] = jnp.zeros_like(y_ref)


def expert_ffn(h, tile_expert, tile_active, slot_token, slot_gate, w_gate, w_up, w_down, tm):
    t, d = h.shape
    n_e, _, f = w_gate.shape
    n_tiles = tile_expert.shape[0]
    grid_spec = pltpu.PrefetchScalarGridSpec(
        num_scalar_prefetch=3,
        grid=(n_tiles,),
        in_specs=[pl.BlockSpec(memory_space=pl.ANY),
                  pl.BlockSpec((tm, 1), lambda i, te, act, tok: (i, 0)),
                  pl.BlockSpec((None, d, f), lambda i, te, act, tok: (te[i], 0, 0)),
                  pl.BlockSpec((None, d, f), lambda i, te, act, tok: (te[i], 0, 0)),
                  pl.BlockSpec((None, f, d), lambda i, te, act, tok: (te[i], 0, 0))],
        out_specs=pl.BlockSpec((tm, d), lambda i, te, act, tok: (i, 0)),
        scratch_shapes=[pltpu.VMEM((tm, d), F32), pltpu.SemaphoreType.DMA(())],
    )
    return pl.pallas_call(
        functools.partial(_expert_kernel, tm=tm),
        grid_spec=grid_spec,
        out_shape=jax.ShapeDtypeStruct((n_tiles * tm, d), F32),
        compiler_params=_cparams(1), name="expert_ffn",
    )(tile_expert, tile_active, slot_token, h, slot_gate, w_gate, w_up, w_down)


def _combine_kernel(slots_ref, x_ref, gate_ref, y_hbm, o_ref, ybuf, sem, *, tm, n_tok, n_ctx, seq):
    i = pl.program_id(0)
    _gather_rows(y_hbm, slots_ref, i * tm, tm, ybuf.at[0], sem)
    _gather_rows(y_hbm, slots_ref, n_tok + i * tm, tm, ybuf.at[1], sem)
    rid = _row_id(i * tm, n_ctx, seq)
    o_ref[...] = x_ref[...] + gate_ref[pl.ds(rid, 1), :] * (ybuf[0] + ybuf[1])


def moe_combine(x, y_sorted, tok_slots, mods, layer, dims, tm_pref=256):
    t, d = x.shape
    n_ctx, seq = dims
    tm = _tile(math.gcd(n_ctx, seq), tm_pref)
    grid_spec = pltpu.PrefetchScalarGridSpec(
        num_scalar_prefetch=1,
        grid=(t // tm,),
        in_specs=[pl.BlockSpec((tm, d), lambda i, s: (i, 0)),
                  pl.BlockSpec((None, None, MOD_ROWS, d), lambda i, s: (layer, 5, 0, 0)),
                  pl.BlockSpec(memory_space=pl.ANY)],
        out_specs=pl.BlockSpec((tm, d), lambda i, s: (i, 0)),
        scratch_shapes=[pltpu.VMEM((2, tm, d), F32), pltpu.SemaphoreType.DMA(())],
    )
    return pl.pallas_call(
        functools.partial(_combine_kernel, tm=tm, n_tok=t, n_ctx=n_ctx, seq=seq),
        grid_spec=grid_spec,
        out_shape=jax.ShapeDtypeStruct((t, d), F32),
        input_output_aliases={1: 0},
        compiler_params=_cparams(1), name="moe_combine",
    )(tok_slots, x, mods, y_sorted)


def _moe_plan(ids, gates, n_e, tm):
    n_tok = ids.shape[1]
    e_flat = ids.reshape(-1)
    n_pairs = e_flat.shape[0]
    onehot = (e_flat[:, None] == jnp.arange(n_e, dtype=jnp.int32)[None, :]).astype(jnp.int32)
    rank = jnp.take_along_axis(jnp.cumsum(onehot, axis=0), e_flat[:, None], axis=1)[:, 0] - 1
    counts = jnp.sum(onehot, axis=0)
    padded = (counts + tm - 1) // tm * tm
    ends = jnp.cumsum(padded)
    starts = ends - padded
    dest = starts[e_flat] + rank
    n_tiles = n_pairs // tm + n_e
    n_slots = n_tiles * tm
    pair_tok = jnp.arange(n_pairs, dtype=jnp.int32) % n_tok
    slot_token = jnp.zeros((n_slots,), jnp.int32).at[dest].set(pair_tok)
    slot_gate = jnp.zeros((n_slots,), F32).at[dest].set(gates.reshape(-1))
    tile_start = jnp.arange(n_tiles, dtype=jnp.int32) * tm
    tile_expert = jnp.minimum(jnp.searchsorted(ends, tile_start, side="right"), n_e - 1).astype(jnp.int32)
    tile_active = (tile_start < ends[-1]).astype(jnp.int32)
    last_expert = tile_expert[jnp.maximum(ends[-1] // tm - 1, 0)]
    tile_expert = jnp.where(tile_active == 1, tile_expert, last_expert)
    return tile_expert, tile_active, slot_token, slot_gate.reshape(n_slots, 1), dest.astype(jnp.int32)


def moe_layer(x, norm_g, mods, layer, router_w_t, router_b, w_gate, w_up, w_down, dims, tm_pref=256):
    n_e = w_gate.shape[0]
    h, ids, gates = router(x, norm_g, mods, layer, router_w_t, router_b, dims)
    tm = _tile(2 * x.shape[0], tm_pref)
    tile_expert, tile_active, slot_token, slot_gate, dest = _moe_plan(ids, gates, n_e, tm)
    y_sorted = expert_ffn(h, tile_expert, tile_active, slot_token, slot_gate, w_gate, w_up, w_down, tm)
    return moe_combine(x, y_sorted, dest, mods, layer, dims)


def fourier_rwkv_layer(x, mods, layer, norm_g, w_in, w_out, shift_mu, w0, w_up, a0, a_up, g_up,
                       k_k, k_a, r_k, lnx_g, lnx_b, batch, dims, fnet_tabs):
    n_ctx, ctx_len, seq = dims
    rw = w0.shape[1]
    fw = w_in.shape[1] - 3 * rw - w_up.shape[1] - a_up.shape[1] - g_up.shape[0]
    dl, il, gl = w_up.shape[1], a_up.shape[1], g_up.shape[0]
    lora_pad = 128
    rdims = (n_ctx, seq)

    def pad_cols(a, n):
        return jnp.pad(a, ((0, 0), (0, n - a.shape[1])))

    o_lo = fw + 3 * rw
    w_lo = jnp.concatenate([pad_cols(w_in[:, o_lo:o_lo + dl], lora_pad),
                            pad_cols(w_in[:, o_lo + dl:o_lo + dl + il], lora_pad),
                            w_in[:, o_lo + dl + il:]], axis=1).astype(BF16)
    mu = shift_mu.reshape(1, -1)
    mu_lo = jnp.concatenate([pad_cols(mu[:, 3 * rw:3 * rw + dl], lora_pad),
                             pad_cols(mu[:, 3 * rw + dl:3 * rw + dl + il], lora_pad),
                             mu[:, 3 * rw + dl + il:]], axis=1)
    w_up_p = jnp.pad(w_up, ((0, 0), (0, lora_pad - dl), (0, 0))).astype(BF16)
    a_up_p = jnp.pad(a_up, ((0, 0), (0, lora_pad - il), (0, 0))).astype(BF16)

    f = norm_mod_matmul(x, norm_g, mods, layer, 0, 1, w_in[:, :fw].astype(BF16), BF16, rdims)
    prkv = norm_mod_matmul(x, norm_g, mods, layer, 0, 1, w_in[:, fw:o_lo].astype(BF16), F32, rdims)
    plo = norm_mod_matmul(x, norm_g, mods, layer, 0, 1, w_lo, F32, rdims)

    r, v, kk, lw, kd, bd, g, bv = rwkv_terms(
        prkv, plo, mu[:, :3 * rw], mu_lo, w0, w_up_p, a0, a_up_p, g_up.astype(BF16),
        k_k.reshape(1, rw), k_a.reshape(1, rw), r_k.reshape(1, rw), dims, lora_pad)
    o2 = rwkv_scan(r, v, kk, lw, kd, bd, batch, dims)
    y_rw = rwkv_readout(o2, bv, g, lnx_g, lnx_b)

    cos_c, sin_c, cos_x, sin_x, cos_l, sin_l = fnet_tabs
    gc = fw // FNET_GROUPS
    y1, y2 = fnet_channel_dft(f, cos_c, sin_c)
    f_ctx = fnet_sequence_dft(y1, y2, 0, batch, ctx_len, cos_x, sin_x, gc)
    f_lat = fnet_sequence_dft(y1, y2, n_ctx, batch, seq, cos_l, sin_l, gc)
    f_mix = jnp.concatenate([f_ctx, f_lat], axis=0)
    w_out_b = w_out.astype(BF16)
    return matmul_gated_residual([f_mix, y_rw], [w_out_b[:fw], w_out_b[fw:]], x, mods, layer, 2, rdims)


def attention_layer(x, mods, layer, norm_g, w_qkv, w_o, sink, batch, dims, rope_tabs):
    n_ctx, ctx_len, seq = dims
    q_cols = w_o.shape[0]
    kv_cols = (w_qkv.shape[1] - q_cols) // 2
    cos_t, sin_t = rope_tabs
    qkv = norm_mod_matmul(x, norm_g, mods, layer, 0, 1, w_qkv.astype(BF16), BF16, (n_ctx, seq),
                          rope=(q_cols + kv_cols, cos_t, sin_t), tn_pref=cos_t.shape[1])
    att = attention(qkv, sink, batch, dims, q_cols, kv_cols)
    return matmul_gated_residual([att], [w_o.astype(BF16)], x, mods, layer, 2, (n_ctx, seq))


def kernel(x, c, ctx, c_ctx, ada_down, ada_up, ada_bias, norm1_g, norm2_g, final_g, mix_w_in, mix_w_out, shift_mu, decay_w0, decay_up, iclr_a0, iclr_up, gate_up, k_k, k_a, r_k, lnx_g, lnx_b, att_w_qkv, att_w_o, att_sink, router_w, router_b, exp_w_gate, exp_w_up, exp_w_down):
    batch, seq, d = x.shape
    ctx_len = ctx.shape[1]
    depth = ada_down.shape[0]
    n_ctx = batch * ctx_len
    assert batch + 1 <= MOD_ROWS
    dims = (n_ctx, ctx_len, seq)
    xs = jnp.concatenate([ctx.reshape(n_ctx, d), x.reshape(batch * seq, d)], axis=0)

    cvec = jnp.concatenate([c_ctx[None, :], c, jnp.zeros((MOD_ROWS - 1 - batch, d), F32)], axis=0)
    mods = adaln_all(cvec, ada_down, ada_up, ada_bias)
    mods = mods.reshape(depth, MOD_ROWS, N_MOD, d).transpose(0, 2, 1, 3)

    fw = mix_w_in.shape[2] - 3 * decay_w0.shape[2] - decay_up.shape[2] - iclr_up.shape[2] - gate_up.shape[1]
    gc = fw // FNET_GROUPS
    fnet_tabs = _dft_tables(gc) + _dft_tables(ctx_len) + _dft_tables(seq)
    kv_cols = (att_w_qkv.shape[2] - att_w_o.shape[1]) // 2
    rope_tabs = _rope_tables(seq, min(512, kv_cols))
    router_w_t = router_w.T

    for layer in range(depth):
        j = layer // 2
        if layer % 2 == 0:
            xs = fourier_rwkv_layer(xs, mods, layer, norm1_g[layer], mix_w_in[j], mix_w_out[j], shift_mu[j],
                                    decay_w0[j], decay_up[j], iclr_a0[j], iclr_up[j], gate_up[j],
                                    k_k[j], k_a[j], r_k[j], lnx_g[j], lnx_b[j], batch, dims, fnet_tabs)
        else:
            xs = attention_layer(xs, mods, layer, norm1_g[layer], att_w_qkv[j], att_w_o[j], att_sink[j],
                                 batch, dims, rope_tabs)
        xs = moe_layer(xs, norm2_g[layer], mods, layer, router_w_t, router_b,
                       exp_w_gate[layer].astype(BF16), exp_w_up[layer].astype(BF16),
                       exp_w_down[layer].astype(BF16), (n_ctx, seq))
    return final_norm(xs, final_g, n_ctx).reshape(batch, seq, d)
```

```python
import functools
import math

import jax
import jax.numpy as jnp
from jax import lax
from jax.experimental import pallas as pl
from jax.experimental.pallas import tpu as pltpu

F32 = jnp.float32
BF16 = jnp.bfloat16
HIGHEST = lax.Precision.HIGHEST

GRID_W = 64
NORM_EPS = 1e-6
GN_EPS = 64e-5
L2_EPS = 1e-12
N_MOD = 6
FNET_GROUPS = 4
HEAD = 64
ATT_GROUP = 8
ATT_SCALE = 0.125
WINDOW = 128
QBLOCK = 128
ROPE_THETA = 10000.0
N_EXPERT_GROUPS = 4
EXPERTS_PER_GROUP = 4
CHUNK = 64
MOD_ROWS = 8

VMEM_LIMIT = 56 * 1024 * 1024


def _cparams(n_axes):
    return pltpu.CompilerParams(dimension_semantics=("arbitrary",) * n_axes,
                                vmem_limit_bytes=VMEM_LIMIT)


def _tile(n, pref):
    t = min(n, pref)
    while n % t:
        t //= 2
    return t


def _row_id(r0, n_ctx, seq):
    return jnp.where(r0 < n_ctx, 0, 1 + (r0 - n_ctx) // seq)


def _dot(a, b):
    return jnp.dot(a, b, preferred_element_type=F32)


def _dot_hi(a, b):
    return jnp.dot(a, b, preferred_element_type=F32, precision=HIGHEST)


def _dot_t0(a, b):
    return lax.dot_general(a, b, (((0,), (0,)), ((), ())), preferred_element_type=F32)


def _dot_t1(a, b):
    return lax.dot_general(a, b, (((1,), (1,)), ((), ())), preferred_element_type=F32)


def _dot_hi_t1(a, b):
    return lax.dot_general(a, b, (((1,), (1,)), ((), ())), preferred_element_type=F32, precision=HIGHEST)


def _sigmoid(x):
    return 1.0 / (1.0 + jnp.exp(-x))


def _norm_mod(x, g, sh, sc):
    ms = jnp.mean(x * x, axis=-1, keepdims=True)
    y = x * lax.rsqrt(ms + NORM_EPS) * g
    return y * (1.0 + sc) + sh


def _adaln_kernel(cv_ref, down_ref, up_ref, bias_ref, o_ref, t_scr):
    @pl.when(pl.program_id(1) == 0)
    def _():
        cv = cv_ref[...]
        s = cv * _sigmoid(cv)
        t_scr[...] = _dot(s.astype(BF16), down_ref[...].astype(BF16)).astype(BF16)
    o_ref[...] = _dot(t_scr[...], up_ref[...].astype(BF16)) + bias_ref[...]


def adaln_all(cvec, ada_down, ada_up, ada_bias):
    depth, d, rank = ada_down.shape
    n = ada_up.shape[2]
    tn = _tile(n, 2048)
    bias = ada_bias.reshape(depth, 1, n)
    return pl.pallas_call(
        _adaln_kernel,
        grid=(depth, n // tn),
        in_specs=[pl.BlockSpec((MOD_ROWS, d), lambda l, j: (0, 0)),
                  pl.BlockSpec((None, d, rank), lambda l, j: (l, 0, 0)),
                  pl.BlockSpec((None, rank, tn), lambda l, j: (l, 0, j)),
                  pl.BlockSpec((None, 1, tn), lambda l, j: (l, 0, j))],
        out_specs=pl.BlockSpec((None, MOD_ROWS, tn), lambda l, j: (l, 0, j)),
        out_shape=jax.ShapeDtypeStruct((depth, MOD_ROWS, n), F32),
        scratch_shapes=[pltpu.VMEM((MOD_ROWS, rank), BF16)],
        compiler_params=_cparams(2), name="adaln",
    )(cvec, ada_down, ada_up, bias)


def _norm_mod_kernel(x_ref, g_ref, sh_ref, sc_ref, h_ref, *, tm, n_ctx, seq):
    rid = _row_id(pl.program_id(0) * tm, n_ctx, seq)
    h = _norm_mod(x_ref[...], g_ref[...], sh_ref[pl.ds(rid, 1), :], sc_ref[pl.ds(rid, 1), :])
    h_ref[...] = h.astype(h_ref.dtype)


def norm_mod(x, g, mods, layer, k_shift, k_scale, dims, tm_pref=512):
    t, d = x.shape
    n_ctx, seq = dims
    tm = _tile(math.gcd(n_ctx, seq), tm_pref)
    return pl.pallas_call(
        functools.partial(_norm_mod_kernel, tm=tm, n_ctx=n_ctx, seq=seq),
        grid=(t // tm,),
        in_specs=[pl.BlockSpec((tm, d), lambda i: (i, 0)),
                  pl.BlockSpec((1, d), lambda i: (0, 0)),
                  pl.BlockSpec((None, None, MOD_ROWS, d), lambda i: (layer, k_shift, 0, 0)),
                  pl.BlockSpec((None, None, MOD_ROWS, d), lambda i: (layer, k_scale, 0, 0))],
        out_specs=pl.BlockSpec((tm, d), lambda i: (i, 0)),
        out_shape=jax.ShapeDtypeStruct((t, d), BF16),
        compiler_params=_cparams(1), name="norm_mod",
    )(x, g.reshape(1, d), mods, mods)


def _matmul_kernel(*refs, tm, n_ctx, rope_cols):
    if rope_cols:
        a_ref, w_ref, cos_ref, sin_ref, o_ref = refs
    else:
        a_ref, w_ref, o_ref = refs
    r0 = pl.program_id(0) * tm
    j = pl.program_id(1)
    acc = _dot(a_ref[...], w_ref[...])
    if rope_cols:
        tn = acc.shape[1]
        do_rope = jnp.logical_and(r0 >= n_ctx, j * tn < rope_cols)

        @pl.when(do_rope)
        def _():
            lane = lax.broadcasted_iota(jnp.int32, acc.shape, 1)
            swapped = jnp.where(lane % 32 < 16, pltpu.roll(acc, tn - 16, 1), pltpu.roll(acc, 16, 1))
            o_ref[...] = (acc * cos_ref[...] + swapped * sin_ref[...]).astype(o_ref.dtype)

        @pl.when(jnp.logical_not(do_rope))
        def _():
            o_ref[...] = acc.astype(o_ref.dtype)
    else:
        o_ref[...] = acc.astype(o_ref.dtype)


def matmul(a, w, out_dtype, dims, rope=None, tm_pref=1024, tn_pref=512):
    t, d = a.shape
    n = w.shape[1]
    n_ctx, seq = dims
    tm = _tile(math.gcd(n_ctx, seq), tm_pref)
    tn = _tile(n, tn_pref)
    in_specs = [pl.BlockSpec((tm, d), lambda i, j: (i, 0)),
                pl.BlockSpec((d, tn), lambda i, j: (0, j))]
    args = [a, w]
    rope_cols = 0
    if rope is not None:
        rope_cols, cos_t, sin_t = rope
        assert rope_cols % tn == 0 and cos_t.shape == (seq, tn)
        pos_map = lambda i, j: (jnp.maximum(i * tm - n_ctx, 0) % seq // tm, 0)
        in_specs += [pl.BlockSpec((tm, tn), pos_map), pl.BlockSpec((tm, tn), pos_map)]
        args += [cos_t, sin_t]
    return pl.pallas_call(
        functools.partial(_matmul_kernel, tm=tm, n_ctx=n_ctx, rope_cols=rope_cols),
        grid=(t // tm, n // tn),
        in_specs=in_specs,
        out_specs=pl.BlockSpec((tm, tn), lambda i, j: (i, j)),
        out_shape=jax.ShapeDtypeStruct((t, n), out_dtype),
        compiler_params=_cparams(2), name="matmul",
    )(*args)


def _mmres_kernel(*refs, n_a, tm, n_ctx, seq):
    a_refs = refs[:n_a]
    w_refs = refs[n_a:2 * n_a]
    x_ref, gate_ref, o_ref = refs[2 * n_a:]
    acc = _dot(a_refs[0][...], w_refs[0][...])
    for a_ref, w_ref in zip(a_refs[1:], w_refs[1:]):
        acc += _dot(a_ref[...], w_ref[...])
    rid = _row_id(pl.program_id(0) * tm, n_ctx, seq)
    o_ref[...] = x_ref[...] + gate_ref[pl.ds(rid, 1), :] * acc


def matmul_gated_residual(a_list, w_list, x, mods, layer, k_gate, dims, tm_pref=1024, tn_pref=512):
    t, d = x.shape
    n_ctx, seq = dims
    tm = _tile(math.gcd(n_ctx, seq), tm_pref)
    tn = _tile(d, tn_pref)
    n_a = len(a_list)
    in_specs = [pl.BlockSpec((tm, a.shape[1]), lambda i, j: (i, 0)) for a in a_list]
    in_specs += [pl.BlockSpec((w.shape[0], tn), lambda i, j: (0, j)) for w in w_list]
    in_specs += [pl.BlockSpec((tm, tn), lambda i, j: (i, j)),
                 pl.BlockSpec((None, None, MOD_ROWS, tn), lambda i, j: (layer, k_gate, 0, j))]
    return pl.pallas_call(
        functools.partial(_mmres_kernel, n_a=n_a, tm=tm, n_ctx=n_ctx, seq=seq),
        grid=(t // tm, d // tn),
        in_specs=in_specs,
        out_specs=pl.BlockSpec((tm, tn), lambda i, j: (i, j)),
        out_shape=jax.ShapeDtypeStruct((t, d), F32),
        input_output_aliases={2 * n_a: 0},
        compiler_params=_cparams(2), name="matmul_gated_residual",
    )(*a_list, *w_list, x, mods)


def _final_norm_kernel(x_ref, g_ref, o_ref):
    x = x_ref[...]
    ms = jnp.mean(x * x, axis=-1, keepdims=True)
    o_ref[...] = x * lax.rsqrt(ms + NORM_EPS) * g_ref[...]


def final_norm(x, g, n_ctx, tm_pref=512):
    t, d = x.shape
    tm = _tile(math.gcd(n_ctx, t - n_ctx), tm_pref)
    off = n_ctx // tm
    return pl.pallas_call(
        _final_norm_kernel,
        grid=((t - n_ctx) // tm,),
        in_specs=[pl.BlockSpec((tm, d), lambda i: (i + off, 0)),
                  pl.BlockSpec((1, d), lambda i: (0, 0))],
        out_specs=pl.BlockSpec((tm, d), lambda i: (i, 0)),
        out_shape=jax.ShapeDtypeStruct((t - n_ctx, d), F32),
        compiler_params=_cparams(1), name="final_norm",
    )(x, g.reshape(1, d))


def _dft_tables(n, scale=1.0):
    j = jnp.arange(n, dtype=jnp.int32)
    m = (j[:, None] * j[None, :]) % n
    ang = m.astype(F32) * (2.0 * math.pi / n)
    return (jnp.cos(ang) * scale).astype(BF16), (jnp.sin(ang) * scale).astype(BF16)


def _fnet_ch_kernel(x_ref, c_ref, s_ref, y1_ref, y2_ref):
    x = x_ref[...]
    y1_ref[...] = _dot(x, c_ref[...]).astype(y1_ref.dtype)
    y2_ref[...] = _dot(x, s_ref[...]).astype(y2_ref.dtype)


def fnet_channel_dft(f, cos_c, sin_c, tm_pref=1024):
    t, width = f.shape
    gc = width // FNET_GROUPS
    tm = _tile(t, tm_pref)
    spec = pl.BlockSpec((tm, gc), lambda i, g: (i, g))
    tab = pl.BlockSpec((gc, gc), lambda i, g: (0, 0))
    return pl.pallas_call(
        _fnet_ch_kernel,
        grid=(t // tm, FNET_GROUPS),
        in_specs=[spec, tab, tab],
        out_specs=[spec, spec],
        out_shape=[jax.ShapeDtypeStruct((t, width), BF16)] * 2,
        compiler_params=_cparams(2), name="fnet_channel_dft",
    )(f, cos_c, sin_c)


def _fnet_seq_kernel(c_ref, s_ref, y1_ref, y2_ref, o_ref, acc_ref, *, scale):
    k = pl.program_id(2)

    @pl.when(k == 0)
    def _():
        acc_ref[...] = jnp.zeros_like(acc_ref)

    acc_ref[...] += _dot(c_ref[...], y1_ref[...]) - _dot(s_ref[...], y2_ref[...])

    @pl.when(k == pl.num_programs(2) - 1)
    def _():
        o_ref[...] = (acc_ref[...] * scale).astype(o_ref.dtype)


def fnet_sequence_dft(y1, y2, row0, n_seq, length, cos_l, sin_l, gc, t_pref=512):
    width = y1.shape[1]
    tm = _tile(length, t_pref)
    tk = _tile(math.gcd(length, row0) if row0 else length, t_pref)
    nb = length // tm
    nk = length // tk
    scale = 1.0 / math.sqrt(length * gc)
    rb0 = row0 // tk
    y_spec = pl.BlockSpec((tk, width), lambda b, i, k: (rb0 + b * nk + k, 0))
    return pl.pallas_call(
        functools.partial(_fnet_seq_kernel, scale=scale),
        grid=(n_seq, nb, nk),
        in_specs=[pl.BlockSpec((tm, tk), lambda b, i, k: (i, k)),
                  pl.BlockSpec((tm, tk), lambda b, i, k: (i, k)),
                  y_spec, y_spec],
        out_specs=pl.BlockSpec((tm, width), lambda b, i, k: (b * nb + i, 0)),
        out_shape=jax.ShapeDtypeStruct((n_seq * length, width), BF16),
        scratch_shapes=[pltpu.VMEM((tm, width), F32)],
        compiler_params=_cparams(3), name="fnet_sequence_dft",
    )(cos_l, sin_l, y1, y2)


def _head_sum(x, ones_bd):
    hi = x.astype(BF16)
    lo = (x - hi.astype(F32)).astype(BF16)
    return _dot(hi, ones_bd) + _dot(lo, ones_bd)


def _shift(x, prev_row, next_row, mu, pos, length):
    rows = x.shape[0]
    ridx = lax.broadcasted_iota(jnp.int32, (rows, 1), 0)
    prev = jnp.where(ridx == 0, prev_row, pltpu.roll(x, 1, 0))
    nxt = jnp.where(ridx == rows - 1, next_row, pltpu.roll(x, rows - 1, 0))
    prev = jnp.where(pos == 0, 0.0, prev)
    nxt = jnp.where(pos == length - 1, 0.0, nxt)
    return x + mu * (0.5 * (prev + nxt) - x)


def _rwkv_terms_kernel(r_ref, k_ref, v_ref, rp_ref, kp_ref, vp_ref, rn_ref, kn_ref, vn_ref,
                       lo_ref, lop_ref, lon_ref, mur_ref, muk_ref, muv_ref, mulo_ref,
                       w0_ref, wup_ref, a0_ref, aup_ref, gup_ref, kk_ref, ka_ref, rk_ref, ones_ref,
                       r_o, v_o, kk_o, lw_o, kd_o, bd_o, g_o, bv_o, *, tm, n_ctx, ctx_len, seq, lora_pad):
    i = pl.program_id(0)
    r0 = i * tm
    ridx = lax.broadcasted_iota(jnp.int32, (tm, 1), 0) + r0
    in_ctx = r0 < n_ctx
    length = jnp.where(in_ctx, ctx_len, seq)
    pos = jnp.where(in_ctx, ridx % ctx_len, (ridx - n_ctx) % seq)

    def sh(ref, pref, nref, mu_ref):
        return _shift(ref[...], pref[7:8, :], nref[0:1, :], mu_ref[...], pos, length)

    r = sh(r_ref, rp_ref, rn_ref, mur_ref)
    k = sh(k_ref, kp_ref, kn_ref, muk_ref)
    v = sh(v_ref, vp_ref, vn_ref, muv_ref)
    lo = sh(lo_ref, lop_ref, lon_ref, mulo_ref)
    w_in = jnp.tanh(lo[:, :lora_pad]).astype(BF16)
    a_in = lo[:, lora_pad:2 * lora_pad].astype(BF16)
    g_in = _sigmoid(lo[:, 2 * lora_pad:]).astype(BF16)
    ones_bd = ones_ref[...]

    kk = k * kk_ref[...]
    kk = kk * lax.rsqrt(_head_sum(kk * kk, ones_bd) + L2_EPS)
    ksum = jnp.zeros_like(k)
    for d in range(2):
        w_logit = w0_ref[d:d + 1, :] + _dot(w_in, wup_ref[d])
        lw_o[d] = -math.exp(-0.5) * _sigmoid(w_logit)
        a = _sigmoid(a0_ref[d:d + 1, :] + _dot(a_in, aup_ref[d]))
        k_d = k * (1.0 + (a - 1.0) * ka_ref[...])
        kd_o[d] = k_d
        bd_o[d] = kk * a
        ksum = ksum + k_d
    g_o[...] = _dot(g_in, gup_ref[...])
    bv_o[...] = _head_sum(r * ksum * rk_ref[...], ones_bd) * v
    r_o[...] = r
    v_o[...] = v
    kk_o[...] = kk


def rwkv_terms(prkv, plo, mu_rkv, mu_lo, w0, w_up, a0, a_up, g_up, k_k, k_a, r_k, dims, lora_pad, tm_pref=256):
    t = prkv.shape[0]
    rw = prkv.shape[1] // 3
    n_ctx, ctx_len, seq = dims
    tm = _tile(math.gcd(n_ctx, seq), tm_pref)
    tn = _tile(rw, 512)
    nj = rw // tn
    lo_w = plo.shape[1]
    hb = tm // 8
    last8 = t // 8 - 1

    def main(c):
        return pl.BlockSpec((tm, tn), lambda i, j: (i, c * nj + j))

    def prev(c):
        return pl.BlockSpec((8, tn), lambda i, j: (jnp.maximum(i * hb - 1, 0), c * nj + j))

    def nxt(c):
        return pl.BlockSpec((8, tn), lambda i, j: (jnp.minimum((i + 1) * hb, last8), c * nj + j))

    def vec(c=0):
        return pl.BlockSpec((1, tn), lambda i, j: (0, c * nj + j))

    ones_bd = (jnp.arange(tn)[:, None] // HEAD == jnp.arange(tn)[None, :] // HEAD).astype(BF16)
    out_tok = pl.BlockSpec((tm, tn), lambda i, j: (i, j))
    out_dir = pl.BlockSpec((2, tm, tn), lambda i, j: (0, i, j))
    tok = jax.ShapeDtypeStruct((t, rw), F32)
    tok2 = jax.ShapeDtypeStruct((2, t, rw), F32)
    return pl.pallas_call(
        functools.partial(_rwkv_terms_kernel, tm=tm, n_ctx=n_ctx, ctx_len=ctx_len, seq=seq, lora_pad=lora_pad),
        grid=(t // tm, nj),
        in_specs=[main(0), main(1), main(2), prev(0), prev(1), prev(2), nxt(0), nxt(1), nxt(2),
                  pl.BlockSpec((tm, lo_w), lambda i, j: (i, 0)),
                  pl.BlockSpec((8, lo_w), lambda i, j: (jnp.maximum(i * hb - 1, 0), 0)),
                  pl.BlockSpec((8, lo_w), lambda i, j: (jnp.minimum((i + 1) * hb, last8), 0)),
                  vec(0), vec(1), vec(2),
                  pl.BlockSpec((1, lo_w), lambda i, j: (0, 0)),
                  pl.BlockSpec((2, tn), lambda i, j: (0, j)),
                  pl.BlockSpec((2, lora_pad, tn), lambda i, j: (0, 0, j)),
                  pl.BlockSpec((2, tn), lambda i, j: (0, j)),
                  pl.BlockSpec((2, lora_pad, tn), lambda i, j: (0, 0, j)),
                  pl.BlockSpec((g_up.shape[0], tn), lambda i, j: (0, j)),
                  vec(), vec(), vec(),
                  pl.BlockSpec((tn, tn), lambda i, j: (0, 0))],
        out_specs=[out_tok, out_tok, out_tok, out_dir, out_dir, out_dir, out_tok, out_tok],
        out_shape=[tok, tok, tok, tok2, tok2, tok2, tok, tok],
        compiler_params=_cparams(2), name="rwkv_terms",
    )(prkv, prkv, prkv, prkv, prkv, prkv, prkv, prkv, prkv, plo, plo, plo,
      mu_rkv, mu_rkv, mu_rkv, mu_lo, w0, w_up, a0, a_up, g_up, k_k, k_a, r_k, ones_bd)


def _rwkv_scan_kernel(r_ref, v_ref, kk_ref, lw_ref, kd_ref, bd_ref, o_ref,
                      s_scr, rt_s, at_s, kt_s, bt_s, bh_s, kh_s, vb_s, pd_s, *, n_heads, group):
    d = pl.program_id(1)
    step = pl.program_id(2)
    c = CHUNK

    @pl.when(step == 0)
    def _():
        s_scr[...] = jnp.zeros_like(s_scr)

    row = lax.broadcasted_iota(jnp.int32, (c, c), 0)
    col = lax.broadcasted_iota(jnp.int32, (c, c), 1)
    fwd = d == 0
    ahead = (row - col) * jnp.where(fwd, 1, -1)
    incl = ahead >= 0
    strict = ahead > 0
    eye = row == col
    eye_f = jnp.where(eye, 1.0, 0.0)

    tri = jnp.where(incl, 1.0, 0.0).astype(BF16)
    lw = lw_ref[...]
    lw_hi = lw.astype(BF16)
    lw_lo = (lw - lw_hi.astype(F32)).astype(BF16)
    l_inc = _dot(tri, lw_hi) + _dot(tri, lw_lo)
    l_tot = jnp.where(fwd, l_inc[c - 1:c, :], l_inc[0:1, :])
    e_neg = jnp.exp(-l_inc)
    e_rem = jnp.exp(l_tot - l_inc)
    kd = kd_ref[...]
    bd = bd_ref[...]
    rt_s[...] = (r_ref[...] * jnp.exp(l_inc)).astype(BF16)
    at_s[...] = (-kk_ref[...] * jnp.exp(l_inc - lw)).astype(BF16)
    kt_s[...] = (kd * e_neg).astype(BF16)
    bt_s[...] = (bd * e_neg).astype(BF16)
    bh_s[...] = (bd * e_rem).astype(BF16)
    kh_s[...] = (kd * e_rem).astype(BF16)
    vb_s[...] = v_ref[...].astype(BF16)
    pd_s[...] = jnp.exp(l_tot)

    gw = group * HEAD

    def group_body(gi, carry):
        sl = pl.ds(pl.multiple_of(gi * gw, gw), gw)
        rt_g, at_g, kt_g, bt_g, bh_g, kh_g, v_g = (s[:, sl] for s in (rt_s, at_s, kt_s, bt_s, bh_s, kh_s, vb_s))
        pd_g = pd_s[:, sl]
        heads = range(group)

        def hd(a, q):
            return a[:, q * HEAD:(q + 1) * HEAD]

        lhs = [jnp.concatenate([hd(at_g, q), hd(rt_g, q)], axis=0) for q in heads]
        gb = [_dot_t1(lhs[q], hd(bt_g, q)) for q in heads]
        gk = [_dot_t1(lhs[q], hd(kt_g, q)) for q in heads]
        a_ab = [jnp.where(strict, gb[q][:c], 0.0) for q in heads]
        a_rb = [jnp.where(incl, gb[q][c:], 0.0).astype(BF16) for q in heads]
        a_k = [jnp.concatenate([jnp.where(strict, gk[q][:c], 0.0), jnp.where(incl, gk[q][c:], 0.0)],
                               axis=0).astype(BF16) for q in heads]
        vv = [_dot(a_k[q], hd(v_g, q)) for q in heads]
        kv = [_dot_t0(hd(kh_g, q), hd(v_g, q)) for q in heads]
        x = [a.astype(BF16) for a in a_ab]
        tinv = [eye_f + a for a in a_ab]
        for _ in range(int(math.log2(c)) - 1):
            x = [_dot(x[q], x[q]).astype(BF16) for q in heads]
            tinv = [tinv[q] + _dot(tinv[q].astype(BF16), x[q]) for q in heads]
        tb = [t.astype(BF16) for t in tinv]
        wt = [_dot(tb[q], hd(at_g, q)).astype(BF16) for q in heads]
        u0 = [_dot(tb[q], vv[q][:c].astype(BF16)).astype(BF16) for q in heads]
        qm = [hd(rt_g, q).astype(F32) + _dot(a_rb[q], wt[q]) for q in heads]
        o0 = [_dot(a_rb[q], u0[q]) + vv[q][c:] for q in heads]
        m = [eye_f * hd(pd_g, q) + _dot_t0(hd(bh_g, q), wt[q]) for q in heads]
        nn = [_dot_t0(hd(bh_g, q), u0[q]) + kv[q] for q in heads]
        h0 = gi * group
        st = s_scr[pl.ds(h0, group)]
        res = [_dot(jnp.concatenate([qm[q], m[q]], axis=0).astype(BF16), st[q].astype(BF16)) for q in heads]
        s_scr[pl.ds(h0, group)] = jnp.stack([res[q][c:] + nn[q] for q in heads])
        o_ref[:, sl] = jnp.concatenate([res[q][:c] + o0[q] for q in heads], axis=1)
        return carry

    lax.fori_loop(0, n_heads // group, group_body, 0)


def rwkv_scan(r, v, kk, lw, kd, bd, batch, dims):
    t, rw = r.shape
    n_ctx, ctx_len, seq = dims
    c = CHUNK
    n_cc = ctx_len // c
    n_lc = seq // c
    steps = n_cc + n_lc

    def blk(b, d, s):
        ctx_i = jnp.where(d == 0, s, n_cc - 1 - s)
        lat_i = jnp.where(d == 0, s - n_cc, n_lc - 1 - (s - n_cc))
        return jnp.where(s < n_cc, b * n_cc + ctx_i, n_ctx // c + b * n_lc + lat_i)

    shared = pl.BlockSpec((c, rw), lambda b, d, s: (blk(b, d, s), 0))
    per_dir = pl.BlockSpec((None, c, rw), lambda b, d, s: (d, blk(b, d, s), 0))
    n_heads = rw // HEAD
    group = 16 if n_heads % 16 == 0 else 2
    prep = pltpu.VMEM((c, rw), BF16)
    return pl.pallas_call(
        functools.partial(_rwkv_scan_kernel, n_heads=n_heads, group=group),
        grid=(batch, 2, steps),
        in_specs=[shared, shared, shared, per_dir, per_dir, per_dir],
        out_specs=per_dir,
        out_shape=jax.ShapeDtypeStruct((2, t, rw), F32),
        scratch_shapes=[pltpu.VMEM((n_heads, HEAD, HEAD), F32)] + [prep] * 7 + [pltpu.VMEM((1, rw), F32)],
        compiler_params=_cparams(3), name="rwkv_scan",
    )(r, v, kk, lw, kd, bd)


def _rwkv_readout_kernel(o_ref, bv_ref, g_ref, lg_ref, lb_ref, ones_ref, y_ref):
    o = o_ref[0] + o_ref[1]
    ones_bd = ones_ref[...]
    mu = _head_sum(o, ones_bd) * (1.0 / HEAD)
    dev = o - mu
    var = _head_sum(dev * dev, ones_bd) * (1.0 / HEAD)
    on = dev * lax.rsqrt(var + GN_EPS) * lg_ref[...] + lb_ref[...]
    y_ref[...] = ((on + bv_ref[...]) * g_ref[...]).astype(y_ref.dtype)


def rwkv_readout(o2, bv, g, lnx_g, lnx_b, tm_pref=512):
    _, t, rw = o2.shape
    tm = _tile(t, tm_pref)
    tn = _tile(rw, 512)
    ones_bd = (jnp.arange(tn)[:, None] // HEAD == jnp.arange(tn)[None, :] // HEAD).astype(BF16)
    tok = pl.BlockSpec((tm, tn), lambda i, j: (i, j))
    vec = pl.BlockSpec((1, tn), lambda i, j: (0, j))
    return pl.pallas_call(
        _rwkv_readout_kernel,
        grid=(t // tm, rw // tn),
        in_specs=[pl.BlockSpec((2, tm, tn), lambda i, j: (0, i, j)), tok, tok, vec, vec,
                  pl.BlockSpec((tn, tn), lambda i, j: (0, 0))],
        out_specs=tok,
        out_shape=jax.ShapeDtypeStruct((t, rw), BF16),
        compiler_params=_cparams(2), name="rwkv_readout",
    )(o2, bv, g, lnx_g.reshape(1, rw), lnx_b.reshape(1, rw), ones_bd)


def _attend_group(q_heads, k_all, v_all, bias, sinks):
    nq = bias.shape[0]
    s = _dot_t1(jnp.concatenate(q_heads, axis=0), k_all)
    probs, denoms = [], []
    for g, sink in enumerate(sinks):
        sg = s[g * nq:(g + 1) * nq] + bias
        m = jnp.maximum(jnp.max(sg, axis=-1, keepdims=True), sink)
        p = jnp.exp(sg - m)
        denoms.append(jnp.sum(p, axis=-1, keepdims=True) + jnp.exp(sink - m))
        probs.append(p.astype(BF16))
    o = _dot(jnp.concatenate(probs, axis=0), v_all)
    return [o[g * nq:(g + 1) * nq] / denoms[g] for g in range(len(sinks))]


def _attn_kernel(sink_ref, q_ref, kp_ref, kc_ref, kn_ref, vp_ref, vc_ref, vn_ref, kx_ref, vx_ref, o_ref,
                 *, seq, n_ctx_keys, n_ctx_blocks):
    pair = pl.program_id(2)
    qi = pl.program_id(1) - n_ctx_blocks
    span = QBLOCK + 2 * WINDOW
    rr = lax.broadcasted_iota(jnp.int32, (QBLOCK, span + n_ctx_keys), 0)
    cc = lax.broadcasted_iota(jnp.int32, (QBLOCK, span + n_ctx_keys), 1)
    key_pos = qi * QBLOCK - WINDOW + cc
    off = cc - WINDOW - rr
    in_band = (jnp.abs(off) <= WINDOW) & (key_pos >= 0) & (key_pos < seq) & (qi >= 0)
    bias = jnp.where(jnp.logical_or(cc >= span, in_band), 0.0, -jnp.inf)
    q_all = q_ref[...] * jnp.asarray(ATT_SCALE, q_ref.dtype)
    outs = []
    for kv in range(2):
        hs = slice(kv * HEAD, (kv + 1) * HEAD)
        k_all = jnp.concatenate([kp_ref[:, hs], kc_ref[:, hs], kn_ref[:, hs], kx_ref[:, hs]], axis=0)
        v_all = jnp.concatenate([vp_ref[:, hs], vc_ref[:, hs], vn_ref[:, hs], vx_ref[:, hs]], axis=0)
        heads = [kv * ATT_GROUP + g for g in range(ATT_GROUP)]
        outs += _attend_group([q_all[:, hq * HEAD:(hq + 1) * HEAD] for hq in heads], k_all, v_all, bias,
                              [sink_ref[pair * 2 * ATT_GROUP + hq] for hq in heads])
    o_ref[...] = jnp.concatenate(outs, axis=1).astype(o_ref.dtype)


def attention(qkv, sink, batch, dims, q_cols, kv_cols):
    t = qkv.shape[0]
    n_ctx, ctx_len, seq = dims
    n_pairs = kv_cols // (2 * HEAD)
    qw = 2 * ATT_GROUP * HEAD
    kw = 2 * HEAD
    nqb = seq // QBLOCK
    ncb = ctx_len // QBLOCK
    lat0 = n_ctx // QBLOCK
    kcol = q_cols // kw
    vcol = (q_cols + kv_cols) // kw
    smem = pl.BlockSpec(memory_space=pltpu.SMEM)

    def q_map(b, i, p):
        return (jnp.where(i < ncb, b * ncb + i, lat0 + b * nqb + i - ncb), p)

    def band(col0, shift):
        def imap(b, i, p):
            return (lat0 + b * nqb + jnp.clip(i - ncb + shift, 0, nqb - 1), col0 + p)
        return pl.BlockSpec((QBLOCK, kw), imap)

    return pl.pallas_call(
        functools.partial(_attn_kernel, seq=seq, n_ctx_keys=ctx_len, n_ctx_blocks=ncb),
        grid=(batch, ncb + nqb, n_pairs),
        in_specs=[smem,
                  pl.BlockSpec((QBLOCK, qw), q_map),
                  band(kcol, -1), band(kcol, 0), band(kcol, 1),
                  band(vcol, -1), band(vcol, 0), band(vcol, 1),
                  pl.BlockSpec((ctx_len, kw), lambda b, i, p: (b, kcol + p)),
                  pl.BlockSpec((ctx_len, kw), lambda b, i, p: (b, vcol + p))],
        out_specs=pl.BlockSpec((QBLOCK, qw), q_map),
        out_shape=jax.ShapeDtypeStruct((t, q_cols), BF16),
        compiler_params=_cparams(3), name="attention",
    )(sink, qkv, qkv, qkv, qkv, qkv, qkv, qkv, qkv, qkv)


def _rope_tables(seq, width):
    pos = jnp.arange(seq, dtype=jnp.int32)
    row_pos = (pos // GRID_W).astype(F32)
    col_pos = (pos % GRID_W).astype(F32)
    half = HEAD // 2
    inv_freq = ROPE_THETA ** (-jnp.arange(0, half, 2, dtype=F32) / half)
    ang_r = row_pos[:, None] * inv_freq
    ang_c = col_pos[:, None] * inv_freq
    cos64 = jnp.concatenate([jnp.cos(ang_r), jnp.cos(ang_r), jnp.cos(ang_c), jnp.cos(ang_c)], axis=-1)
    sin64 = jnp.concatenate([-jnp.sin(ang_r), jnp.sin(ang_r), -jnp.sin(ang_c), jnp.sin(ang_c)], axis=-1)
    reps = width // HEAD
    return jnp.tile(cos64, (1, reps)), jnp.tile(sin64, (1, reps))


def _router_kernel(x_ref, g_ref, sh_ref, sc_ref, rw_ref, rb_ref, h_ref, ids_ref, gts_ref, *, tm, n_ctx, seq):
    rid = _row_id(pl.program_id(0) * tm, n_ctx, seq)
    h = _norm_mod(x_ref[...], g_ref[...], sh_ref[pl.ds(rid, 1), :], sc_ref[pl.ds(rid, 1), :])
    h_ref[...] = h
    logits = _dot_hi_t1(rw_ref[...], h)
    e = jnp.exp(logits - jnp.max(logits, axis=0, keepdims=True))
    probs = e / jnp.sum(e, axis=0, keepdims=True)
    sel = probs + rb_ref[...]
    n_g, per = N_EXPERT_GROUPS, EXPERTS_PER_GROUP

    def row(a, r):
        return a[r:r + 1, :]

    scores = []
    for gi in range(n_g):
        a, b, c, d = (row(sel, gi * per + r) for r in range(per))
        hi1, lo1 = jnp.maximum(a, b), jnp.minimum(a, b)
        hi2, lo2 = jnp.maximum(c, d), jnp.minimum(c, d)
        scores.append(jnp.maximum(hi1, hi2) + jnp.maximum(jnp.minimum(hi1, hi2), jnp.maximum(lo1, lo2)))
    best = scores[0]
    bg = jnp.zeros_like(best, dtype=jnp.int32)
    for gi in range(1, n_g):
        better = scores[gi] > best
        best = jnp.where(better, scores[gi], best)
        bg = jnp.where(better, gi, bg)
    in_sel = []
    in_prob = []
    for r in range(per):
        s_r = row(sel, r)
        p_r = row(probs, r)
        for gi in range(1, n_g):
            s_r = jnp.where(bg == gi, row(sel, gi * per + r), s_r)
            p_r = jnp.where(bg == gi, row(probs, gi * per + r), p_r)
        in_sel.append(s_r)
        in_prob.append(p_r)

    def argmax_first(vals, exclude):
        bv = None
        for r in range(per):
            v = vals[r] if exclude is None else jnp.where(exclude == r, -jnp.inf, vals[r])
            if bv is None:
                bv, bi = v, jnp.zeros_like(bg)
            else:
                better = v > bv
                bv = jnp.where(better, v, bv)
                bi = jnp.where(better, r, bi)
        return bi

    i1 = argmax_first(in_sel, None)
    i2 = argmax_first(in_sel, i1)

    def pick(vals, idx):
        out = vals[0]
        for r in range(1, per):
            out = jnp.where(idx == r, vals[r], out)
        return out

    p1 = pick(in_prob, i1)
    p2 = pick(in_prob, i2)
    tot = p1 + p2
    ids_ref[...] = jnp.concatenate([bg * per + i1, bg * per + i2], axis=0)
    gts_ref[...] = jnp.concatenate([p1 / tot, p2 / tot], axis=0)


def router(x, g, mods, layer, router_w_t, router_b, dims, tm_pref=256):
    t, d = x.shape
    n_ctx, seq = dims
    n_e = router_w_t.shape[0]
    tm = _tile(math.gcd(n_ctx, seq), tm_pref)
    return pl.pallas_call(
        functools.partial(_router_kernel, tm=tm, n_ctx=n_ctx, seq=seq),
        grid=(t // tm,),
        in_specs=[pl.BlockSpec((tm, d), lambda i: (i, 0)),
                  pl.BlockSpec((1, d), lambda i: (0, 0)),
                  pl.BlockSpec((None, None, MOD_ROWS, d), lambda i: (layer, 3, 0, 0)),
                  pl.BlockSpec((None, None, MOD_ROWS, d), lambda i: (layer, 4, 0, 0)),
                  pl.BlockSpec((n_e, d), lambda i: (0, 0)),
                  pl.BlockSpec((n_e, 1), lambda i: (0, 0))],
        out_specs=[pl.BlockSpec((tm, d), lambda i: (i, 0)),
                   pl.BlockSpec((2, tm), lambda i: (0, i)),
                   pl.BlockSpec((2, tm), lambda i: (0, i))],
        out_shape=[jax.ShapeDtypeStruct((t, d), F32),
                   jax.ShapeDtypeStruct((2, t), jnp.int32),
                   jax.ShapeDtypeStruct((2, t), F32)],
        compiler_params=_cparams(1), name="router",
    )(x, g.reshape(1, d), mods, mods, router_w_t, router_b.reshape(n_e, 1))


def _gather(src_hbm, idx_ref, base, n_rows, dst, sem):
    def copy(r):
        return pltpu.make_async_copy(src_hbm.at[pl.ds(idx_ref[base + r], 1), :], dst.at[pl.ds(r, 1), :], sem)

    def start_row(r, c):
        copy(r).start()
        return c

    def wait_row(r, c):
        copy(r).wait()
        return c

    def start():
        lax.fori_loop(0, n_rows, start_row, 0, unroll=8)

    def wait():
        lax.fori_loop(0, n_rows, wait_row, 0, unroll=8)

    return start, wait


def _expert_kernel(te_ref, act_ref, tok_ref, h_hbm, gate_ref, wg_ref, wu_ref, wd_ref, y_ref, xbuf, sems, *, tm):
    i = pl.program_id(0)
    n = pl.num_programs(0)
    slot = i % 2
    nxt = jnp.minimum(i + 1, n - 1)

    def gather(tile, s):
        return _gather(h_hbm, tok_ref, tile * tm, tm, xbuf.at[s], sems.at[s])

    @pl.when(jnp.logical_and(i == 0, act_ref[0] == 1))
    def _():
        gather(0, 0)[0]()

    @pl.when(jnp.logical_and(i + 1 < n, act_ref[nxt] == 1))
    def _():
        gather(i + 1, 1 - slot)[0]()

    @pl.when(act_ref[i] == 1)
    def _():
        gather(i, slot)[1]()
        x = xbuf[slot].astype(BF16)
        a = _dot(x, wg_ref[...])
        hid = (a * _sigmoid(a)) * _dot(x, wu_ref[...])
        y_ref[...] = _dot(hid.astype(BF16), wd_ref[...]) * gate_ref[...]

    @pl.when(act_ref[i] == 0)
    def _():
        y_ref[...] = jnp.zeros_like(y_ref)


def expert_ffn(h, tile_expert, tile_active, slot_token, slot_gate, w_gate, w_up, w_down, tm):
    t, d = h.shape
    n_e, _, f = w_gate.shape
    n_tiles = tile_expert.shape[0]
    single = pl.Buffered(1)
    grid_spec = pltpu.PrefetchScalarGridSpec(
        num_scalar_prefetch=3,
        grid=(n_tiles,),
        in_specs=[pl.BlockSpec(memory_space=pl.ANY),
                  pl.BlockSpec((tm, 1), lambda i, te, act, tok: (i, 0)),
                  pl.BlockSpec((None, d, f), lambda i, te, act, tok: (te[i], 0, 0), pipeline_mode=single),
                  pl.BlockSpec((None, d, f), lambda i, te, act, tok: (te[i], 0, 0), pipeline_mode=single),
                  pl.BlockSpec((None, f, d), lambda i, te, act, tok: (te[i], 0, 0), pipeline_mode=single)],
        out_specs=pl.BlockSpec((tm, d), lambda i, te, act, tok: (i, 0)),
        scratch_shapes=[pltpu.VMEM((2, tm, d), F32), pltpu.SemaphoreType.DMA((2,))],
    )
    return pl.pallas_call(
        functools.partial(_expert_kernel, tm=tm),
        grid_spec=grid_spec,
        out_shape=jax.ShapeDtypeStruct((n_tiles * tm, d), F32),
        compiler_params=_cparams(1), name="expert_ffn",
    )(tile_expert, tile_active, slot_token, h, slot_gate, w_gate, w_up, w_down)


def _combine_kernel(slots_ref, x_ref, gate_ref, y_hbm, o_ref, ybuf, sems, *, tm, n_tok, n_ctx, seq):
    i = pl.program_id(0)
    n = pl.num_programs(0)
    slot = i % 2

    def gathers(tile, s):
        return [_gather(y_hbm, slots_ref, k * n_tok + tile * tm, tm, ybuf.at[s, k], sems.at[s]) for k in range(2)]

    @pl.when(i == 0)
    def _():
        for start, _ in gathers(0, 0):
            start()

    @pl.when(i + 1 < n)
    def _():
        for start, _ in gathers(i + 1, 1 - slot):
            start()

    for _, wait in gathers(i, slot):
        wait()
    rid = _row_id(i * tm, n_ctx, seq)
    o_ref[...] = x_ref[...] + gate_ref[pl.ds(rid, 1), :] * (ybuf[slot, 0] + ybuf[slot, 1])


def moe_combine(x, y_sorted, tok_slots, mods, layer, dims, tm_pref=256):
    t, d = x.shape
    n_ctx, seq = dims
    tm = _tile(math.gcd(n_ctx, seq), tm_pref)
    grid_spec = pltpu.PrefetchScalarGridSpec(
        num_scalar_prefetch=1,
        grid=(t // tm,),
        in_specs=[pl.BlockSpec((tm, d), lambda i, s: (i, 0)),
                  pl.BlockSpec((None, None, MOD_ROWS, d), lambda i, s: (layer, 5, 0, 0)),
                  pl.BlockSpec(memory_space=pl.ANY)],
        out_specs=pl.BlockSpec((tm, d), lambda i, s: (i, 0)),
        scratch_shapes=[pltpu.VMEM((2, 2, tm, d), F32), pltpu.SemaphoreType.DMA((2,))],
    )
    return pl.pallas_call(
        functools.partial(_combine_kernel, tm=tm, n_tok=t, n_ctx=n_ctx, seq=seq),
        grid_spec=grid_spec,
        out_shape=jax.ShapeDtypeStruct((t, d), F32),
        input_output_aliases={1: 0},
        compiler_params=_cparams(1), name="moe_combine",
    )(tok_slots, x, mods, y_sorted)


def _moe_plan(ids, gates, n_e, tm):
    n_tok = ids.shape[1]
    e_flat = ids.reshape(-1)
    n_pairs = e_flat.shape[0]
    onehot = (e_flat[:, None] == jnp.arange(n_e, dtype=jnp.int32)[None, :]).astype(jnp.int32)
    rank = jnp.take_along_axis(jnp.cumsum(onehot, axis=0), e_flat[:, None], axis=1)[:, 0] - 1
    counts = jnp.sum(onehot, axis=0)
    padded = (counts + tm - 1) // tm * tm
    ends = jnp.cumsum(padded)
    starts = ends - padded
    dest = starts[e_flat] + rank
    n_tiles = n_pairs // tm + n_e
    n_slots = n_tiles * tm
    pair_tok = jnp.arange(n_pairs, dtype=jnp.int32) % n_tok
    slot_token = jnp.zeros((n_slots,), jnp.int32).at[dest].set(pair_tok)
    slot_gate = jnp.zeros((n_slots,), F32).at[dest].set(gates.reshape(-1))
    tile_start = jnp.arange(n_tiles, dtype=jnp.int32) * tm
    tile_expert = jnp.minimum(jnp.searchsorted(ends, tile_start, side="right"), n_e - 1).astype(jnp.int32)
    tile_active = (tile_start < ends[-1]).astype(jnp.int32)
    last_expert = tile_expert[jnp.maximum(ends[-1] // tm - 1, 0)]
    tile_expert = jnp.where(tile_active == 1, tile_expert, last_expert)
    return tile_expert, tile_active, slot_token, slot_gate.reshape(n_slots, 1), dest.astype(jnp.int32)


def moe_layer(x, norm_g, mods, layer, router_w_t, router_b, w_gate, w_up, w_down, dims, tm_pref=256):
    n_e = w_gate.shape[0]
    h, ids, gates = router(x, norm_g, mods, layer, router_w_t, router_b, dims)
    tm = _tile(2 * x.shape[0], tm_pref)
    tile_expert, tile_active, slot_token, slot_gate, dest = _moe_plan(ids, gates, n_e, tm)
    y_sorted = expert_ffn(h, tile_expert, tile_active, slot_token, slot_gate, w_gate, w_up, w_down, tm)
    return moe_combine(x, y_sorted, dest, mods, layer, dims)


def fourier_rwkv_layer(x, mods, layer, norm_g, w_in, w_out, shift_mu, w0, w_up, a0, a_up, g_up,
                       k_k, k_a, r_k, lnx_g, lnx_b, batch, dims, fnet_tabs):
    n_ctx, ctx_len, seq = dims
    rw = w0.shape[1]
    fw = w_in.shape[1] - 3 * rw - w_up.shape[1] - a_up.shape[1] - g_up.shape[0]
    dl, il, gl = w_up.shape[1], a_up.shape[1], g_up.shape[0]
    lora_pad = 128
    rdims = (n_ctx, seq)

    def pad_cols(a, n):
        return jnp.pad(a, ((0, 0), (0, n - a.shape[1])))

    o_lo = fw + 3 * rw
    w_lo = jnp.concatenate([pad_cols(w_in[:, o_lo:o_lo + dl], lora_pad),
                            pad_cols(w_in[:, o_lo + dl:o_lo + dl + il], lora_pad),
                            w_in[:, o_lo + dl + il:]], axis=1).astype(BF16)
    mu = shift_mu.reshape(1, -1)
    mu_lo = jnp.concatenate([pad_cols(mu[:, 3 * rw:3 * rw + dl], lora_pad),
                             pad_cols(mu[:, 3 * rw + dl:3 * rw + dl + il], lora_pad),
                             mu[:, 3 * rw + dl + il:]], axis=1)
    w_up_p = jnp.pad(w_up, ((0, 0), (0, lora_pad - dl), (0, 0))).astype(BF16)
    a_up_p = jnp.pad(a_up, ((0, 0), (0, lora_pad - il), (0, 0))).astype(BF16)

    h = norm_mod(x, norm_g, mods, layer, 0, 1, rdims)
    f = matmul(h, w_in[:, :fw].astype(BF16), BF16, rdims)
    prkv = matmul(h, w_in[:, fw:o_lo].astype(BF16), F32, rdims)
    plo = matmul(h, w_lo, F32, rdims)

    r, v, kk, lw, kd, bd, g, bv = rwkv_terms(
        prkv, plo, mu[:, :3 * rw], mu_lo, w0, w_up_p, a0, a_up_p, g_up.astype(BF16),
        k_k.reshape(1, rw), k_a.reshape(1, rw), r_k.reshape(1, rw), dims, lora_pad)
    o2 = rwkv_scan(r, v, kk, lw, kd, bd, batch, dims)
    y_rw = rwkv_readout(o2, bv, g, lnx_g, lnx_b)

    cos_c, sin_c, cos_x, sin_x, cos_l, sin_l = fnet_tabs
    gc = fw // FNET_GROUPS
    y1, y2 = fnet_channel_dft(f, cos_c, sin_c)
    f_ctx = fnet_sequence_dft(y1, y2, 0, batch, ctx_len, cos_x, sin_x, gc)
    f_lat = fnet_sequence_dft(y1, y2, n_ctx, batch, seq, cos_l, sin_l, gc)
    f_mix = jnp.concatenate([f_ctx, f_lat], axis=0)
    w_out_b = w_out.astype(BF16)
    return matmul_gated_residual([f_mix, y_rw], [w_out_b[:fw], w_out_b[fw:]], x, mods, layer, 2, rdims)


def attention_layer(x, mods, layer, norm_g, w_qkv, w_o, sink, batch, dims, rope_tabs):
    n_ctx, ctx_len, seq = dims
    q_cols = w_o.shape[0]
    kv_cols = (w_qkv.shape[1] - q_cols) // 2
    cos_t, sin_t = rope_tabs
    h = norm_mod(x, norm_g, mods, layer, 0, 1, (n_ctx, seq))
    qkv = matmul(h, w_qkv.astype(BF16), BF16, (n_ctx, seq),
                 rope=(q_cols + kv_cols, cos_t, sin_t), tn_pref=cos_t.shape[1])
    att = attention(qkv, sink, batch, dims, q_cols, kv_cols)
    return matmul_gated_residual([att], [w_o.astype(BF16)], x, mods, layer, 2, (n_ctx, seq))


def kernel(x, c, ctx, c_ctx, ada_down, ada_up, ada_bias, norm1_g, norm2_g, final_g, mix_w_in, mix_w_out, shift_mu, decay_w0, decay_up, iclr_a0, iclr_up, gate_up, k_k, k_a, r_k, lnx_g, lnx_b, att_w_qkv, att_w_o, att_sink, router_w, router_b, exp_w_gate, exp_w_up, exp_w_down):
    batch, seq, d = x.shape
    ctx_len = ctx.shape[1]
    depth = ada_down.shape[0]
    n_ctx = batch * ctx_len
    assert batch + 1 <= MOD_ROWS
    dims = (n_ctx, ctx_len, seq)
    xs = jnp.concatenate([ctx.reshape(n_ctx, d), x.reshape(batch * seq, d)], axis=0)

    cvec = jnp.concatenate([c_ctx[None, :], c, jnp.zeros((MOD_ROWS - 1 - batch, d), F32)], axis=0)
    mods = adaln_all(cvec, ada_down, ada_up, ada_bias)
    mods = mods.reshape(depth, MOD_ROWS, N_MOD, d).transpose(0, 2, 1, 3)

    fw = mix_w_in.shape[2] - 3 * decay_w0.shape[2] - decay_up.shape[2] - iclr_up.shape[2] - gate_up.shape[1]
    gc = fw // FNET_GROUPS
    fnet_tabs = _dft_tables(gc) + _dft_tables(ctx_len) + _dft_tables(seq)
    kv_cols = (att_w_qkv.shape[2] - att_w_o.shape[1]) // 2
    rope_tabs = _rope_tables(seq, min(512, kv_cols))
    router_w_t = router_w.T

    for layer in range(depth):
        j = layer // 2
        if layer % 2 == 0:
            xs = fourier_rwkv_layer(xs, mods, layer, norm1_g[layer], mix_w_in[j], mix_w_out[j], shift_mu[j],
                                    decay_w0[j], decay_up[j], iclr_a0[j], iclr_up[j], gate_up[j],
                                    k_k[j], k_a[j], r_k[j], lnx_g[j], lnx_b[j], batch, dims, fnet_tabs)
        else:
            xs = attention_layer(xs, mods, layer, norm1_g[layer], att_w_qkv[j], att_w_o[j], att_sink[j],
                                 batch, dims, rope_tabs)
        xs = moe_layer(xs, norm2_g[layer], mods, layer, router_w_t, router_b,
                       exp_w_gate[layer].astype(BF16), exp_w_up[layer].astype(BF16),
                       exp_w_down[layer].astype(BF16), (n_ctx, seq))
    return final_norm(xs, final_g, n_ctx).reshape(batch, seq, d)
```

```python
import functools
import math

import jax
import jax.numpy as jnp
from jax import lax
from jax.experimental import pallas as pl
from jax.experimental.pallas import tpu as pltpu

F32 = jnp.float32
BF16 = jnp.bfloat16
HIGHEST = lax.Precision.HIGHEST

GRID_W = 64
NORM_EPS = 1e-6
GN_EPS = 64e-5
L2_EPS = 1e-12
N_MOD = 6
FNET_GROUPS = 4
HEAD = 64
ATT_GROUP = 8
ATT_SCALE = 0.125
WINDOW = 128
QBLOCK = 128
ROPE_THETA = 10000.0
N_EXPERT_GROUPS = 4
EXPERTS_PER_GROUP = 4
CHUNK = 64
MOD_ROWS = 8

VMEM_LIMIT = 56 * 1024 * 1024


def _cparams(n_axes):
    return pltpu.CompilerParams(dimension_semantics=("arbitrary",) * n_axes,
                                vmem_limit_bytes=VMEM_LIMIT)


def _tile(n, pref):
    t = min(n, pref)
    while n % t:
        t //= 2
    return t


def _row_id(r0, n_ctx, seq):
    return jnp.where(r0 < n_ctx, 0, 1 + (r0 - n_ctx) // seq)


def _dot(a, b):
    return jnp.dot(a, b, preferred_element_type=F32)


def _dot_hi(a, b):
    return jnp.dot(a, b, preferred_element_type=F32, precision=HIGHEST)


def _dot_t0(a, b):
    return lax.dot_general(a, b, (((0,), (0,)), ((), ())), preferred_element_type=F32)


def _dot_t1(a, b):
    return lax.dot_general(a, b, (((1,), (1,)), ((), ())), preferred_element_type=F32)


def _dot_hi_t1(a, b):
    return lax.dot_general(a, b, (((1,), (1,)), ((), ())), preferred_element_type=F32, precision=HIGHEST)


def _sigmoid(x):
    return 1.0 / (1.0 + jnp.exp(-x))


def _norm_mod(x, g, sh, sc):
    ms = jnp.mean(x * x, axis=-1, keepdims=True)
    y = x * lax.rsqrt(ms + NORM_EPS) * g
    return y * (1.0 + sc) + sh


def _adaln_kernel(cv_ref, down_ref, up_ref, bias_ref, o_ref, t_scr):
    @pl.when(pl.program_id(1) == 0)
    def _():
        cv = cv_ref[...]
        s = cv * _sigmoid(cv)
        t_scr[...] = _dot(s.astype(BF16), down_ref[...].astype(BF16)).astype(BF16)
    o_ref[...] = _dot(t_scr[...], up_ref[...].astype(BF16)) + bias_ref[...]


def adaln_all(cvec, ada_down, ada_up, ada_bias):
    depth, d, rank = ada_down.shape
    n = ada_up.shape[2]
    tn = _tile(n, 2048)
    bias = ada_bias.reshape(depth, 1, n)
    return pl.pallas_call(
        _adaln_kernel,
        grid=(depth, n // tn),
        in_specs=[pl.BlockSpec((MOD_ROWS, d), lambda l, j: (0, 0)),
                  pl.BlockSpec((None, d, rank), lambda l, j: (l, 0, 0)),
                  pl.BlockSpec((None, rank, tn), lambda l, j: (l, 0, j)),
                  pl.BlockSpec((None, 1, tn), lambda l, j: (l, 0, j))],
        out_specs=pl.BlockSpec((None, MOD_ROWS, tn), lambda l, j: (l, 0, j)),
        out_shape=jax.ShapeDtypeStruct((depth, MOD_ROWS, n), F32),
        scratch_shapes=[pltpu.VMEM((MOD_ROWS, rank), BF16)],
        compiler_params=_cparams(2), name="adaln",
    )(cvec, ada_down, ada_up, bias)


def _norm_mod_kernel(x_ref, g_ref, sh_ref, sc_ref, h_ref, *, tm, n_ctx, seq):
    rid = _row_id(pl.program_id(0) * tm, n_ctx, seq)
    h = _norm_mod(x_ref[...], g_ref[...], sh_ref[pl.ds(rid, 1), :], sc_ref[pl.ds(rid, 1), :])
    h_ref[...] = h.astype(h_ref.dtype)


def norm_mod(x, g, mods, layer, k_shift, k_scale, dims, tm_pref=512):
    t, d = x.shape
    n_ctx, seq = dims
    tm = _tile(math.gcd(n_ctx, seq), tm_pref)
    return pl.pallas_call(
        functools.partial(_norm_mod_kernel, tm=tm, n_ctx=n_ctx, seq=seq),
        grid=(t // tm,),
        in_specs=[pl.BlockSpec((tm, d), lambda i: (i, 0)),
                  pl.BlockSpec((1, d), lambda i: (0, 0)),
                  pl.BlockSpec((None, None, MOD_ROWS, d), lambda i: (layer, k_shift, 0, 0)),
                  pl.BlockSpec((None, None, MOD_ROWS, d), lambda i: (layer, k_scale, 0, 0))],
        out_specs=pl.BlockSpec((tm, d), lambda i: (i, 0)),
        out_shape=jax.ShapeDtypeStruct((t, d), BF16),
        compiler_params=_cparams(1), name="norm_mod",
    )(x, g.reshape(1, d), mods, mods)


def _matmul_kernel(*refs, tm, n_ctx, rope_cols):
    if rope_cols:
        a_ref, w_ref, cos_ref, sin_ref, o_ref = refs
    else:
        a_ref, w_ref, o_ref = refs
    r0 = pl.program_id(0) * tm
    j = pl.program_id(1)
    acc = _dot(a_ref[...], w_ref[...])
    if rope_cols:
        tn = acc.shape[1]
        do_rope = jnp.logical_and(r0 >= n_ctx, j * tn < rope_cols)

        @pl.when(do_rope)
        def _():
            lane = lax.broadcasted_iota(jnp.int32, acc.shape, 1)
            swapped = jnp.where(lane % 32 < 16, pltpu.roll(acc, tn - 16, 1), pltpu.roll(acc, 16, 1))
            o_ref[...] = (acc * cos_ref[...] + swapped * sin_ref[...]).astype(o_ref.dtype)

        @pl.when(jnp.logical_not(do_rope))
        def _():
            o_ref[...] = acc.astype(o_ref.dtype)
    else:
        o_ref[...] = acc.astype(o_ref.dtype)


def matmul(a, w, out_dtype, dims, rope=None, cols=None, tm_pref=1024, tn_pref=512):
    t, d = a.shape
    col0, n = cols if cols is not None else (0, w.shape[1])
    n_ctx, seq = dims
    tm = _tile(math.gcd(n_ctx, seq), tm_pref)
    tn = _tile(math.gcd(n, col0) if col0 else n, tn_pref)
    jb = col0 // tn
    in_specs = [pl.BlockSpec((tm, d), lambda i, j: (i, 0)),
                pl.BlockSpec((d, tn), lambda i, j: (0, jb + j))]
    args = [a, w]
    rope_cols = 0
    if rope is not None:
        rope_cols, cos_t, sin_t = rope
        assert rope_cols % tn == 0 and cos_t.shape == (seq, tn)
        pos_map = lambda i, j: (jnp.maximum(i * tm - n_ctx, 0) % seq // tm, 0)
        in_specs += [pl.BlockSpec((tm, tn), pos_map), pl.BlockSpec((tm, tn), pos_map)]
        args += [cos_t, sin_t]
    return pl.pallas_call(
        functools.partial(_matmul_kernel, tm=tm, n_ctx=n_ctx, rope_cols=rope_cols),
        grid=(t // tm, n // tn),
        in_specs=in_specs,
        out_specs=pl.BlockSpec((tm, tn), lambda i, j: (i, j)),
        out_shape=jax.ShapeDtypeStruct((t, n), out_dtype),
        compiler_params=_cparams(2), name="matmul",
    )(*args)


def _mmres_kernel(*refs, n_a, tm, n_ctx, seq):
    a_refs = refs[:n_a]
    w_refs = refs[n_a:2 * n_a]
    x_ref, gate_ref, o_ref = refs[2 * n_a:]
    acc = _dot(a_refs[0][...], w_refs[0][...])
    for a_ref, w_ref in zip(a_refs[1:], w_refs[1:]):
        acc += _dot(a_ref[...], w_ref[...])
    rid = _row_id(pl.program_id(0) * tm, n_ctx, seq)
    o_ref[...] = x_ref[...] + gate_ref[pl.ds(rid, 1), :] * acc


def matmul_gated_residual(a_list, w_list, x, mods, layer, k_gate, dims, tm_pref=1024, tn_pref=512):
    t, d = x.shape
    n_ctx, seq = dims
    tm = _tile(math.gcd(n_ctx, seq), tm_pref)
    tn = _tile(d, tn_pref)
    n_a = len(a_list)
    in_specs = [pl.BlockSpec((tm, a.shape[1]), lambda i, j: (i, 0)) for a in a_list]
    in_specs += [pl.BlockSpec((w.shape[0], tn), lambda i, j: (0, j)) for w in w_list]
    in_specs += [pl.BlockSpec((tm, tn), lambda i, j: (i, j)),
                 pl.BlockSpec((None, None, MOD_ROWS, tn), lambda i, j: (layer, k_gate, 0, j))]
    return pl.pallas_call(
        functools.partial(_mmres_kernel, n_a=n_a, tm=tm, n_ctx=n_ctx, seq=seq),
        grid=(t // tm, d // tn),
        in_specs=in_specs,
        out_specs=pl.BlockSpec((tm, tn), lambda i, j: (i, j)),
        out_shape=jax.ShapeDtypeStruct((t, d), F32),
        input_output_aliases={2 * n_a: 0},
        compiler_params=_cparams(2), name="matmul_gated_residual",
    )(*a_list, *w_list, x, mods)


def _final_norm_kernel(x_ref, g_ref, o_ref):
    x = x_ref[...]
    ms = jnp.mean(x * x, axis=-1, keepdims=True)
    o_ref[...] = x * lax.rsqrt(ms + NORM_EPS) * g_ref[...]


def final_norm(x, g, n_ctx, tm_pref=512):
    t, d = x.shape
    tm = _tile(math.gcd(n_ctx, t - n_ctx), tm_pref)
    off = n_ctx // tm
    return pl.pallas_call(
        _final_norm_kernel,
        grid=((t - n_ctx) // tm,),
        in_specs=[pl.BlockSpec((tm, d), lambda i: (i + off, 0)),
                  pl.BlockSpec((1, d), lambda i: (0, 0))],
        out_specs=pl.BlockSpec((tm, d), lambda i: (i, 0)),
        out_shape=jax.ShapeDtypeStruct((t - n_ctx, d), F32),
        compiler_params=_cparams(1), name="final_norm",
    )(x, g.reshape(1, d))


def _dft_tables(n, scale=1.0):
    j = jnp.arange(n, dtype=jnp.int32)
    m = (j[:, None] * j[None, :]) % n
    ang = m.astype(F32) * (2.0 * math.pi / n)
    return (jnp.cos(ang) * scale).astype(BF16), (jnp.sin(ang) * scale).astype(BF16)


def _fnet_ch_kernel(x_ref, c_ref, s_ref, y1_ref, y2_ref):
    x = x_ref[...]
    y1_ref[...] = _dot(x, c_ref[...]).astype(y1_ref.dtype)
    y2_ref[...] = _dot(x, s_ref[...]).astype(y2_ref.dtype)


def fnet_channel_dft(f, cos_c, sin_c, tm_pref=1024):
    t, width = f.shape
    gc = width // FNET_GROUPS
    tm = _tile(t, tm_pref)
    spec = pl.BlockSpec((tm, gc), lambda i, g: (i, g))
    tab = pl.BlockSpec((gc, gc), lambda i, g: (0, 0))
    return pl.pallas_call(
        _fnet_ch_kernel,
        grid=(t // tm, FNET_GROUPS),
        in_specs=[spec, tab, tab],
        out_specs=[spec, spec],
        out_shape=[jax.ShapeDtypeStruct((t, width), BF16)] * 2,
        compiler_params=_cparams(2), name="fnet_channel_dft",
    )(f, cos_c, sin_c)


def _fnet_seq_kernel(c_ref, s_ref, y1_ref, y2_ref, o_ref, acc_ref, *, scale):
    k = pl.program_id(2)

    @pl.when(k == 0)
    def _():
        acc_ref[...] = jnp.zeros_like(acc_ref)

    acc_ref[...] += _dot(c_ref[...], y1_ref[...]) - _dot(s_ref[...], y2_ref[...])

    @pl.when(k == pl.num_programs(2) - 1)
    def _():
        o_ref[...] = (acc_ref[...] * scale).astype(o_ref.dtype)


def fnet_sequence_dft(y1, y2, row0, n_seq, length, cos_l, sin_l, gc, t_pref=512):
    width = y1.shape[1]
    tm = _tile(length, t_pref)
    tk = _tile(math.gcd(length, row0) if row0 else length, t_pref)
    nb = length // tm
    nk = length // tk
    scale = 1.0 / math.sqrt(length * gc)
    rb0 = row0 // tk
    y_spec = pl.BlockSpec((tk, width), lambda b, i, k: (rb0 + b * nk + k, 0))
    return pl.pallas_call(
        functools.partial(_fnet_seq_kernel, scale=scale),
        grid=(n_seq, nb, nk),
        in_specs=[pl.BlockSpec((tm, tk), lambda b, i, k: (i, k)),
                  pl.BlockSpec((tm, tk), lambda b, i, k: (i, k)),
                  y_spec, y_spec],
        out_specs=pl.BlockSpec((tm, width), lambda b, i, k: (b * nb + i, 0)),
        out_shape=jax.ShapeDtypeStruct((n_seq * length, width), BF16),
        scratch_shapes=[pltpu.VMEM((tm, width), F32)],
        compiler_params=_cparams(3), name="fnet_sequence_dft",
    )(cos_l, sin_l, y1, y2)


def _head_sum(x, ones_bd):
    hi = x.astype(BF16)
    lo = (x - hi.astype(F32)).astype(BF16)
    return _dot(hi, ones_bd) + _dot(lo, ones_bd)


def _shift(x, prev_row, next_row, mu, pos, length):
    rows = x.shape[0]
    ridx = lax.broadcasted_iota(jnp.int32, (rows, 1), 0)
    prev = jnp.where(ridx == 0, prev_row, pltpu.roll(x, 1, 0))
    nxt = jnp.where(ridx == rows - 1, next_row, pltpu.roll(x, rows - 1, 0))
    prev = jnp.where(pos == 0, 0.0, prev)
    nxt = jnp.where(pos == length - 1, 0.0, nxt)
    return x + mu * (0.5 * (prev + nxt) - x)


def _rwkv_terms_kernel(r_ref, k_ref, v_ref, rp_ref, kp_ref, vp_ref, rn_ref, kn_ref, vn_ref,
                       lo_ref, lop_ref, lon_ref, mur_ref, muk_ref, muv_ref, mulo_ref,
                       w0_ref, wup_ref, a0_ref, aup_ref, gup_ref, kk_ref, ka_ref, rk_ref, ones_ref,
                       r_o, v_o, kk_o, lw_o, kd_o, bd_o, g_o, bv_o, *, tm, n_ctx, ctx_len, seq, lora_pad):
    i = pl.program_id(0)
    r0 = i * tm
    ridx = lax.broadcasted_iota(jnp.int32, (tm, 1), 0) + r0
    in_ctx = r0 < n_ctx
    length = jnp.where(in_ctx, ctx_len, seq)
    pos = jnp.where(in_ctx, ridx % ctx_len, (ridx - n_ctx) % seq)

    def sh(ref, pref, nref, mu_ref):
        last = pref.shape[0] - 1
        return _shift(ref[...].astype(F32), pref[last:last + 1, :].astype(F32), nref[0:1, :].astype(F32),
                      mu_ref[...], pos, length)

    r = sh(r_ref, rp_ref, rn_ref, mur_ref)
    k = sh(k_ref, kp_ref, kn_ref, muk_ref)
    v = sh(v_ref, vp_ref, vn_ref, muv_ref)
    lo = sh(lo_ref, lop_ref, lon_ref, mulo_ref)
    w_in = jnp.tanh(lo[:, :lora_pad]).astype(BF16)
    a_in = lo[:, lora_pad:2 * lora_pad].astype(BF16)
    g_in = _sigmoid(lo[:, 2 * lora_pad:]).astype(BF16)
    ones_bd = ones_ref[...]

    kk = k * kk_ref[...]
    kk = kk * lax.rsqrt(_head_sum(kk * kk, ones_bd) + L2_EPS)
    ksum = jnp.zeros_like(k)
    for d in range(2):
        w_logit = w0_ref[d:d + 1, :] + _dot(w_in, wup_ref[d])
        lw_o[d] = -math.exp(-0.5) * _sigmoid(w_logit)
        a = _sigmoid(a0_ref[d:d + 1, :] + _dot(a_in, aup_ref[d]))
        k_d = k * (1.0 + (a - 1.0) * ka_ref[...])
        kd_o[d] = k_d.astype(kd_o.dtype)
        bd_o[d] = (kk * a).astype(bd_o.dtype)
        ksum = ksum + k_d
    g_o[...] = _dot(g_in, gup_ref[...]).astype(g_o.dtype)
    bv_o[...] = (_head_sum(r * ksum * rk_ref[...], ones_bd) * v).astype(bv_o.dtype)
    r_o[...] = r.astype(r_o.dtype)
    v_o[...] = v.astype(v_o.dtype)
    kk_o[...] = kk.astype(kk_o.dtype)


def rwkv_terms(prkv, plo, mu_rkv, mu_lo, w0, w_up, a0, a_up, g_up, k_k, k_a, r_k, dims, lora_pad, tm_pref=256):
    t = prkv.shape[0]
    rw = prkv.shape[1] // 3
    n_ctx, ctx_len, seq = dims
    tm = _tile(math.gcd(n_ctx, seq), tm_pref)
    tn = _tile(rw, 512)
    nj = rw // tn
    lo_w = plo.shape[1]
    hb = tm // 8
    last8 = t // 8 - 1
    hr = 8 * (4 // prkv.dtype.itemsize)
    hbr = tm // hr
    lastr = t // hr - 1

    def main(c):
        return pl.BlockSpec((tm, tn), lambda i, j: (i, c * nj + j))

    def prev(c):
        return pl.BlockSpec((hr, tn), lambda i, j: (jnp.maximum(i * hbr - 1, 0), c * nj + j))

    def nxt(c):
        return pl.BlockSpec((hr, tn), lambda i, j: (jnp.minimum((i + 1) * hbr, lastr), c * nj + j))

    def vec(c=0):
        return pl.BlockSpec((1, tn), lambda i, j: (0, c * nj + j))

    ones_bd = (jnp.arange(tn)[:, None] // HEAD == jnp.arange(tn)[None, :] // HEAD).astype(BF16)
    out_tok = pl.BlockSpec((tm, tn), lambda i, j: (i, j))
    out_dir = pl.BlockSpec((2, tm, tn), lambda i, j: (0, i, j))
    tok = jax.ShapeDtypeStruct((t, rw), BF16)
    tok2 = jax.ShapeDtypeStruct((2, t, rw), BF16)
    lw2 = jax.ShapeDtypeStruct((2, t, rw), F32)
    return pl.pallas_call(
        functools.partial(_rwkv_terms_kernel, tm=tm, n_ctx=n_ctx, ctx_len=ctx_len, seq=seq, lora_pad=lora_pad),
        grid=(t // tm, nj),
        in_specs=[main(0), main(1), main(2), prev(0), prev(1), prev(2), nxt(0), nxt(1), nxt(2),
                  pl.BlockSpec((tm, lo_w), lambda i, j: (i, 0)),
                  pl.BlockSpec((8, lo_w), lambda i, j: (jnp.maximum(i * hb - 1, 0), 0)),
                  pl.BlockSpec((8, lo_w), lambda i, j: (jnp.minimum((i + 1) * hb, last8), 0)),
                  vec(0), vec(1), vec(2),
                  pl.BlockSpec((1, lo_w), lambda i, j: (0, 0)),
                  pl.BlockSpec((2, tn), lambda i, j: (0, j)),
                  pl.BlockSpec((2, lora_pad, tn), lambda i, j: (0, 0, j)),
                  pl.BlockSpec((2, tn), lambda i, j: (0, j)),
                  pl.BlockSpec((2, lora_pad, tn), lambda i, j: (0, 0, j)),
                  pl.BlockSpec((g_up.shape[0], tn), lambda i, j: (0, j)),
                  vec(), vec(), vec(),
                  pl.BlockSpec((tn, tn), lambda i, j: (0, 0))],
        out_specs=[out_tok, out_tok, out_tok, out_dir, out_dir, out_dir, out_tok, out_tok],
        out_shape=[tok, tok, tok, lw2, tok2, tok2, tok, tok],
        compiler_params=_cparams(2), name="rwkv_terms",
    )(prkv, prkv, prkv, prkv, prkv, prkv, prkv, prkv, prkv, plo, plo, plo,
      mu_rkv, mu_rkv, mu_rkv, mu_lo, w0, w_up, a0, a_up, g_up, k_k, k_a, r_k, ones_bd)


def _rwkv_scan_kernel(r_ref, v_ref, kk_ref, lw_ref, kd_ref, bd_ref, o_ref,
                      s_scr, rt_s, at_s, kt_s, bt_s, bh_s, kh_s, vb_s, pd_s, *, n_heads, group):
    d = pl.program_id(1)
    step = pl.program_id(2)
    c = CHUNK

    @pl.when(step == 0)
    def _():
        s_scr[...] = jnp.zeros_like(s_scr)

    row = lax.broadcasted_iota(jnp.int32, (c, c), 0)
    col = lax.broadcasted_iota(jnp.int32, (c, c), 1)
    fwd = d == 0
    ahead = (row - col) * jnp.where(fwd, 1, -1)
    incl = ahead >= 0
    strict = ahead > 0
    eye = row == col
    eye_f = jnp.where(eye, 1.0, 0.0)

    tri = jnp.where(incl, 1.0, 0.0).astype(BF16)
    lw = lw_ref[...]
    lw_hi = lw.astype(BF16)
    lw_lo = (lw - lw_hi.astype(F32)).astype(BF16)
    l_inc = _dot(tri, lw_hi) + _dot(tri, lw_lo)
    l_tot = jnp.where(fwd, l_inc[c - 1:c, :], l_inc[0:1, :])
    e_neg = jnp.exp(-l_inc)
    e_rem = jnp.exp(l_tot - l_inc)
    kd = kd_ref[...]
    bd = bd_ref[...]
    rt_s[...] = (r_ref[...] * jnp.exp(l_inc)).astype(BF16)
    at_s[...] = (-kk_ref[...] * jnp.exp(l_inc - lw)).astype(BF16)
    kt_s[...] = (kd * e_neg).astype(BF16)
    bt_s[...] = (bd * e_neg).astype(BF16)
    bh_s[...] = (bd * e_rem).astype(BF16)
    kh_s[...] = (kd * e_rem).astype(BF16)
    vb_s[...] = v_ref[...].astype(BF16)
    pd_s[...] = jnp.exp(l_tot)

    gw = group * HEAD

    def group_body(gi, carry):
        sl = pl.ds(pl.multiple_of(gi * gw, gw), gw)
        rt_g, at_g, kt_g, bt_g, bh_g, kh_g, v_g = (s[:, sl] for s in (rt_s, at_s, kt_s, bt_s, bh_s, kh_s, vb_s))
        pd_g = pd_s[:, sl]
        heads = range(group)

        def hd(a, q):
            return a[:, q * HEAD:(q + 1) * HEAD]

        lhs = [jnp.concatenate([hd(at_g, q), hd(rt_g, q)], axis=0) for q in heads]
        gb = [_dot_t1(lhs[q], hd(bt_g, q)) for q in heads]
        gk = [_dot_t1(lhs[q], hd(kt_g, q)) for q in heads]
        a_ab = [jnp.where(strict, gb[q][:c], 0.0) for q in heads]
        a_rb = [jnp.where(incl, gb[q][c:], 0.0).astype(BF16) for q in heads]
        a_k = [jnp.concatenate([jnp.where(strict, gk[q][:c], 0.0), jnp.where(incl, gk[q][c:], 0.0)],
                               axis=0).astype(BF16) for q in heads]
        vv = [_dot(a_k[q], hd(v_g, q)) for q in heads]
        kv = [_dot_t0(hd(kh_g, q), hd(v_g, q)) for q in heads]
        x = [a.astype(BF16) for a in a_ab]
        tinv = [eye_f + a for a in a_ab]
        for _ in range(int(math.log2(c)) - 1):
            x = [_dot(x[q], x[q]).astype(BF16) for q in heads]
            tinv = [tinv[q] + _dot(tinv[q].astype(BF16), x[q]) for q in heads]
        tb = [t.astype(BF16) for t in tinv]
        wt = [_dot(tb[q], hd(at_g, q)).astype(BF16) for q in heads]
        u0 = [_dot(tb[q], vv[q][:c].astype(BF16)).astype(BF16) for q in heads]
        qm = [hd(rt_g, q).astype(F32) + _dot(a_rb[q], wt[q]) for q in heads]
        o0 = [_dot(a_rb[q], u0[q]) + vv[q][c:] for q in heads]
        m = [eye_f * hd(pd_g, q) + _dot_t0(hd(bh_g, q), wt[q]) for q in heads]
        nn = [_dot_t0(hd(bh_g, q), u0[q]) + kv[q] for q in heads]
        h0 = gi * group
        st = s_scr[pl.ds(h0, group)]
        res = [_dot(jnp.concatenate([qm[q], m[q]], axis=0).astype(BF16), st[q].astype(BF16)) for q in heads]
        s_scr[pl.ds(h0, group)] = jnp.stack([res[q][c:] + nn[q] for q in heads])
        o_ref[:, sl] = jnp.concatenate([res[q][:c] + o0[q] for q in heads], axis=1)
        return carry

    lax.fori_loop(0, n_heads // group, group_body, 0)


def rwkv_scan(r, v, kk, lw, kd, bd, batch, dims):
    t, rw = r.shape
    n_ctx, ctx_len, seq = dims
    c = CHUNK
    n_cc = ctx_len // c
    n_lc = seq // c
    steps = n_cc + n_lc

    def blk(b, d, s):
        ctx_i = jnp.where(d == 0, s, n_cc - 1 - s)
        lat_i = jnp.where(d == 0, s - n_cc, n_lc - 1 - (s - n_cc))
        return jnp.where(s < n_cc, b * n_cc + ctx_i, n_ctx // c + b * n_lc + lat_i)

    shared = pl.BlockSpec((c, rw), lambda b, d, s: (blk(b, d, s), 0))
    per_dir = pl.BlockSpec((None, c, rw), lambda b, d, s: (d, blk(b, d, s), 0))
    n_heads = rw // HEAD
    group = 16 if n_heads % 16 == 0 else 2
    prep = pltpu.VMEM((c, rw), BF16)
    return pl.pallas_call(
        functools.partial(_rwkv_scan_kernel, n_heads=n_heads, group=group),
        grid=(batch, 2, steps),
        in_specs=[shared, shared, shared, per_dir, per_dir, per_dir],
        out_specs=per_dir,
        out_shape=jax.ShapeDtypeStruct((2, t, rw), F32),
        scratch_shapes=[pltpu.VMEM((n_heads, HEAD, HEAD), F32)] + [prep] * 7 + [pltpu.VMEM((1, rw), F32)],
        compiler_params=_cparams(3), name="rwkv_scan",
    )(r, v, kk, lw, kd, bd)


def _rwkv_readout_kernel(o_ref, bv_ref, g_ref, lg_ref, lb_ref, ones_ref, y_ref):
    o = o_ref[0] + o_ref[1]
    ones_bd = ones_ref[...]
    mu = _head_sum(o, ones_bd) * (1.0 / HEAD)
    dev = o - mu
    var = _head_sum(dev * dev, ones_bd) * (1.0 / HEAD)
    on = dev * lax.rsqrt(var + GN_EPS) * lg_ref[...] + lb_ref[...]
    y_ref[...] = ((on + bv_ref[...]) * g_ref[...]).astype(y_ref.dtype)


def rwkv_readout(o2, bv, g, lnx_g, lnx_b, tm_pref=512):
    _, t, rw = o2.shape
    tm = _tile(t, tm_pref)
    tn = _tile(rw, 512)
    ones_bd = (jnp.arange(tn)[:, None] // HEAD == jnp.arange(tn)[None, :] // HEAD).astype(BF16)
    tok = pl.BlockSpec((tm, tn), lambda i, j: (i, j))
    vec = pl.BlockSpec((1, tn), lambda i, j: (0, j))
    return pl.pallas_call(
        _rwkv_readout_kernel,
        grid=(t // tm, rw // tn),
        in_specs=[pl.BlockSpec((2, tm, tn), lambda i, j: (0, i, j)), tok, tok, vec, vec,
                  pl.BlockSpec((tn, tn), lambda i, j: (0, 0))],
        out_specs=tok,
        out_shape=jax.ShapeDtypeStruct((t, rw), BF16),
        compiler_params=_cparams(2), name="rwkv_readout",
    )(o2, bv, g, lnx_g.reshape(1, rw), lnx_b.reshape(1, rw), ones_bd)


def _attend_group(q_heads, k_all, v_all, bias, sinks):
    nq = bias.shape[0]
    s = _dot_t1(jnp.concatenate(q_heads, axis=0), k_all)
    probs, denoms = [], []
    for g, sink in enumerate(sinks):
        sg = s[g * nq:(g + 1) * nq] + bias
        m = jnp.maximum(jnp.max(sg, axis=-1, keepdims=True), sink)
        p = jnp.exp(sg - m)
        denoms.append(jnp.sum(p, axis=-1, keepdims=True) + jnp.exp(sink - m))
        probs.append(p.astype(BF16))
    o = _dot(jnp.concatenate(probs, axis=0), v_all)
    return [o[g * nq:(g + 1) * nq] / denoms[g] for g in range(len(sinks))]


def _attn_kernel(sink_ref, q_ref, kp_ref, kc_ref, kn_ref, vp_ref, vc_ref, vn_ref, kx_ref, vx_ref, o_ref,
                 *, seq, n_ctx_keys, n_ctx_blocks):
    pair = pl.program_id(2)
    qi = pl.program_id(1) - n_ctx_blocks
    span = QBLOCK + 2 * WINDOW
    rr = lax.broadcasted_iota(jnp.int32, (QBLOCK, span + n_ctx_keys), 0)
    cc = lax.broadcasted_iota(jnp.int32, (QBLOCK, span + n_ctx_keys), 1)
    key_pos = qi * QBLOCK - WINDOW + cc
    off = cc - WINDOW - rr
    in_band = (jnp.abs(off) <= WINDOW) & (key_pos >= 0) & (key_pos < seq) & (qi >= 0)
    bias = jnp.where(jnp.logical_or(cc >= span, in_band), 0.0, -jnp.inf)
    q_all = q_ref[...] * jnp.asarray(ATT_SCALE, q_ref.dtype)
    outs = []
    for kv in range(2):
        hs = slice(kv * HEAD, (kv + 1) * HEAD)
        k_all = jnp.concatenate([kp_ref[:, hs], kc_ref[:, hs], kn_ref[:, hs], kx_ref[:, hs]], axis=0)
        v_all = jnp.concatenate([vp_ref[:, hs], vc_ref[:, hs], vn_ref[:, hs], vx_ref[:, hs]], axis=0)
        heads = [kv * ATT_GROUP + g for g in range(ATT_GROUP)]
        outs += _attend_group([q_all[:, hq * HEAD:(hq + 1) * HEAD] for hq in heads], k_all, v_all, bias,
                              [sink_ref[pair * 2 * ATT_GROUP + hq] for hq in heads])
    o_ref[...] = jnp.concatenate(outs, axis=1).astype(o_ref.dtype)


def attention(qkv, sink, batch, dims, q_cols, kv_cols):
    t = qkv.shape[0]
    n_ctx, ctx_len, seq = dims
    n_pairs = kv_cols // (2 * HEAD)
    qw = 2 * ATT_GROUP * HEAD
    kw = 2 * HEAD
    nqb = seq // QBLOCK
    ncb = ctx_len // QBLOCK
    lat0 = n_ctx // QBLOCK
    kcol = q_cols // kw
    vcol = (q_cols + kv_cols) // kw
    smem = pl.BlockSpec(memory_space=pltpu.SMEM)

    def q_map(b, i, p):
        return (jnp.where(i < ncb, b * ncb + i, lat0 + b * nqb + i - ncb), p)

    def band(col0, shift):
        def imap(b, i, p):
            return (lat0 + b * nqb + jnp.clip(i - ncb + shift, 0, nqb - 1), col0 + p)
        return pl.BlockSpec((QBLOCK, kw), imap)

    return pl.pallas_call(
        functools.partial(_attn_kernel, seq=seq, n_ctx_keys=ctx_len, n_ctx_blocks=ncb),
        grid=(batch, ncb + nqb, n_pairs),
        in_specs=[smem,
                  pl.BlockSpec((QBLOCK, qw), q_map),
                  band(kcol, -1), band(kcol, 0), band(kcol, 1),
                  band(vcol, -1), band(vcol, 0), band(vcol, 1),
                  pl.BlockSpec((ctx_len, kw), lambda b, i, p: (b, kcol + p)),
                  pl.BlockSpec((ctx_len, kw), lambda b, i, p: (b, vcol + p))],
        out_specs=pl.BlockSpec((QBLOCK, qw), q_map),
        out_shape=jax.ShapeDtypeStruct((t, q_cols), BF16),
        compiler_params=_cparams(3), name="attention",
    )(sink, qkv, qkv, qkv, qkv, qkv, qkv, qkv, qkv, qkv)


def _rope_tables(seq, width):
    pos = jnp.arange(seq, dtype=jnp.int32)
    row_pos = (pos // GRID_W).astype(F32)
    col_pos = (pos % GRID_W).astype(F32)
    half = HEAD // 2
    inv_freq = ROPE_THETA ** (-jnp.arange(0, half, 2, dtype=F32) / half)
    ang_r = row_pos[:, None] * inv_freq
    ang_c = col_pos[:, None] * inv_freq
    cos64 = jnp.concatenate([jnp.cos(ang_r), jnp.cos(ang_r), jnp.cos(ang_c), jnp.cos(ang_c)], axis=-1)
    sin64 = jnp.concatenate([-jnp.sin(ang_r), jnp.sin(ang_r), -jnp.sin(ang_c), jnp.sin(ang_c)], axis=-1)
    reps = width // HEAD
    return jnp.tile(cos64, (1, reps)), jnp.tile(sin64, (1, reps))


def _router_kernel(x_ref, g_ref, sh_ref, sc_ref, rw_ref, rb_ref, h_ref, ids_ref, gts_ref, *, tm, n_ctx, seq):
    rid = _row_id(pl.program_id(0) * tm, n_ctx, seq)
    h = _norm_mod(x_ref[...], g_ref[...], sh_ref[pl.ds(rid, 1), :], sc_ref[pl.ds(rid, 1), :])
    h_ref[...] = h
    logits = _dot_hi(h, rw_ref[...]).T[:rb_ref.shape[0]]
    e = jnp.exp(logits - jnp.max(logits, axis=0, keepdims=True))
    probs = e / jnp.sum(e, axis=0, keepdims=True)
    sel = probs + rb_ref[...]
    n_g, per = N_EXPERT_GROUPS, EXPERTS_PER_GROUP

    def row(a, r):
        return a[r:r + 1, :]

    scores = []
    for gi in range(n_g):
        a, b, c, d = (row(sel, gi * per + r) for r in range(per))
        hi1, lo1 = jnp.maximum(a, b), jnp.minimum(a, b)
        hi2, lo2 = jnp.maximum(c, d), jnp.minimum(c, d)
        scores.append(jnp.maximum(hi1, hi2) + jnp.maximum(jnp.minimum(hi1, hi2), jnp.maximum(lo1, lo2)))
    best = scores[0]
    bg = jnp.zeros_like(best, dtype=jnp.int32)
    for gi in range(1, n_g):
        better = scores[gi] > best
        best = jnp.where(better, scores[gi], best)
        bg = jnp.where(better, gi, bg)
    in_sel = []
    in_prob = []
    for r in range(per):
        s_r = row(sel, r)
        p_r = row(probs, r)
        for gi in range(1, n_g):
            s_r = jnp.where(bg == gi, row(sel, gi * per + r), s_r)
            p_r = jnp.where(bg == gi, row(probs, gi * per + r), p_r)
        in_sel.append(s_r)
        in_prob.append(p_r)

    def argmax_first(vals, exclude):
        bv = None
        for r in range(per):
            v = vals[r] if exclude is None else jnp.where(exclude == r, -jnp.inf, vals[r])
            if bv is None:
                bv, bi = v, jnp.zeros_like(bg)
            else:
                better = v > bv
                bv = jnp.where(better, v, bv)
                bi = jnp.where(better, r, bi)
        return bi

    i1 = argmax_first(in_sel, None)
    i2 = argmax_first(in_sel, i1)

    def pick(vals, idx):
        out = vals[0]
        for r in range(1, per):
            out = jnp.where(idx == r, vals[r], out)
        return out

    p1 = pick(in_prob, i1)
    p2 = pick(in_prob, i2)
    tot = p1 + p2
    ids_ref[...] = jnp.concatenate([bg * per + i1, bg * per + i2], axis=0)
    gts_ref[...] = jnp.concatenate([p1 / tot, p2 / tot], axis=0)


def router(x, g, mods, layer, router_w_pad, router_b, dims, tm_pref=256):
    t, d = x.shape
    n_ctx, seq = dims
    n_e = router_b.shape[0]
    lanes = router_w_pad.shape[1]
    tm = _tile(math.gcd(n_ctx, seq), tm_pref)
    return pl.pallas_call(
        functools.partial(_router_kernel, tm=tm, n_ctx=n_ctx, seq=seq),
        grid=(t // tm,),
        in_specs=[pl.BlockSpec((tm, d), lambda i: (i, 0)),
                  pl.BlockSpec((1, d), lambda i: (0, 0)),
                  pl.BlockSpec((None, None, MOD_ROWS, d), lambda i: (layer, 3, 0, 0)),
                  pl.BlockSpec((None, None, MOD_ROWS, d), lambda i: (layer, 4, 0, 0)),
                  pl.BlockSpec((d, lanes), lambda i: (0, 0)),
                  pl.BlockSpec((n_e, 1), lambda i: (0, 0))],
        out_specs=[pl.BlockSpec((tm, d), lambda i: (i, 0)),
                   pl.BlockSpec((2, tm), lambda i: (0, i)),
                   pl.BlockSpec((2, tm), lambda i: (0, i))],
        out_shape=[jax.ShapeDtypeStruct((t, d), F32),
                   jax.ShapeDtypeStruct((2, t), jnp.int32),
                   jax.ShapeDtypeStruct((2, t), F32)],
        compiler_params=_cparams(1), name="router",
    )(x, g.reshape(1, d), mods, mods, router_w_pad, router_b.reshape(n_e, 1))


def _gather(src_hbm, idx_ref, base, n_rows, dst, sem):
    def copy(r):
        return pltpu.make_async_copy(src_hbm.at[pl.ds(idx_ref[base + r], 1), :], dst.at[pl.ds(r, 1), :], sem)

    def start_row(r, c):
        copy(r).start()
        return c

    def wait_row(r, c):
        copy(r).wait()
        return c

    def start():
        lax.fori_loop(0, n_rows, start_row, 0, unroll=8)

    def wait():
        lax.fori_loop(0, n_rows, wait_row, 0, unroll=8)

    return start, wait


def _expert_kernel(te_ref, act_ref, tok_ref, h_hbm, wg_ref, wu_ref, wd_ref, y_ref, xbuf, sems, *, tm):
    i = pl.program_id(0)
    n = pl.num_programs(0)
    slot = i % 2
    nxt = jnp.minimum(i + 1, n - 1)

    def gather(tile, s):
        return _gather(h_hbm, tok_ref, tile * tm, tm, xbuf.at[s], sems.at[s])

    @pl.when(jnp.logical_and(i == 0, act_ref[0] == 1))
    def _():
        gather(0, 0)[0]()

    @pl.when(jnp.logical_and(i + 1 < n, act_ref[nxt] == 1))
    def _():
        gather(i + 1, 1 - slot)[0]()

    @pl.when(act_ref[i] == 1)
    def _():
        gather(i, slot)[1]()
        x = xbuf[slot].astype(BF16)
        a = _dot(x, wg_ref[...])
        hid = (a * _sigmoid(a)) * _dot(x, wu_ref[...])
        y_ref[...] = _dot(hid.astype(BF16), wd_ref[...])

    @pl.when(act_ref[i] == 0)
    def _():
        y_ref[...] = jnp.zeros_like(y_ref)


def expert_ffn(h, tile_expert, tile_active, slot_token, w_gate, w_up, w_down, tm):
    t, d = h.shape
    n_e, _, f = w_gate.shape
    n_tiles = tile_expert.shape[0]
    single = pl.Buffered(1)
    grid_spec = pltpu.PrefetchScalarGridSpec(
        num_scalar_prefetch=3,
        grid=(n_tiles,),
        in_specs=[pl.BlockSpec(memory_space=pl.ANY),
                  pl.BlockSpec((None, d, f), lambda i, te, act, tok: (te[i], 0, 0), pipeline_mode=single),
                  pl.BlockSpec((None, d, f), lambda i, te, act, tok: (te[i], 0, 0), pipeline_mode=single),
                  pl.BlockSpec((None, f, d), lambda i, te, act, tok: (te[i], 0, 0), pipeline_mode=single)],
        out_specs=pl.BlockSpec((tm, d), lambda i, te, act, tok: (i, 0)),
        scratch_shapes=[pltpu.VMEM((2, tm, d), F32), pltpu.SemaphoreType.DMA((2,))],
    )
    return pl.pallas_call(
        functools.partial(_expert_kernel, tm=tm),
        grid_spec=grid_spec,
        out_shape=jax.ShapeDtypeStruct((n_tiles * tm, d), F32),
        compiler_params=_cparams(1), name="expert_ffn",
    )(tile_expert, tile_active, slot_token, h, w_gate, w_up, w_down)


def _combine_kernel(slots_ref, x_ref, gate_ref, rw_ref, y_hbm, o_ref, ybuf, sems, *, tm, n_tok, n_ctx, seq):
    i = pl.program_id(0)
    n = pl.num_programs(0)
    slot = i % 2

    def gathers(tile, s):
        return [_gather(y_hbm, slots_ref, k * n_tok + tile * tm, tm, ybuf.at[s, k], sems.at[s]) for k in range(2)]

    @pl.when(i == 0)
    def _():
        for start, _ in gathers(0, 0):
            start()

    @pl.when(i + 1 < n)
    def _():
        for start, _ in gathers(i + 1, 1 - slot):
            start()

    for _, wait in gathers(i, slot):
        wait()
    rid = _row_id(i * tm, n_ctx, seq)
    mixed = rw_ref[:, 0:1] * ybuf[slot, 0] + rw_ref[:, 1:2] * ybuf[slot, 1]
    o_ref[...] = x_ref[...] + gate_ref[pl.ds(rid, 1), :] * mixed


def moe_combine(x, y_sorted, tok_slots, route_w, mods, layer, dims, tm_pref=256):
    t, d = x.shape
    n_ctx, seq = dims
    tm = _tile(math.gcd(n_ctx, seq), tm_pref)
    grid_spec = pltpu.PrefetchScalarGridSpec(
        num_scalar_prefetch=1,
        grid=(t // tm,),
        in_specs=[pl.BlockSpec((tm, d), lambda i, s: (i, 0)),
                  pl.BlockSpec((None, None, MOD_ROWS, d), lambda i, s: (layer, 5, 0, 0)),
                  pl.BlockSpec((tm, 2), lambda i, s: (i, 0)),
                  pl.BlockSpec(memory_space=pl.ANY)],
        out_specs=pl.BlockSpec((tm, d), lambda i, s: (i, 0)),
        scratch_shapes=[pltpu.VMEM((2, 2, tm, d), F32), pltpu.SemaphoreType.DMA((2,))],
    )
    return pl.pallas_call(
        functools.partial(_combine_kernel, tm=tm, n_tok=t, n_ctx=n_ctx, seq=seq),
        grid_spec=grid_spec,
        out_shape=jax.ShapeDtypeStruct((t, d), F32),
        input_output_aliases={1: 0},
        compiler_params=_cparams(1), name="moe_combine",
    )(tok_slots, x, mods, route_w, y_sorted)


def _moe_plan(ids, n_e, tm):
    n_tok = ids.shape[1]
    e_flat = ids.reshape(-1)
    n_pairs = e_flat.shape[0]
    onehot = (e_flat[:, None] == jnp.arange(n_e, dtype=jnp.int32)[None, :]).astype(jnp.int32)
    rank = jnp.take_along_axis(jnp.cumsum(onehot, axis=0), e_flat[:, None], axis=1)[:, 0] - 1
    counts = jnp.sum(onehot, axis=0)
    padded = (counts + tm - 1) // tm * tm
    ends = jnp.cumsum(padded)
    starts = ends - padded
    dest = starts[e_flat] + rank
    n_tiles = n_pairs // tm + n_e
    n_slots = n_tiles * tm
    pair_tok = jnp.arange(n_pairs, dtype=jnp.int32) % n_tok
    slot_token = jnp.zeros((n_slots,), jnp.int32).at[dest].set(pair_tok)
    tile_start = jnp.arange(n_tiles, dtype=jnp.int32) * tm
    tile_expert = jnp.minimum(jnp.searchsorted(ends, tile_start, side="right"), n_e - 1).astype(jnp.int32)
    tile_active = (tile_start < ends[-1]).astype(jnp.int32)
    last_expert = tile_expert[jnp.maximum(ends[-1] // tm - 1, 0)]
    tile_expert = jnp.where(tile_active == 1, tile_expert, last_expert)
    return tile_expert, tile_active, slot_token, dest.astype(jnp.int32)


def moe_layer(x, norm_g, mods, layer, router_w_pad, router_b, w_gate, w_up, w_down, dims, tm_pref=256):
    n_e = w_gate.shape[0]
    h, ids, gates = router(x, norm_g, mods, layer, router_w_pad, router_b, dims)
    tm = _tile(2 * x.shape[0], tm_pref)
    tile_expert, tile_active, slot_token, dest = _moe_plan(ids, n_e, tm)
    y_sorted = expert_ffn(h, tile_expert, tile_active, slot_token, w_gate, w_up, w_down, tm)
    return moe_combine(x, y_sorted, dest, gates.T, mods, layer, dims)


def fourier_rwkv_layer(x, mods, layer, norm_g, w_in, w_in_b, w_out, shift_mu, w0, w_up, a0, a_up, g_up,
                       k_k, k_a, r_k, lnx_g, lnx_b, batch, dims, fnet_tabs):
    n_ctx, ctx_len, seq = dims
    rw = w0.shape[1]
    fw = w_in.shape[1] - 3 * rw - w_up.shape[1] - a_up.shape[1] - g_up.shape[0]
    dl, il, gl = w_up.shape[1], a_up.shape[1], g_up.shape[0]
    lora_pad = 128
    rdims = (n_ctx, seq)

    def pad_cols(a, n):
        return jnp.pad(a, ((0, 0), (0, n - a.shape[1])))

    o_lo = fw + 3 * rw
    w_lo = jnp.concatenate([pad_cols(w_in[:, o_lo:o_lo + dl], lora_pad),
                            pad_cols(w_in[:, o_lo + dl:o_lo + dl + il], lora_pad),
                            w_in[:, o_lo + dl + il:]], axis=1).astype(BF16)
    mu = shift_mu.reshape(1, -1)
    mu_lo = jnp.concatenate([pad_cols(mu[:, 3 * rw:3 * rw + dl], lora_pad),
                             pad_cols(mu[:, 3 * rw + dl:3 * rw + dl + il], lora_pad),
                             mu[:, 3 * rw + dl + il:]], axis=1)
    w_up_p = jnp.pad(w_up, ((0, 0), (0, lora_pad - dl), (0, 0))).astype(BF16)
    a_up_p = jnp.pad(a_up, ((0, 0), (0, lora_pad - il), (0, 0))).astype(BF16)

    h = norm_mod(x, norm_g, mods, layer, 0, 1, rdims)
    f = matmul(h, w_in_b, BF16, rdims, cols=(0, fw))
    prkv = matmul(h, w_in_b, BF16, rdims, cols=(fw, 3 * rw))
    plo = matmul(h, w_lo, F32, rdims)

    r, v, kk, lw, kd, bd, g, bv = rwkv_terms(
        prkv, plo, mu[:, :3 * rw], mu_lo, w0, w_up_p, a0, a_up_p, g_up.astype(BF16),
        k_k.reshape(1, rw), k_a.reshape(1, rw), r_k.reshape(1, rw), dims, lora_pad)
    o2 = rwkv_scan(r, v, kk, lw, kd, bd, batch, dims)
    y_rw = rwkv_readout(o2, bv, g, lnx_g, lnx_b)

    cos_c, sin_c, cos_x, sin_x, cos_l, sin_l = fnet_tabs
    gc = fw // FNET_GROUPS
    y1, y2 = fnet_channel_dft(f, cos_c, sin_c)
    f_ctx = fnet_sequence_dft(y1, y2, 0, batch, ctx_len, cos_x, sin_x, gc)
    f_lat = fnet_sequence_dft(y1, y2, n_ctx, batch, seq, cos_l, sin_l, gc)
    f_mix = jnp.concatenate([f_ctx, f_lat], axis=0)
    w_out_b = w_out.astype(BF16)
    return matmul_gated_residual([f_mix, y_rw], [w_out_b[:fw], w_out_b[fw:]], x, mods, layer, 2, rdims)


def attention_layer(x, mods, layer, norm_g, w_qkv, w_o, sink, batch, dims, rope_tabs):
    n_ctx, ctx_len, seq = dims
    q_cols = w_o.shape[0]
    kv_cols = (w_qkv.shape[1] - q_cols) // 2
    cos_t, sin_t = rope_tabs
    h = norm_mod(x, norm_g, mods, layer, 0, 1, (n_ctx, seq))
    qkv = matmul(h, w_qkv.astype(BF16), BF16, (n_ctx, seq),
                 rope=(q_cols + kv_cols, cos_t, sin_t), tn_pref=cos_t.shape[1])
    att = attention(qkv, sink, batch, dims, q_cols, kv_cols)
    return matmul_gated_residual([att], [w_o.astype(BF16)], x, mods, layer, 2, (n_ctx, seq))


def kernel(x, c, ctx, c_ctx, ada_down, ada_up, ada_bias, norm1_g, norm2_g, final_g, mix_w_in, mix_w_out, shift_mu, decay_w0, decay_up, iclr_a0, iclr_up, gate_up, k_k, k_a, r_k, lnx_g, lnx_b, att_w_qkv, att_w_o, att_sink, router_w, router_b, exp_w_gate, exp_w_up, exp_w_down):
    batch, seq, d = x.shape
    ctx_len = ctx.shape[1]
    depth = ada_down.shape[0]
    n_ctx = batch * ctx_len
    assert batch + 1 <= MOD_ROWS
    dims = (n_ctx, ctx_len, seq)
    xs = jnp.concatenate([ctx.reshape(n_ctx, d), x.reshape(batch * seq, d)], axis=0)

    cvec = jnp.concatenate([c_ctx[None, :], c, jnp.zeros((MOD_ROWS - 1 - batch, d), F32)], axis=0)
    mods = adaln_all(cvec, ada_down, ada_up, ada_bias)
    mods = mods.reshape(depth, MOD_ROWS, N_MOD, d).transpose(0, 2, 1, 3)

    fw = mix_w_in.shape[2] - 3 * decay_w0.shape[2] - decay_up.shape[2] - iclr_up.shape[2] - gate_up.shape[1]
    gc = fw // FNET_GROUPS
    fnet_tabs = _dft_tables(gc) + _dft_tables(ctx_len) + _dft_tables(seq)
    kv_cols = (att_w_qkv.shape[2] - att_w_o.shape[1]) // 2
    rope_tabs = _rope_tables(seq, min(512, kv_cols))
    router_w_pad = jnp.pad(router_w, ((0, 0), (0, 128 - router_w.shape[1])))
    mix_w_in_b = mix_w_in.astype(BF16)

    for layer in range(depth):
        j = layer // 2
        if layer % 2 == 0:
            xs = fourier_rwkv_layer(xs, mods, layer, norm1_g[layer], mix_w_in[j], mix_w_in_b[j], mix_w_out[j], shift_mu[j],
                                    decay_w0[j], decay_up[j], iclr_a0[j], iclr_up[j], gate_up[j],
                                    k_k[j], k_a[j], r_k[j], lnx_g[j], lnx_b[j], batch, dims, fnet_tabs)
        else:
            xs = attention_layer(xs, mods, layer, norm1_g[layer], att_w_qkv[j], att_w_o[j], att_sink[j],
                                 batch, dims, rope_tabs)
        xs = moe_layer(xs, norm2_g[layer], mods, layer, router_w_pad, router_b,
                       exp_w_gate[layer].astype(BF16), exp_w_up[layer].astype(BF16),
                       exp_w_down[layer].astype(BF16), (n_ctx, seq))
    return final_norm(xs, final_g, n_ctx).reshape(batch, seq, d)
```

```python
import functools
import math

import jax
import jax.numpy as jnp
from jax import lax
from jax.experimental import pallas as pl
from jax.experimental.pallas import tpu as pltpu

F32 = jnp.float32
BF16 = jnp.bfloat16
HIGHEST = lax.Precision.HIGHEST

GRID_W = 64
NORM_EPS = 1e-6
GN_EPS = 64e-5
L2_EPS = 1e-12
N_MOD = 6
FNET_GROUPS = 4
HEAD = 64
ATT_GROUP = 8
ATT_SCALE = 0.125
WINDOW = 128
QBLOCK = 128
ROPE_THETA = 10000.0
N_EXPERT_GROUPS = 4
EXPERTS_PER_GROUP = 4
CHUNK = 64
MOD_ROWS = 8

VMEM_LIMIT = 56 * 1024 * 1024


def _cparams(n_axes):
    return pltpu.CompilerParams(dimension_semantics=("arbitrary",) * n_axes,
                                vmem_limit_bytes=VMEM_LIMIT)


def _tile(n, pref):
    t = min(n, pref)
    while n % t:
        t //= 2
    return t


def _row_id(r0, n_ctx, seq):
    return jnp.where(r0 < n_ctx, 0, 1 + (r0 - n_ctx) // seq)


def _dot(a, b):
    return jnp.dot(a, b, preferred_element_type=F32)


def _dot_hi(a, b):
    return jnp.dot(a, b, preferred_element_type=F32, precision=HIGHEST)


def _dot_t0(a, b):
    return lax.dot_general(a, b, (((0,), (0,)), ((), ())), preferred_element_type=F32)


def _dot_t1(a, b):
    return lax.dot_general(a, b, (((1,), (1,)), ((), ())), preferred_element_type=F32)


def _dot_hi_t1(a, b):
    return lax.dot_general(a, b, (((1,), (1,)), ((), ())), preferred_element_type=F32, precision=HIGHEST)


def _sigmoid(x):
    return 1.0 / (1.0 + jnp.exp(-x))


def _norm_mod(x, g, sh, sc):
    ms = jnp.mean(x * x, axis=-1, keepdims=True)
    y = x * lax.rsqrt(ms + NORM_EPS) * g
    return y * (1.0 + sc) + sh


def _adaln_kernel(cv_ref, down_ref, up_ref, bias_ref, o_ref, t_scr):
    @pl.when(pl.program_id(1) == 0)
    def _():
        cv = cv_ref[...]
        s = cv * _sigmoid(cv)
        t_scr[...] = _dot(s.astype(BF16), down_ref[...].astype(BF16)).astype(BF16)
    o_ref[...] = _dot(t_scr[...], up_ref[...].astype(BF16)) + bias_ref[...]


def adaln_all(cvec, ada_down, ada_up, ada_bias):
    depth, d, rank = ada_down.shape
    n = ada_up.shape[2]
    tn = _tile(n, 2048)
    bias = ada_bias.reshape(depth, 1, n)
    return pl.pallas_call(
        _adaln_kernel,
        grid=(depth, n // tn),
        in_specs=[pl.BlockSpec((MOD_ROWS, d), lambda l, j: (0, 0)),
                  pl.BlockSpec((None, d, rank), lambda l, j: (l, 0, 0)),
                  pl.BlockSpec((None, rank, tn), lambda l, j: (l, 0, j)),
                  pl.BlockSpec((None, 1, tn), lambda l, j: (l, 0, j))],
        out_specs=pl.BlockSpec((None, MOD_ROWS, tn), lambda l, j: (l, 0, j)),
        out_shape=jax.ShapeDtypeStruct((depth, MOD_ROWS, n), F32),
        scratch_shapes=[pltpu.VMEM((MOD_ROWS, rank), BF16)],
        compiler_params=_cparams(2), name="adaln",
    )(cvec, ada_down, ada_up, bias)


def _norm_mod_kernel(x_ref, g_ref, sh_ref, sc_ref, h_ref, *, tm, n_ctx, seq):
    rid = _row_id(pl.program_id(0) * tm, n_ctx, seq)
    h = _norm_mod(x_ref[...], g_ref[...], sh_ref[pl.ds(rid, 1), :], sc_ref[pl.ds(rid, 1), :])
    h_ref[...] = h.astype(h_ref.dtype)


def norm_mod(x, g, mods, layer, k_shift, k_scale, dims, tm_pref=512):
    t, d = x.shape
    n_ctx, seq = dims
    tm = _tile(math.gcd(n_ctx, seq), tm_pref)
    return pl.pallas_call(
        functools.partial(_norm_mod_kernel, tm=tm, n_ctx=n_ctx, seq=seq),
        grid=(t // tm,),
        in_specs=[pl.BlockSpec((tm, d), lambda i: (i, 0)),
                  pl.BlockSpec((1, d), lambda i: (0, 0)),
                  pl.BlockSpec((None, None, MOD_ROWS, d), lambda i: (layer, k_shift, 0, 0)),
                  pl.BlockSpec((None, None, MOD_ROWS, d), lambda i: (layer, k_scale, 0, 0))],
        out_specs=pl.BlockSpec((tm, d), lambda i: (i, 0)),
        out_shape=jax.ShapeDtypeStruct((t, d), BF16),
        compiler_params=_cparams(1), name="norm_mod",
    )(x, g.reshape(1, d), mods, mods)


def _matmul_kernel(*refs, tm, n_ctx, rope_cols):
    if rope_cols:
        a_ref, w_ref, cos_ref, sin_ref, o_ref = refs
    else:
        a_ref, w_ref, o_ref = refs
    r0 = pl.program_id(0) * tm
    j = pl.program_id(1)
    acc = _dot(a_ref[...], w_ref[...])
    if rope_cols:
        tn = acc.shape[1]
        do_rope = jnp.logical_and(r0 >= n_ctx, j * tn < rope_cols)

        @pl.when(do_rope)
        def _():
            lane = lax.broadcasted_iota(jnp.int32, acc.shape, 1)
            swapped = jnp.where(lane % 32 < 16, pltpu.roll(acc, tn - 16, 1), pltpu.roll(acc, 16, 1))
            o_ref[...] = (acc * cos_ref[...] + swapped * sin_ref[...]).astype(o_ref.dtype)

        @pl.when(jnp.logical_not(do_rope))
        def _():
            o_ref[...] = acc.astype(o_ref.dtype)
    else:
        o_ref[...] = acc.astype(o_ref.dtype)


def matmul(a, w, out_dtype, dims, rope=None, cols=None, w_index=None, tm_pref=1024, tn_pref=512):
    t, d = a.shape
    col0, n = cols if cols is not None else (0, w.shape[-1])
    n_ctx, seq = dims
    tm = _tile(math.gcd(n_ctx, seq), tm_pref)
    tn = _tile(math.gcd(n, col0) if col0 else n, tn_pref)
    jb = col0 // tn
    if w_index is None:
        w_spec = pl.BlockSpec((d, tn), lambda i, j: (0, jb + j))
    else:
        w_spec = pl.BlockSpec((None, d, tn), lambda i, j: (w_index, 0, jb + j))
    in_specs = [pl.BlockSpec((tm, d), lambda i, j: (i, 0)), w_spec]
    args = [a, w]
    rope_cols = 0
    if rope is not None:
        rope_cols, cos_t, sin_t = rope
        assert rope_cols % tn == 0 and cos_t.shape == (seq, tn)
        pos_map = lambda i, j: (jnp.maximum(i * tm - n_ctx, 0) % seq // tm, 0)
        in_specs += [pl.BlockSpec((tm, tn), pos_map), pl.BlockSpec((tm, tn), pos_map)]
        args += [cos_t, sin_t]
    return pl.pallas_call(
        functools.partial(_matmul_kernel, tm=tm, n_ctx=n_ctx, rope_cols=rope_cols),
        grid=(t // tm, n // tn),
        in_specs=in_specs,
        out_specs=pl.BlockSpec((tm, tn), lambda i, j: (i, j)),
        out_shape=jax.ShapeDtypeStruct((t, n), out_dtype),
        compiler_params=_cparams(2), name="matmul",
    )(*args)


def _mmres_kernel(*refs, n_a, tm, n_ctx, seq):
    a_refs = refs[:n_a]
    w_refs = refs[n_a:2 * n_a]
    x_ref, gate_ref, o_ref = refs[2 * n_a:]
    acc = _dot(a_refs[0][...], w_refs[0][...])
    for a_ref, w_ref in zip(a_refs[1:], w_refs[1:]):
        acc += _dot(a_ref[...], w_ref[...])
    rid = _row_id(pl.program_id(0) * tm, n_ctx, seq)
    o_ref[...] = x_ref[...] + gate_ref[pl.ds(rid, 1), :] * acc


def matmul_gated_residual(a_list, w_stack, w_index, x, mods, layer, k_gate, dims, tm_pref=1024, tn_pref=512):
    t, d = x.shape
    n_ctx, seq = dims
    tm = _tile(math.gcd(n_ctx, seq), tm_pref)
    tn = _tile(d, tn_pref)
    n_a = len(a_list)
    assert len({a.shape[1] for a in a_list}) == 1 and n_a * a_list[0].shape[1] == w_stack.shape[1]
    w_list = [w_stack] * n_a
    in_specs = [pl.BlockSpec((tm, a.shape[1]), lambda i, j: (i, 0)) for a in a_list]
    in_specs += [pl.BlockSpec((None, a.shape[1], tn), lambda i, j, p=p: (w_index, p, j)) for p, a in enumerate(a_list)]
    in_specs += [pl.BlockSpec((tm, tn), lambda i, j: (i, j)),
                 pl.BlockSpec((None, None, MOD_ROWS, tn), lambda i, j: (layer, k_gate, 0, j))]
    return pl.pallas_call(
        functools.partial(_mmres_kernel, n_a=n_a, tm=tm, n_ctx=n_ctx, seq=seq),
        grid=(t // tm, d // tn),
        in_specs=in_specs,
        out_specs=pl.BlockSpec((tm, tn), lambda i, j: (i, j)),
        out_shape=jax.ShapeDtypeStruct((t, d), F32),
        input_output_aliases={2 * n_a: 0},
        compiler_params=_cparams(2), name="matmul_gated_residual",
    )(*a_list, *w_list, x, mods)


def _final_norm_kernel(x_ref, g_ref, o_ref):
    x = x_ref[...]
    ms = jnp.mean(x * x, axis=-1, keepdims=True)
    o_ref[...] = x * lax.rsqrt(ms + NORM_EPS) * g_ref[...]


def final_norm(x, g, n_ctx, tm_pref=512):
    t, d = x.shape
    tm = _tile(math.gcd(n_ctx, t - n_ctx), tm_pref)
    off = n_ctx // tm
    return pl.pallas_call(
        _final_norm_kernel,
        grid=((t - n_ctx) // tm,),
        in_specs=[pl.BlockSpec((tm, d), lambda i: (i + off, 0)),
                  pl.BlockSpec((1, d), lambda i: (0, 0))],
        out_specs=pl.BlockSpec((tm, d), lambda i: (i, 0)),
        out_shape=jax.ShapeDtypeStruct((t - n_ctx, d), F32),
        compiler_params=_cparams(1), name="final_norm",
    )(x, g.reshape(1, d))


def _dft_tables(n, split=64):
    k = jnp.arange(n, dtype=jnp.int32)

    def cs(rows, period):
        ang = ((rows[:, None] * k[None, :]) % period).astype(F32) * (2.0 * math.pi / period)
        return jnp.cos(ang), jnp.sin(ang)

    if n % split or n <= split:
        c, s = cs(k, n)
        return c.astype(BF16), s.astype(BF16)
    ca, sa = cs(jnp.arange(n // split, dtype=jnp.int32), n // split)
    cb, sb = cs(jnp.arange(split, dtype=jnp.int32), n)
    ca, sa, cb, sb = ca[:, None, :], sa[:, None, :], cb[None, :, :], sb[None, :, :]
    return ((ca * cb - sa * sb).reshape(n, n).astype(BF16), (sa * cb + ca * sb).reshape(n, n).astype(BF16))


def _fnet_ch_kernel(x_ref, c_ref, s_ref, y1_ref, y2_ref):
    x = x_ref[...]
    y1_ref[...] = _dot(x, c_ref[...]).astype(y1_ref.dtype)
    y2_ref[...] = _dot(x, s_ref[...]).astype(y2_ref.dtype)


def fnet_channel_dft(f, cos_c, sin_c, tm_pref=1024):
    t, width = f.shape
    gc = width // FNET_GROUPS
    tm = _tile(t, tm_pref)
    spec = pl.BlockSpec((tm, gc), lambda i, g: (i, g))
    tab = pl.BlockSpec((gc, gc), lambda i, g: (0, 0))
    return pl.pallas_call(
        _fnet_ch_kernel,
        grid=(t // tm, FNET_GROUPS),
        in_specs=[spec, tab, tab],
        out_specs=[spec, spec],
        out_shape=[jax.ShapeDtypeStruct((t, width), BF16)] * 2,
        compiler_params=_cparams(2), name="fnet_channel_dft",
    )(f, cos_c, sin_c)


def _fnet_seq_kernel(c_ref, s_ref, y1_ref, y2_ref, o_ref, acc_ref, *, scale):
    k = pl.program_id(2)

    @pl.when(k == 0)
    def _():
        acc_ref[...] = jnp.zeros_like(acc_ref)

    acc_ref[...] += _dot(c_ref[...], y1_ref[...]) - _dot(s_ref[...], y2_ref[...])

    @pl.when(k == pl.num_programs(2) - 1)
    def _():
        o_ref[...] = (acc_ref[...] * scale).astype(o_ref.dtype)


def fnet_sequence_dft(y1, y2, row0, n_seq, length, cos_l, sin_l, gc, t_pref=512):
    width = y1.shape[1]
    tm = _tile(length, t_pref)
    tk = _tile(math.gcd(length, row0) if row0 else length, t_pref)
    nb = length // tm
    nk = length // tk
    scale = 1.0 / math.sqrt(length * gc)
    rb0 = row0 // tk
    y_spec = pl.BlockSpec((tk, width), lambda b, i, k: (rb0 + b * nk + k, 0))
    return pl.pallas_call(
        functools.partial(_fnet_seq_kernel, scale=scale),
        grid=(n_seq, nb, nk),
        in_specs=[pl.BlockSpec((tm, tk), lambda b, i, k: (i, k)),
                  pl.BlockSpec((tm, tk), lambda b, i, k: (i, k)),
                  y_spec, y_spec],
        out_specs=pl.BlockSpec((tm, width), lambda b, i, k: (b * nb + i, 0)),
        out_shape=jax.ShapeDtypeStruct((n_seq * length, width), BF16),
        scratch_shapes=[pltpu.VMEM((tm, width), F32)],
        compiler_params=_cparams(3), name="fnet_sequence_dft",
    )(cos_l, sin_l, y1, y2)


def _head_sum(x, ones_bd):
    hi = x.astype(BF16)
    lo = (x - hi.astype(F32)).astype(BF16)
    return _dot(hi, ones_bd) + _dot(lo, ones_bd)


def _shift(x, prev_row, next_row, mu, pos, length):
    rows = x.shape[0]
    ridx = lax.broadcasted_iota(jnp.int32, (rows, 1), 0)
    prev = jnp.where(ridx == 0, prev_row, pltpu.roll(x, 1, 0))
    nxt = jnp.where(ridx == rows - 1, next_row, pltpu.roll(x, rows - 1, 0))
    prev = jnp.where(pos == 0, 0.0, prev)
    nxt = jnp.where(pos == length - 1, 0.0, nxt)
    return x + mu * (0.5 * (prev + nxt) - x)


def _rwkv_terms_kernel(r_ref, k_ref, v_ref, rp_ref, kp_ref, vp_ref, rn_ref, kn_ref, vn_ref,
                       lo_ref, lop_ref, lon_ref, mur_ref, muk_ref, muv_ref, mulo_ref,
                       w0_ref, wup_ref, a0_ref, aup_ref, gup_ref, kk_ref, ka_ref, rk_ref, ones_ref,
                       r_o, v_o, kk_o, lw_o, kd_o, bd_o, g_o, bv_o, *, tm, n_ctx, ctx_len, seq, lora_pad):
    i = pl.program_id(0)
    r0 = i * tm
    ridx = lax.broadcasted_iota(jnp.int32, (tm, 1), 0) + r0
    in_ctx = r0 < n_ctx
    length = jnp.where(in_ctx, ctx_len, seq)
    pos = jnp.where(in_ctx, ridx % ctx_len, (ridx - n_ctx) % seq)

    def sh(ref, pref, nref, mu_ref):
        last = pref.shape[0] - 1
        return _shift(ref[...].astype(F32), pref[last:last + 1, :].astype(F32), nref[0:1, :].astype(F32),
                      mu_ref[...], pos, length)

    r = sh(r_ref, rp_ref, rn_ref, mur_ref)
    k = sh(k_ref, kp_ref, kn_ref, muk_ref)
    v = sh(v_ref, vp_ref, vn_ref, muv_ref)
    lo = sh(lo_ref, lop_ref, lon_ref, mulo_ref)
    w_in = jnp.tanh(lo[:, :lora_pad]).astype(BF16)
    a_in = lo[:, lora_pad:2 * lora_pad].astype(BF16)
    g_in = _sigmoid(lo[:, 2 * lora_pad:]).astype(BF16)
    ones_bd = ones_ref[...]

    kk = k * kk_ref[...]
    kk = kk * lax.rsqrt(_head_sum(kk * kk, ones_bd) + L2_EPS)
    ksum = jnp.zeros_like(k)
    for d in range(2):
        w_logit = w0_ref[d:d + 1, :] + _dot(w_in, wup_ref[d])
        lw_o[d] = -math.exp(-0.5) * _sigmoid(w_logit)
        a = _sigmoid(a0_ref[d:d + 1, :] + _dot(a_in, aup_ref[d]))
        k_d = k * (1.0 + (a - 1.0) * ka_ref[...])
        kd_o[d] = k_d.astype(kd_o.dtype)
        bd_o[d] = (kk * a).astype(bd_o.dtype)
        ksum = ksum + k_d
    g_o[...] = _dot(g_in, gup_ref[...]).astype(g_o.dtype)
    bv_o[...] = (_head_sum(r * ksum * rk_ref[...], ones_bd) * v).astype(bv_o.dtype)
    r_o[...] = r.astype(r_o.dtype)
    v_o[...] = v.astype(v_o.dtype)
    kk_o[...] = kk.astype(kk_o.dtype)


def rwkv_terms(prkv, plo, mu_rkv, mu_lo, w0, w_up, a0, a_up, g_up, k_k, k_a, r_k, dims, lora_pad, tm_pref=256):
    t = prkv.shape[0]
    rw = prkv.shape[1] // 3
    n_ctx, ctx_len, seq = dims
    tm = _tile(math.gcd(n_ctx, seq), tm_pref)
    tn = _tile(rw, 512)
    nj = rw // tn
    lo_w = plo.shape[1]
    hb = tm // 8
    last8 = t // 8 - 1
    hr = 8 * (4 // prkv.dtype.itemsize)
    hbr = tm // hr
    lastr = t // hr - 1

    def main(c):
        return pl.BlockSpec((tm, tn), lambda i, j: (i, c * nj + j))

    def prev(c):
        return pl.BlockSpec((hr, tn), lambda i, j: (jnp.maximum(i * hbr - 1, 0), c * nj + j))

    def nxt(c):
        return pl.BlockSpec((hr, tn), lambda i, j: (jnp.minimum((i + 1) * hbr, lastr), c * nj + j))

    def vec(c=0):
        return pl.BlockSpec((1, tn), lambda i, j: (0, c * nj + j))

    ones_bd = (jnp.arange(tn)[:, None] // HEAD == jnp.arange(tn)[None, :] // HEAD).astype(BF16)
    out_tok = pl.BlockSpec((tm, tn), lambda i, j: (i, j))
    out_dir = pl.BlockSpec((2, tm, tn), lambda i, j: (0, i, j))
    tok = jax.ShapeDtypeStruct((t, rw), BF16)
    tok2 = jax.ShapeDtypeStruct((2, t, rw), BF16)
    lw2 = jax.ShapeDtypeStruct((2, t, rw), F32)
    return pl.pallas_call(
        functools.partial(_rwkv_terms_kernel, tm=tm, n_ctx=n_ctx, ctx_len=ctx_len, seq=seq, lora_pad=lora_pad),
        grid=(t // tm, nj),
        in_specs=[main(0), main(1), main(2), prev(0), prev(1), prev(2), nxt(0), nxt(1), nxt(2),
                  pl.BlockSpec((tm, lo_w), lambda i, j: (i, 0)),
                  pl.BlockSpec((8, lo_w), lambda i, j: (jnp.maximum(i * hb - 1, 0), 0)),
                  pl.BlockSpec((8, lo_w), lambda i, j: (jnp.minimum((i + 1) * hb, last8), 0)),
                  vec(0), vec(1), vec(2),
                  pl.BlockSpec((1, lo_w), lambda i, j: (0, 0)),
                  pl.BlockSpec((2, tn), lambda i, j: (0, j)),
                  pl.BlockSpec((2, lora_pad, tn), lambda i, j: (0, 0, j)),
                  pl.BlockSpec((2, tn), lambda i, j: (0, j)),
                  pl.BlockSpec((2, lora_pad, tn), lambda i, j: (0, 0, j)),
                  pl.BlockSpec((g_up.shape[0], tn), lambda i, j: (0, j)),
                  vec(), vec(), vec(),
                  pl.BlockSpec((tn, tn), lambda i, j: (0, 0))],
        out_specs=[out_tok, out_tok, out_tok, out_dir, out_dir, out_dir, out_tok, out_tok],
        out_shape=[tok, tok, tok, lw2, tok2, tok2, tok, tok],
        compiler_params=_cparams(2), name="rwkv_terms",
    )(prkv, prkv, prkv, prkv, prkv, prkv, prkv, prkv, prkv, plo, plo, plo,
      mu_rkv, mu_rkv, mu_rkv, mu_lo, w0, w_up, a0, a_up, g_up, k_k, k_a, r_k, ones_bd)


def _rwkv_scan_kernel(r_ref, v_ref, kk_ref, lw_ref, kd_ref, bd_ref, o_ref,
                      s_scr, rt_s, at_s, kt_s, bt_s, bh_s, kh_s, vb_s, pd_s, *, n_heads, group):
    d = pl.program_id(1)
    step = pl.program_id(2)
    c = CHUNK

    @pl.when(step == 0)
    def _():
        s_scr[...] = jnp.zeros_like(s_scr)

    row = lax.broadcasted_iota(jnp.int32, (c, c), 0)
    col = lax.broadcasted_iota(jnp.int32, (c, c), 1)
    fwd = d == 0
    ahead = (row - col) * jnp.where(fwd, 1, -1)
    incl = ahead >= 0
    strict = ahead > 0
    eye = row == col
    eye_f = jnp.where(eye, 1.0, 0.0)

    tri = jnp.where(incl, 1.0, 0.0).astype(BF16)
    lw = lw_ref[...]
    lw_hi = lw.astype(BF16)
    lw_lo = (lw - lw_hi.astype(F32)).astype(BF16)
    l_inc = _dot(tri, lw_hi) + _dot(tri, lw_lo)
    l_tot = jnp.where(fwd, l_inc[c - 1:c, :], l_inc[0:1, :])
    e_neg = jnp.exp(-l_inc)
    e_rem = jnp.exp(l_tot - l_inc)
    kd = kd_ref[...]
    bd = bd_ref[...]
    rt_s[...] = (r_ref[...] * jnp.exp(l_inc)).astype(BF16)
    at_s[...] = (-kk_ref[...] * jnp.exp(l_inc - lw)).astype(BF16)
    kt_s[...] = (kd * e_neg).astype(BF16)
    bt_s[...] = (bd * e_neg).astype(BF16)
    bh_s[...] = (bd * e_rem).astype(BF16)
    kh_s[...] = (kd * e_rem).astype(BF16)
    vb_s[...] = v_ref[...].astype(BF16)
    pd_s[...] = jnp.exp(l_tot)

    gw = group * HEAD

    def group_body(gi, carry):
        sl = pl.ds(pl.multiple_of(gi * gw, gw), gw)
        rt_g, at_g, kt_g, bt_g, bh_g, kh_g, v_g = (s[:, sl] for s in (rt_s, at_s, kt_s, bt_s, bh_s, kh_s, vb_s))
        pd_g = pd_s[:, sl]
        heads = range(group)

        def hd(a, q):
            return a[:, q * HEAD:(q + 1) * HEAD]

        lhs = [jnp.concatenate([hd(at_g, q), hd(rt_g, q)], axis=0) for q in heads]
        gb = [_dot_t1(lhs[q], hd(bt_g, q)) for q in heads]
        gk = [_dot_t1(lhs[q], hd(kt_g, q)) for q in heads]
        a_ab = [jnp.where(strict, gb[q][:c], 0.0) for q in heads]
        a_rb = [jnp.where(incl, gb[q][c:], 0.0).astype(BF16) for q in heads]
        a_k = [jnp.concatenate([jnp.where(strict, gk[q][:c], 0.0), jnp.where(incl, gk[q][c:], 0.0)],
                               axis=0).astype(BF16) for q in heads]
        vv = [_dot(a_k[q], hd(v_g, q)) for q in heads]
        kv = [_dot_t0(hd(kh_g, q), hd(v_g, q)) for q in heads]
        x = [a.astype(BF16) for a in a_ab]
        tinv = [eye_f + a for a in a_ab]
        for _ in range(int(math.log2(c)) - 1):
            x = [_dot(x[q], x[q]).astype(BF16) for q in heads]
            tinv = [tinv[q] + _dot(tinv[q].astype(BF16), x[q]) for q in heads]
        tb = [t.astype(BF16) for t in tinv]
        wt = [_dot(tb[q], hd(at_g, q)).astype(BF16) for q in heads]
        u0 = [_dot(tb[q], vv[q][:c].astype(BF16)).astype(BF16) for q in heads]
        qm = [hd(rt_g, q).astype(F32) + _dot(a_rb[q], wt[q]) for q in heads]
        o0 = [_dot(a_rb[q], u0[q]) + vv[q][c:] for q in heads]
        m = [eye_f * hd(pd_g, q) + _dot_t0(hd(bh_g, q), wt[q]) for q in heads]
        nn = [_dot_t0(hd(bh_g, q), u0[q]) + kv[q] for q in heads]
        h0 = gi * group
        st = s_scr[pl.ds(h0, group)]
        res = [_dot(jnp.concatenate([qm[q], m[q]], axis=0).astype(BF16), st[q].astype(BF16)) for q in heads]
        s_scr[pl.ds(h0, group)] = jnp.stack([res[q][c:] + nn[q] for q in heads])
        o_ref[:, sl] = jnp.concatenate([res[q][:c] + o0[q] for q in heads], axis=1)
        return carry

    lax.fori_loop(0, n_heads // group, group_body, 0)


def rwkv_scan(r, v, kk, lw, kd, bd, batch, dims):
    t, rw = r.shape
    n_ctx, ctx_len, seq = dims
    c = CHUNK
    n_cc = ctx_len // c
    n_lc = seq // c
    steps = n_cc + n_lc

    def blk(b, d, s):
        ctx_i = jnp.where(d == 0, s, n_cc - 1 - s)
        lat_i = jnp.where(d == 0, s - n_cc, n_lc - 1 - (s - n_cc))
        return jnp.where(s < n_cc, b * n_cc + ctx_i, n_ctx // c + b * n_lc + lat_i)

    shared = pl.BlockSpec((c, rw), lambda b, d, s: (blk(b, d, s), 0))
    per_dir = pl.BlockSpec((None, c, rw), lambda b, d, s: (d, blk(b, d, s), 0))
    n_heads = rw // HEAD
    group = 16 if n_heads % 16 == 0 else 2
    prep = pltpu.VMEM((c, rw), BF16)
    return pl.pallas_call(
        functools.partial(_rwkv_scan_kernel, n_heads=n_heads, group=group),
        grid=(batch, 2, steps),
        in_specs=[shared, shared, shared, per_dir, per_dir, per_dir],
        out_specs=per_dir,
        out_shape=jax.ShapeDtypeStruct((2, t, rw), F32),
        scratch_shapes=[pltpu.VMEM((n_heads, HEAD, HEAD), F32)] + [prep] * 7 + [pltpu.VMEM((1, rw), F32)],
        compiler_params=_cparams(3), name="rwkv_scan",
    )(r, v, kk, lw, kd, bd)


def _rwkv_readout_kernel(o_ref, bv_ref, g_ref, lg_ref, lb_ref, ones_ref, y_ref):
    o = o_ref[0] + o_ref[1]
    ones_bd = ones_ref[...]
    mu = _head_sum(o, ones_bd) * (1.0 / HEAD)
    dev = o - mu
    var = _head_sum(dev * dev, ones_bd) * (1.0 / HEAD)
    on = dev * lax.rsqrt(var + GN_EPS) * lg_ref[...] + lb_ref[...]
    y_ref[...] = ((on + bv_ref[...]) * g_ref[...]).astype(y_ref.dtype)


def rwkv_readout(o2, bv, g, lnx_g, lnx_b, tm_pref=512):
    _, t, rw = o2.shape
    tm = _tile(t, tm_pref)
    tn = _tile(rw, 512)
    ones_bd = (jnp.arange(tn)[:, None] // HEAD == jnp.arange(tn)[None, :] // HEAD).astype(BF16)
    tok = pl.BlockSpec((tm, tn), lambda i, j: (i, j))
    vec = pl.BlockSpec((1, tn), lambda i, j: (0, j))
    return pl.pallas_call(
        _rwkv_readout_kernel,
        grid=(t // tm, rw // tn),
        in_specs=[pl.BlockSpec((2, tm, tn), lambda i, j: (0, i, j)), tok, tok, vec, vec,
                  pl.BlockSpec((tn, tn), lambda i, j: (0, 0))],
        out_specs=tok,
        out_shape=jax.ShapeDtypeStruct((t, rw), BF16),
        compiler_params=_cparams(2), name="rwkv_readout",
    )(o2, bv, g, lnx_g.reshape(1, rw), lnx_b.reshape(1, rw), ones_bd)


def _attend_group(q_heads, k_all, v_all, bias, sinks):
    nq = bias.shape[0]
    s = _dot_t1(jnp.concatenate(q_heads, axis=0), k_all)
    probs, denoms = [], []
    for g, sink in enumerate(sinks):
        sg = s[g * nq:(g + 1) * nq] + bias
        m = jnp.maximum(jnp.max(sg, axis=-1, keepdims=True), sink)
        p = jnp.exp(sg - m)
        denoms.append(jnp.sum(p, axis=-1, keepdims=True) + jnp.exp(sink - m))
        probs.append(p.astype(BF16))
    o = _dot(jnp.concatenate(probs, axis=0), v_all)
    return [o[g * nq:(g + 1) * nq] / denoms[g] for g in range(len(sinks))]


def _attn_kernel(sink_ref, q_ref, kp_ref, kc_ref, kn_ref, vp_ref, vc_ref, vn_ref, kx_ref, vx_ref, o_ref,
                 *, seq, n_ctx_keys, n_ctx_blocks):
    pair = pl.program_id(2)
    qi = pl.program_id(1) - n_ctx_blocks
    span = QBLOCK + 2 * WINDOW
    rr = lax.broadcasted_iota(jnp.int32, (QBLOCK, span + n_ctx_keys), 0)
    cc = lax.broadcasted_iota(jnp.int32, (QBLOCK, span + n_ctx_keys), 1)
    key_pos = qi * QBLOCK - WINDOW + cc
    off = cc - WINDOW - rr
    in_band = (jnp.abs(off) <= WINDOW) & (key_pos >= 0) & (key_pos < seq) & (qi >= 0)
    bias = jnp.where(jnp.logical_or(cc >= span, in_band), 0.0, -jnp.inf)
    q_all = q_ref[...] * jnp.asarray(ATT_SCALE, q_ref.dtype)
    outs = []
    for kv in range(2):
        hs = slice(kv * HEAD, (kv + 1) * HEAD)
        k_all = jnp.concatenate([kp_ref[:, hs], kc_ref[:, hs], kn_ref[:, hs], kx_ref[:, hs]], axis=0)
        v_all = jnp.concatenate([vp_ref[:, hs], vc_ref[:, hs], vn_ref[:, hs], vx_ref[:, hs]], axis=0)
        heads = [kv * ATT_GROUP + g for g in range(ATT_GROUP)]
        outs += _attend_group([q_all[:, hq * HEAD:(hq + 1) * HEAD] for hq in heads], k_all, v_all, bias,
                              [sink_ref[pair * 2 * ATT_GROUP + hq] for hq in heads])
    o_ref[...] = jnp.concatenate(outs, axis=1).astype(o_ref.dtype)


def attention(qkv, sink, batch, dims, q_cols, kv_cols):
    t = qkv.shape[0]
    n_ctx, ctx_len, seq = dims
    n_pairs = kv_cols // (2 * HEAD)
    qw = 2 * ATT_GROUP * HEAD
    kw = 2 * HEAD
    nqb = seq // QBLOCK
    ncb = ctx_len // QBLOCK
    lat0 = n_ctx // QBLOCK
    kcol = q_cols // kw
    vcol = (q_cols + kv_cols) // kw
    smem = pl.BlockSpec(memory_space=pltpu.SMEM)

    def q_map(b, i, p):
        return (jnp.where(i < ncb, b * ncb + i, lat0 + b * nqb + i - ncb), p)

    def band(col0, shift):
        def imap(b, i, p):
            return (lat0 + b * nqb + jnp.clip(i - ncb + shift, 0, nqb - 1), col0 + p)
        return pl.BlockSpec((QBLOCK, kw), imap)

    return pl.pallas_call(
        functools.partial(_attn_kernel, seq=seq, n_ctx_keys=ctx_len, n_ctx_blocks=ncb),
        grid=(batch, ncb + nqb, n_pairs),
        in_specs=[smem,
                  pl.BlockSpec((QBLOCK, qw), q_map),
                  band(kcol, -1), band(kcol, 0), band(kcol, 1),
                  band(vcol, -1), band(vcol, 0), band(vcol, 1),
                  pl.BlockSpec((ctx_len, kw), lambda b, i, p: (b, kcol + p)),
                  pl.BlockSpec((ctx_len, kw), lambda b, i, p: (b, vcol + p))],
        out_specs=pl.BlockSpec((QBLOCK, qw), q_map),
        out_shape=jax.ShapeDtypeStruct((t, q_cols), BF16),
        compiler_params=_cparams(3), name="attention",
    )(sink, qkv, qkv, qkv, qkv, qkv, qkv, qkv, qkv, qkv)


def _rope_tables(seq, width):
    pos = jnp.arange(seq, dtype=jnp.int32)
    row_pos = (pos // GRID_W).astype(F32)
    col_pos = (pos % GRID_W).astype(F32)
    half = HEAD // 2
    inv_freq = ROPE_THETA ** (-jnp.arange(0, half, 2, dtype=F32) / half)
    ang_r = row_pos[:, None] * inv_freq
    ang_c = col_pos[:, None] * inv_freq
    cos64 = jnp.concatenate([jnp.cos(ang_r), jnp.cos(ang_r), jnp.cos(ang_c), jnp.cos(ang_c)], axis=-1)
    sin64 = jnp.concatenate([-jnp.sin(ang_r), jnp.sin(ang_r), -jnp.sin(ang_c), jnp.sin(ang_c)], axis=-1)
    reps = width // HEAD
    return jnp.tile(cos64, (1, reps)), jnp.tile(sin64, (1, reps))


def _router_kernel(x_ref, g_ref, sh_ref, sc_ref, rw_ref, rb_ref, h_ref, ids_ref, gts_ref, *, tm, n_ctx, seq):
    rid = _row_id(pl.program_id(0) * tm, n_ctx, seq)
    h = _norm_mod(x_ref[...], g_ref[...], sh_ref[pl.ds(rid, 1), :], sc_ref[pl.ds(rid, 1), :])
    h_ref[...] = h
    logits = _dot_hi(h, rw_ref[...]).T[:rb_ref.shape[0]]
    e = jnp.exp(logits - jnp.max(logits, axis=0, keepdims=True))
    probs = e / jnp.sum(e, axis=0, keepdims=True)
    sel = probs + rb_ref[...]
    n_g, per = N_EXPERT_GROUPS, EXPERTS_PER_GROUP

    def row(a, r):
        return a[r:r + 1, :]

    scores = []
    for gi in range(n_g):
        a, b, c, d = (row(sel, gi * per + r) for r in range(per))
        hi1, lo1 = jnp.maximum(a, b), jnp.minimum(a, b)
        hi2, lo2 = jnp.maximum(c, d), jnp.minimum(c, d)
        scores.append(jnp.maximum(hi1, hi2) + jnp.maximum(jnp.minimum(hi1, hi2), jnp.maximum(lo1, lo2)))
    best = scores[0]
    bg = jnp.zeros_like(best, dtype=jnp.int32)
    for gi in range(1, n_g):
        better = scores[gi] > best
        best = jnp.where(better, scores[gi], best)
        bg = jnp.where(better, gi, bg)
    in_sel = []
    in_prob = []
    for r in range(per):
        s_r = row(sel, r)
        p_r = row(probs, r)
        for gi in range(1, n_g):
            s_r = jnp.where(bg == gi, row(sel, gi * per + r), s_r)
            p_r = jnp.where(bg == gi, row(probs, gi * per + r), p_r)
        in_sel.append(s_r)
        in_prob.append(p_r)

    def argmax_first(vals, exclude):
        bv = None
        for r in range(per):
            v = vals[r] if exclude is None else jnp.where(exclude == r, -jnp.inf, vals[r])
            if bv is None:
                bv, bi = v, jnp.zeros_like(bg)
            else:
                better = v > bv
                bv = jnp.where(better, v, bv)
                bi = jnp.where(better, r, bi)
        return bi

    i1 = argmax_first(in_sel, None)
    i2 = argmax_first(in_sel, i1)

    def pick(vals, idx):
        out = vals[0]
        for r in range(1, per):
            out = jnp.where(idx == r, vals[r], out)
        return out

    p1 = pick(in_prob, i1)
    p2 = pick(in_prob, i2)
    tot = p1 + p2
    ids_ref[...] = jnp.concatenate([bg * per + i1, bg * per + i2], axis=0)
    gts_ref[...] = jnp.concatenate([p1 / tot, p2 / tot], axis=0)


def router(x, g, mods, layer, router_w_pad, router_b, dims, tm_pref=256):
    t, d = x.shape
    n_ctx, seq = dims
    n_e = router_b.shape[0]
    lanes = router_w_pad.shape[1]
    tm = _tile(math.gcd(n_ctx, seq), tm_pref)
    return pl.pallas_call(
        functools.partial(_router_kernel, tm=tm, n_ctx=n_ctx, seq=seq),
        grid=(t // tm,),
        in_specs=[pl.BlockSpec((tm, d), lambda i: (i, 0)),
                  pl.BlockSpec((1, d), lambda i: (0, 0)),
                  pl.BlockSpec((None, None, MOD_ROWS, d), lambda i: (layer, 3, 0, 0)),
                  pl.BlockSpec((None, None, MOD_ROWS, d), lambda i: (layer, 4, 0, 0)),
                  pl.BlockSpec((d, lanes), lambda i: (0, 0)),
                  pl.BlockSpec((n_e, 1), lambda i: (0, 0))],
        out_specs=[pl.BlockSpec((tm, d), lambda i: (i, 0)),
                   pl.BlockSpec((2, tm), lambda i: (0, i)),
                   pl.BlockSpec((2, tm), lambda i: (0, i))],
        out_shape=[jax.ShapeDtypeStruct((t, d), F32),
                   jax.ShapeDtypeStruct((2, t), jnp.int32),
                   jax.ShapeDtypeStruct((2, t), F32)],
        compiler_params=_cparams(1), name="router",
    )(x, g.reshape(1, d), mods, mods, router_w_pad, router_b.reshape(n_e, 1))


def _gather(src_hbm, idx_ref, base, n_rows, dst, sem):
    def copy(r):
        return pltpu.make_async_copy(src_hbm.at[pl.ds(idx_ref[base + r], 1), :], dst.at[pl.ds(r, 1), :], sem)

    def start_row(r, c):
        copy(r).start()
        return c

    def wait_row(r, c):
        copy(r).wait()
        return c

    def start(unroll=8):
        lax.fori_loop(0, n_rows, start_row, 0, unroll=unroll)

    def wait():
        lax.fori_loop(0, n_rows, wait_row, 0, unroll=8)

    return start, wait


def _expert_kernel(te_ref, act_ref, tok_ref, h_hbm, wg_ref, wu_ref, wd_ref, y_ref, xbuf, sems, *, tm):
    i = pl.program_id(0)
    slot = i % 2

    def gather(tile, s):
        return _gather(h_hbm, tok_ref, tile * tm, tm, xbuf.at[s], sems.at[s])

    @pl.when(jnp.logical_and(i == 0, act_ref[0] == 1))
    def _():
        gather(0, 0)[0]()

    @pl.when(act_ref[i] == 1)
    def _():
        gather(i, slot)[1]()
        gather(i + 1, 1 - slot)[0](unroll=True)
        x = xbuf[slot].astype(BF16)
        a = _dot(x, wg_ref[...])
        hid = (a * _sigmoid(a)) * _dot(x, wu_ref[...])
        y_ref[...] = _dot(hid.astype(BF16), wd_ref[...])

    @pl.when(act_ref[i] == 0)
    def _():
        @pl.when(jnp.logical_and(i > 0, act_ref[jnp.maximum(i - 1, 0)] == 1))
        def _():
            gather(i, slot)[1]()
        y_ref[...] = jnp.zeros_like(y_ref)


def expert_ffn(h, tile_expert, tile_active, slot_token, w_gate, w_up, w_down, layer, tm):
    t, d = h.shape
    _, n_e, _, f = w_gate.shape
    n_tiles = tile_expert.shape[0]
    single = pl.Buffered(1)
    grid_spec = pltpu.PrefetchScalarGridSpec(
        num_scalar_prefetch=3,
        grid=(n_tiles,),
        in_specs=[pl.BlockSpec(memory_space=pl.ANY),
                  pl.BlockSpec((None, None, d, f), lambda i, te, act, tok: (layer, te[i], 0, 0), pipeline_mode=single),
                  pl.BlockSpec((None, None, d, f), lambda i, te, act, tok: (layer, te[i], 0, 0), pipeline_mode=single),
                  pl.BlockSpec((None, None, f, d), lambda i, te, act, tok: (layer, te[i], 0, 0), pipeline_mode=single)],
        out_specs=pl.BlockSpec((tm, d), lambda i, te, act, tok: (i, 0)),
        scratch_shapes=[pltpu.VMEM((2, tm, d), F32), pltpu.SemaphoreType.DMA((2,))],
    )
    return pl.pallas_call(
        functools.partial(_expert_kernel, tm=tm),
        grid_spec=grid_spec,
        out_shape=jax.ShapeDtypeStruct((n_tiles * tm, d), F32),
        compiler_params=_cparams(1), name="expert_ffn",
    )(tile_expert, tile_active, slot_token, h, w_gate, w_up, w_down)


def _combine_kernel(slots_ref, x_ref, gate_ref, rw_ref, y_hbm, o_ref, ybuf, sems, *, tm, n_tok, n_ctx, seq):
    i = pl.program_id(0)
    n = pl.num_programs(0)
    slot = i % 2

    def gathers(tile, s):
        return [_gather(y_hbm, slots_ref, k * n_tok + tile * tm, tm, ybuf.at[s, k], sems.at[s]) for k in range(2)]

    @pl.when(i == 0)
    def _():
        for start, _ in gathers(0, 0):
            start()

    @pl.when(i + 1 < n)
    def _():
        for start, _ in gathers(i + 1, 1 - slot):
            start()

    for _, wait in gathers(i, slot):
        wait()
    rid = _row_id(i * tm, n_ctx, seq)
    mixed = rw_ref[:, 0:1] * ybuf[slot, 0] + rw_ref[:, 1:2] * ybuf[slot, 1]
    o_ref[...] = x_ref[...] + gate_ref[pl.ds(rid, 1), :] * mixed


def moe_combine(x, y_sorted, tok_slots, route_w, mods, layer, dims, tm_pref=256):
    t, d = x.shape
    n_ctx, seq = dims
    tm = _tile(math.gcd(n_ctx, seq), tm_pref)
    grid_spec = pltpu.PrefetchScalarGridSpec(
        num_scalar_prefetch=1,
        grid=(t // tm,),
        in_specs=[pl.BlockSpec((tm, d), lambda i, s: (i, 0)),
                  pl.BlockSpec((None, None, MOD_ROWS, d), lambda i, s: (layer, 5, 0, 0)),
                  pl.BlockSpec((tm, 2), lambda i, s: (i, 0)),
                  pl.BlockSpec(memory_space=pl.ANY)],
        out_specs=pl.BlockSpec((tm, d), lambda i, s: (i, 0)),
        scratch_shapes=[pltpu.VMEM((2, 2, tm, d), F32), pltpu.SemaphoreType.DMA((2,))],
    )
    return pl.pallas_call(
        functools.partial(_combine_kernel, tm=tm, n_tok=t, n_ctx=n_ctx, seq=seq),
        grid_spec=grid_spec,
        out_shape=jax.ShapeDtypeStruct((t, d), F32),
        input_output_aliases={1: 0},
        compiler_params=_cparams(1), name="moe_combine",
    )(tok_slots, x, mods, route_w, y_sorted)


def _moe_plan(ids, n_e, tm):
    n_tok = ids.shape[1]
    e_flat = ids.reshape(-1)
    n_pairs = e_flat.shape[0]
    onehot = (e_flat[:, None] == jnp.arange(n_e, dtype=jnp.int32)[None, :]).astype(jnp.int32)
    rank = jnp.take_along_axis(jnp.cumsum(onehot, axis=0), e_flat[:, None], axis=1)[:, 0] - 1
    counts = jnp.sum(onehot, axis=0)
    padded = (counts + tm - 1) // tm * tm
    ends = jnp.cumsum(padded)
    starts = ends - padded
    dest = starts[e_flat] + rank
    n_tiles = n_pairs // tm + n_e + 1
    n_slots = n_tiles * tm
    pair_tok = jnp.arange(n_pairs, dtype=jnp.int32) % n_tok
    slot_token = jnp.zeros((n_slots,), jnp.int32).at[dest].set(pair_tok)
    tile_start = jnp.arange(n_tiles, dtype=jnp.int32) * tm
    tile_expert = jnp.minimum(jnp.searchsorted(ends, tile_start, side="right"), n_e - 1).astype(jnp.int32)
    tile_active = (tile_start < ends[-1]).astype(jnp.int32)
    last_expert = tile_expert[jnp.maximum(ends[-1] // tm - 1, 0)]
    tile_expert = jnp.where(tile_active == 1, tile_expert, last_expert)
    return tile_expert, tile_active, slot_token, dest.astype(jnp.int32)


def moe_layer(x, norm_g, mods, layer, router_w_pad, router_b, w_gate, w_up, w_down, dims, tm_pref=256):
    n_e = w_gate.shape[1]
    h, ids, gates = router(x, norm_g, mods, layer, router_w_pad, router_b, dims)
    tm = _tile(2 * x.shape[0], tm_pref)
    tile_expert, tile_active, slot_token, dest = _moe_plan(ids, n_e, tm)
    y_sorted = expert_ffn(h, tile_expert, tile_active, slot_token, w_gate, w_up, w_down, layer, tm)
    return moe_combine(x, y_sorted, dest, gates.T, mods, layer, dims)


def fourier_rwkv_layer(x, mods, layer, j, norm_g, w_in, w_in_b, w_out_b, shift_mu, w0, w_up, a0, a_up, g_up,
                       k_k, k_a, r_k, lnx_g, lnx_b, batch, dims, fnet_tabs):
    n_ctx, ctx_len, seq = dims
    rw = w0.shape[1]
    fw = w_in.shape[1] - 3 * rw - w_up.shape[1] - a_up.shape[1] - g_up.shape[0]
    dl, il, gl = w_up.shape[1], a_up.shape[1], g_up.shape[0]
    lora_pad = 128
    rdims = (n_ctx, seq)

    def pad_cols(a, n):
        return jnp.pad(a, ((0, 0), (0, n - a.shape[1])))

    o_lo = fw + 3 * rw
    w_lo = jnp.concatenate([pad_cols(w_in[:, o_lo:o_lo + dl], lora_pad),
                            pad_cols(w_in[:, o_lo + dl:o_lo + dl + il], lora_pad),
                            w_in[:, o_lo + dl + il:]], axis=1).astype(BF16)
    mu = shift_mu.reshape(1, -1)
    mu_lo = jnp.concatenate([pad_cols(mu[:, 3 * rw:3 * rw + dl], lora_pad),
                             pad_cols(mu[:, 3 * rw + dl:3 * rw + dl + il], lora_pad),
                             mu[:, 3 * rw + dl + il:]], axis=1)
    w_up_p = jnp.pad(w_up, ((0, 0), (0, lora_pad - dl), (0, 0))).astype(BF16)
    a_up_p = jnp.pad(a_up, ((0, 0), (0, lora_pad - il), (0, 0))).astype(BF16)

    h = norm_mod(x, norm_g, mods, layer, 0, 1, rdims)
    f = matmul(h, w_in_b, BF16, rdims, cols=(0, fw), w_index=j)
    prkv = matmul(h, w_in_b, BF16, rdims, cols=(fw, 3 * rw), w_index=j)
    plo = matmul(h, w_lo, F32, rdims)

    r, v, kk, lw, kd, bd, g, bv = rwkv_terms(
        prkv, plo, mu[:, :3 * rw], mu_lo, w0, w_up_p, a0, a_up_p, g_up.astype(BF16),
        k_k.reshape(1, rw), k_a.reshape(1, rw), r_k.reshape(1, rw), dims, lora_pad)
    o2 = rwkv_scan(r, v, kk, lw, kd, bd, batch, dims)
    y_rw = rwkv_readout(o2, bv, g, lnx_g, lnx_b)

    cos_c, sin_c, cos_x, sin_x, cos_l, sin_l = fnet_tabs
    gc = fw // FNET_GROUPS
    y1, y2 = fnet_channel_dft(f, cos_c, sin_c)
    f_ctx = fnet_sequence_dft(y1, y2, 0, batch, ctx_len, cos_x, sin_x, gc)
    f_lat = fnet_sequence_dft(y1, y2, n_ctx, batch, seq, cos_l, sin_l, gc)
    f_mix = jnp.concatenate([f_ctx, f_lat], axis=0)
    return matmul_gated_residual([f_mix, y_rw], w_out_b, j, x, mods, layer, 2, rdims)


def attention_layer(x, mods, layer, j, norm_g, w_qkv_b, w_o_b, sink, batch, dims, rope_tabs):
    n_ctx, ctx_len, seq = dims
    q_cols = w_o_b.shape[1]
    kv_cols = (w_qkv_b.shape[2] - q_cols) // 2
    cos_t, sin_t = rope_tabs
    h = norm_mod(x, norm_g, mods, layer, 0, 1, (n_ctx, seq))
    qkv = matmul(h, w_qkv_b, BF16, (n_ctx, seq), rope=(q_cols + kv_cols, cos_t, sin_t), w_index=j,
                 tn_pref=cos_t.shape[1])
    att = attention(qkv, sink, batch, dims, q_cols, kv_cols)
    return matmul_gated_residual([att], w_o_b, j, x, mods, layer, 2, (n_ctx, seq))


def kernel(x, c, ctx, c_ctx, ada_down, ada_up, ada_bias, norm1_g, norm2_g, final_g, mix_w_in, mix_w_out, shift_mu, decay_w0, decay_up, iclr_a0, iclr_up, gate_up, k_k, k_a, r_k, lnx_g, lnx_b, att_w_qkv, att_w_o, att_sink, router_w, router_b, exp_w_gate, exp_w_up, exp_w_down):
    batch, seq, d = x.shape
    ctx_len = ctx.shape[1]
    depth = ada_down.shape[0]
    n_ctx = batch * ctx_len
    assert batch + 1 <= MOD_ROWS
    dims = (n_ctx, ctx_len, seq)
    xs = jnp.concatenate([ctx.reshape(n_ctx, d), x.reshape(batch * seq, d)], axis=0)

    cvec = jnp.concatenate([c_ctx[None, :], c, jnp.zeros((MOD_ROWS - 1 - batch, d), F32)], axis=0)
    mods = adaln_all(cvec, ada_down, ada_up, ada_bias)
    mods = mods.reshape(depth, MOD_ROWS, N_MOD, d).transpose(0, 2, 1, 3)

    fw = mix_w_in.shape[2] - 3 * decay_w0.shape[2] - decay_up.shape[2] - iclr_up.shape[2] - gate_up.shape[1]
    gc = fw // FNET_GROUPS
    fnet_tabs = _dft_tables(gc) + _dft_tables(ctx_len) + _dft_tables(seq)
    kv_cols = (att_w_qkv.shape[2] - att_w_o.shape[1]) // 2
    rope_tabs = _rope_tables(seq, min(512, kv_cols))
    router_w_pad = jnp.pad(router_w, ((0, 0), (0, 128 - router_w.shape[1])))
    mix_w_in_b = mix_w_in.astype(BF16)
    mix_w_out_b = mix_w_out.astype(BF16)
    att_w_qkv_b = att_w_qkv.astype(BF16)
    att_w_o_b = att_w_o.astype(BF16)
    exp_w_gate_b = exp_w_gate.astype(BF16)
    exp_w_up_b = exp_w_up.astype(BF16)
    exp_w_down_b = exp_w_down.astype(BF16)

    for layer in range(depth):
        j = layer // 2
        if layer % 2 == 0:
            xs = fourier_rwkv_layer(xs, mods, layer, j, norm1_g[layer], mix_w_in[j], mix_w_in_b, mix_w_out_b, shift_mu[j],
                                    decay_w0[j], decay_up[j], iclr_a0[j], iclr_up[j], gate_up[j],
                                    k_k[j], k_a[j], r_k[j], lnx_g[j], lnx_b[j], batch, dims, fnet_tabs)
        else:
            xs = attention_layer(xs, mods, layer, j, norm1_g[layer], att_w_qkv_b, att_w_o_b, att_sink[j],
                                 batch, dims, rope_tabs)
        xs = moe_layer(xs, norm2_g[layer], mods, layer, router_w_pad, router_b,
                       exp_w_gate_b, exp_w_up_b, exp_w_down_b, (n_ctx, seq))
    return final_norm(xs, final_g, n_ctx).reshape(batch, seq, d)
```

```python
import functools
import math

import jax
import jax.numpy as jnp
from jax import lax
from jax.experimental import pallas as pl
from jax.experimental.pallas import tpu as pltpu

F32 = jnp.float32
BF16 = jnp.bfloat16
HIGHEST = lax.Precision.HIGHEST

GRID_W = 64
NORM_EPS = 1e-6
GN_EPS = 64e-5
L2_EPS = 1e-12
N_MOD = 6
FNET_GROUPS = 4
HEAD = 64
ATT_GROUP = 8
ATT_SCALE = 0.125
WINDOW = 128
QBLOCK = 128
ROPE_THETA = 10000.0
N_EXPERT_GROUPS = 4
EXPERTS_PER_GROUP = 4
CHUNK = 64
MOD_ROWS = 8

VMEM_LIMIT = 56 * 1024 * 1024


def _cparams(n_axes):
    return pltpu.CompilerParams(dimension_semantics=("arbitrary",) * n_axes,
                                vmem_limit_bytes=VMEM_LIMIT)


def _tile(n, pref):
    t = min(n, pref)
    while n % t:
        t //= 2
    return t


def _row_id(r0, n_ctx, seq):
    return jnp.where(r0 < n_ctx, 0, 1 + (r0 - n_ctx) // seq)


def _dot(a, b):
    return jnp.dot(a, b, preferred_element_type=F32)


def _dot_hi(a, b):
    return jnp.dot(a, b, preferred_element_type=F32, precision=HIGHEST)


def _dot_t0(a, b):
    return lax.dot_general(a, b, (((0,), (0,)), ((), ())), preferred_element_type=F32)


def _dot_t1(a, b):
    return lax.dot_general(a, b, (((1,), (1,)), ((), ())), preferred_element_type=F32)


def _dot_hi_t1(a, b):
    return lax.dot_general(a, b, (((1,), (1,)), ((), ())), preferred_element_type=F32, precision=HIGHEST)


def _sigmoid(x):
    return 1.0 / (1.0 + jnp.exp(-x))


def _norm_mod(x, g, sh, sc):
    ms = jnp.mean(x * x, axis=-1, keepdims=True)
    y = x * lax.rsqrt(ms + NORM_EPS) * g
    return y * (1.0 + sc) + sh


def _adaln_kernel(cv_ref, down_ref, up_ref, bias_ref, o_ref, t_scr):
    @pl.when(pl.program_id(1) == 0)
    def _():
        cv = cv_ref[...]
        s = cv * _sigmoid(cv)
        t_scr[...] = _dot(s.astype(BF16), down_ref[...].astype(BF16)).astype(BF16)
    o_ref[...] = _dot(t_scr[...], up_ref[...].astype(BF16)) + bias_ref[...]


def adaln_all(cvec, ada_down, ada_up, ada_bias):
    depth, d, rank = ada_down.shape
    n = ada_up.shape[2]
    tn = _tile(n, 2048)
    bias = ada_bias.reshape(depth, 1, n)
    return pl.pallas_call(
        _adaln_kernel,
        grid=(depth, n // tn),
        in_specs=[pl.BlockSpec((MOD_ROWS, d), lambda l, j: (0, 0)),
                  pl.BlockSpec((None, d, rank), lambda l, j: (l, 0, 0)),
                  pl.BlockSpec((None, rank, tn), lambda l, j: (l, 0, j)),
                  pl.BlockSpec((None, 1, tn), lambda l, j: (l, 0, j))],
        out_specs=pl.BlockSpec((None, MOD_ROWS, tn), lambda l, j: (l, 0, j)),
        out_shape=jax.ShapeDtypeStruct((depth, MOD_ROWS, n), F32),
        scratch_shapes=[pltpu.VMEM((MOD_ROWS, rank), BF16)],
        compiler_params=_cparams(2), name="adaln",
    )(cvec, ada_down, ada_up, bias)


def _norm_mod_kernel(x_ref, g_ref, sh_ref, sc_ref, h_ref, *, tm, n_ctx, seq):
    rid = _row_id(pl.program_id(0) * tm, n_ctx, seq)
    h = _norm_mod(x_ref[...], g_ref[...], sh_ref[pl.ds(rid, 1), :], sc_ref[pl.ds(rid, 1), :])
    h_ref[...] = h.astype(h_ref.dtype)


def norm_mod(x, g, mods, layer, k_shift, k_scale, dims, tm_pref=512):
    t, d = x.shape
    n_ctx, seq = dims
    tm = _tile(math.gcd(n_ctx, seq), tm_pref)
    return pl.pallas_call(
        functools.partial(_norm_mod_kernel, tm=tm, n_ctx=n_ctx, seq=seq),
        grid=(t // tm,),
        in_specs=[pl.BlockSpec((tm, d), lambda i: (i, 0)),
                  pl.BlockSpec((1, d), lambda i: (0, 0)),
                  pl.BlockSpec((None, None, MOD_ROWS, d), lambda i: (layer, k_shift, 0, 0)),
                  pl.BlockSpec((None, None, MOD_ROWS, d), lambda i: (layer, k_scale, 0, 0))],
        out_specs=pl.BlockSpec((tm, d), lambda i: (i, 0)),
        out_shape=jax.ShapeDtypeStruct((t, d), BF16),
        compiler_params=_cparams(1), name="norm_mod",
    )(x, g.reshape(1, d), mods, mods)


def _matmul_kernel(*refs, tm, n_ctx, rope_cols):
    if rope_cols:
        a_ref, w_ref, cos_ref, sin_ref, o_ref = refs
    else:
        a_ref, w_ref, o_ref = refs
    r0 = pl.program_id(0) * tm
    j = pl.program_id(1)
    acc = _dot(a_ref[...], w_ref[...])
    if rope_cols:
        tn = acc.shape[1]
        do_rope = jnp.logical_and(r0 >= n_ctx, j * tn < rope_cols)

        @pl.when(do_rope)
        def _():
            lane = lax.broadcasted_iota(jnp.int32, acc.shape, 1)
            swapped = jnp.where(lane % 32 < 16, pltpu.roll(acc, tn - 16, 1), pltpu.roll(acc, 16, 1))
            o_ref[...] = (acc * cos_ref[...] + swapped * sin_ref[...]).astype(o_ref.dtype)

        @pl.when(jnp.logical_not(do_rope))
        def _():
            o_ref[...] = acc.astype(o_ref.dtype)
    else:
        o_ref[...] = acc.astype(o_ref.dtype)


def matmul(a, w, out_dtype, dims, rope=None, cols=None, w_index=None, tm_pref=1024, tn_pref=512):
    t, d = a.shape
    col0, n = cols if cols is not None else (0, w.shape[-1])
    n_ctx, seq = dims
    tm = _tile(math.gcd(n_ctx, seq), tm_pref)
    tn = _tile(math.gcd(n, col0) if col0 else n, tn_pref)
    jb = col0 // tn
    if w_index is None:
        w_spec = pl.BlockSpec((d, tn), lambda i, j: (0, jb + j))
    else:
        w_spec = pl.BlockSpec((None, d, tn), lambda i, j: (w_index, 0, jb + j))
    in_specs = [pl.BlockSpec((tm, d), lambda i, j: (i, 0)), w_spec]
    args = [a, w]
    rope_cols = 0
    if rope is not None:
        rope_cols, cos_t, sin_t = rope
        assert rope_cols % tn == 0 and cos_t.shape == (seq, tn)
        pos_map = lambda i, j: (jnp.maximum(i * tm - n_ctx, 0) % seq // tm, 0)
        in_specs += [pl.BlockSpec((tm, tn), pos_map), pl.BlockSpec((tm, tn), pos_map)]
        args += [cos_t, sin_t]
    return pl.pallas_call(
        functools.partial(_matmul_kernel, tm=tm, n_ctx=n_ctx, rope_cols=rope_cols),
        grid=(t // tm, n // tn),
        in_specs=in_specs,
        out_specs=pl.BlockSpec((tm, tn), lambda i, j: (i, j)),
        out_shape=jax.ShapeDtypeStruct((t, n), out_dtype),
        compiler_params=_cparams(2), name="matmul",
    )(*args)


def _mmres_kernel(*refs, n_a, tm, n_ctx, seq):
    a_refs = refs[:n_a]
    w_refs = refs[n_a:2 * n_a]
    x_ref, gate_ref, o_ref = refs[2 * n_a:]
    acc = _dot(a_refs[0][...], w_refs[0][...])
    for a_ref, w_ref in zip(a_refs[1:], w_refs[1:]):
        acc += _dot(a_ref[...], w_ref[...])
    rid = _row_id(pl.program_id(0) * tm, n_ctx, seq)
    o_ref[...] = x_ref[...] + gate_ref[pl.ds(rid, 1), :] * acc


def matmul_gated_residual(a_list, w_stack, w_index, x, mods, layer, k_gate, dims, tm_pref=1024, tn_pref=512):
    t, d = x.shape
    n_ctx, seq = dims
    tm = _tile(math.gcd(n_ctx, seq), tm_pref)
    tn = _tile(d, tn_pref)
    n_a = len(a_list)
    assert len({a.shape[1] for a in a_list}) == 1 and n_a * a_list[0].shape[1] == w_stack.shape[1]
    w_list = [w_stack] * n_a
    in_specs = [pl.BlockSpec((tm, a.shape[1]), lambda i, j: (i, 0)) for a in a_list]
    in_specs += [pl.BlockSpec((None, a.shape[1], tn), lambda i, j, p=p: (w_index, p, j)) for p, a in enumerate(a_list)]
    in_specs += [pl.BlockSpec((tm, tn), lambda i, j: (i, j)),
                 pl.BlockSpec((None, None, MOD_ROWS, tn), lambda i, j: (layer, k_gate, 0, j))]
    return pl.pallas_call(
        functools.partial(_mmres_kernel, n_a=n_a, tm=tm, n_ctx=n_ctx, seq=seq),
        grid=(t // tm, d // tn),
        in_specs=in_specs,
        out_specs=pl.BlockSpec((tm, tn), lambda i, j: (i, j)),
        out_shape=jax.ShapeDtypeStruct((t, d), F32),
        input_output_aliases={2 * n_a: 0},
        compiler_params=_cparams(2), name="matmul_gated_residual",
    )(*a_list, *w_list, x, mods)


def _final_norm_kernel(x_ref, g_ref, o_ref):
    x = x_ref[...]
    ms = jnp.mean(x * x, axis=-1, keepdims=True)
    o_ref[...] = x * lax.rsqrt(ms + NORM_EPS) * g_ref[...]


def final_norm(x, g, n_ctx, tm_pref=512):
    t, d = x.shape
    tm = _tile(math.gcd(n_ctx, t - n_ctx), tm_pref)
    off = n_ctx // tm
    return pl.pallas_call(
        _final_norm_kernel,
        grid=((t - n_ctx) // tm,),
        in_specs=[pl.BlockSpec((tm, d), lambda i: (i + off, 0)),
                  pl.BlockSpec((1, d), lambda i: (0, 0))],
        out_specs=pl.BlockSpec((tm, d), lambda i: (i, 0)),
        out_shape=jax.ShapeDtypeStruct((t - n_ctx, d), F32),
        compiler_params=_cparams(1), name="final_norm",
    )(x, g.reshape(1, d))


def _dft_tables(n, split=64):
    k = jnp.arange(n, dtype=jnp.int32)

    def cs(rows, period):
        ang = ((rows[:, None] * k[None, :]) % period).astype(F32) * (2.0 * math.pi / period)
        return jnp.cos(ang), jnp.sin(ang)

    if n % split or n <= split:
        c, s = cs(k, n)
        return c.astype(BF16), s.astype(BF16)
    ca, sa = cs(jnp.arange(n // split, dtype=jnp.int32), n // split)
    cb, sb = cs(jnp.arange(split, dtype=jnp.int32), n)
    ca, sa, cb, sb = ca[:, None, :], sa[:, None, :], cb[None, :, :], sb[None, :, :]
    return ((ca * cb - sa * sb).reshape(n, n).astype(BF16), (sa * cb + ca * sb).reshape(n, n).astype(BF16))


def _fnet_ch_kernel(x_ref, c_ref, s_ref, y1_ref, y2_ref):
    x = x_ref[...]
    y1_ref[...] = _dot(x, c_ref[...]).astype(y1_ref.dtype)
    y2_ref[...] = _dot(x, s_ref[...]).astype(y2_ref.dtype)


def fnet_channel_dft(f, cos_c, sin_c, row0, n_rows, tm_pref=1024):
    width = f.shape[1]
    gc = width // FNET_GROUPS
    tm = _tile(math.gcd(row0, n_rows) if row0 else n_rows, tm_pref)
    ib = row0 // tm
    tab = pl.BlockSpec((gc, gc), lambda i, g: (0, 0))
    out = pl.BlockSpec((tm, gc), lambda i, g: (i, g))
    return pl.pallas_call(
        _fnet_ch_kernel,
        grid=(n_rows // tm, FNET_GROUPS),
        in_specs=[pl.BlockSpec((tm, gc), lambda i, g: (ib + i, g)), tab, tab],
        out_specs=[out, out],
        out_shape=[jax.ShapeDtypeStruct((n_rows, width), BF16)] * 2,
        compiler_params=_cparams(2), name="fnet_channel_dft",
    )(f, cos_c, sin_c)


def _fnet_fft_a_kernel(y1_ref, y2_ref, m1_ref, m2_ref, ct_ref, st_ref, br_ref, bi_ref):
    n1 = y1_ref.shape[0]
    a = _dot(m1_ref[...], y1_ref[...]) + _dot(m2_ref[...], y2_ref[...])
    ar, ai = a[:n1], a[n1:]
    ct, st = ct_ref[...], st_ref[...]
    br_ref[...] = (ar * ct + ai * st).astype(br_ref.dtype)
    bi_ref[...] = (ai * ct - ar * st).astype(bi_ref.dtype)


def _fnet_fft_b_kernel(br_ref, bi_ref, c_ref, s_ref, o_ref, *, scale):
    o_ref[...] = ((_dot(c_ref[...], br_ref[...]) + _dot(s_ref[...], bi_ref[...])) * scale).astype(o_ref.dtype)


def fnet_sequence_fft(y1, y2, n_seq, length, gc, n2=64):
    width = y1.shape[1]
    n1 = length // n2

    def cs(rows, cols, period):
        ang = ((rows[:, None] * cols[None, :]) % period).astype(F32) * (2.0 * math.pi / period)
        return jnp.cos(ang), jnp.sin(ang)

    i1 = jnp.arange(n1, dtype=jnp.int32)
    i2 = jnp.arange(n2, dtype=jnp.int32)
    c1, s1 = cs(i1, i1, n1)
    c2, s2 = cs(i2, i2, n2)
    ct, st = cs(i2, i1, length)
    m1 = jnp.concatenate([c1, -s1], axis=0).astype(BF16)
    m2 = jnp.concatenate([-s1, -c1], axis=0).astype(BF16)
    y_spec = pl.BlockSpec((n1, width), lambda s, b: (s, b))
    m_spec = pl.BlockSpec((2 * n1, n1), lambda s, b: (0, 0))
    t_spec = pl.BlockSpec((None, n1, 1), lambda s, b: (b, 0, 0))
    b_spec = pl.BlockSpec((None, n1, width), lambda s, b: (s * n2 + b, 0, 0))
    b_shape = jax.ShapeDtypeStruct((n_seq * n2, n1, width), BF16)
    br, bi = pl.pallas_call(
        _fnet_fft_a_kernel,
        grid=(n_seq, n2),
        in_specs=[y_spec, y_spec, m_spec, m_spec, t_spec, t_spec],
        out_specs=[b_spec, b_spec],
        out_shape=[b_shape, b_shape],
        compiler_params=_cparams(2), name="fnet_fft_a",
    )(y1.reshape(n_seq * n1, n2 * width), y2.reshape(n_seq * n1, n2 * width), m1, m2,
      ct.reshape(n2, n1, 1), st.reshape(n2, n1, 1))
    tc = _tile(n1 * width, 4096)
    v_spec = pl.BlockSpec((n2, tc), lambda s, j: (s, j))
    w_spec = pl.BlockSpec((n2, n2), lambda s, j: (0, 0))
    out = pl.pallas_call(
        functools.partial(_fnet_fft_b_kernel, scale=1.0 / math.sqrt(length * gc)),
        grid=(n_seq, n1 * width // tc),
        in_specs=[v_spec, v_spec, w_spec, w_spec],
        out_specs=v_spec,
        out_shape=jax.ShapeDtypeStruct((n_seq * n2, n1 * width), BF16),
        compiler_params=_cparams(2), name="fnet_fft_b",
    )(br.reshape(n_seq * n2, n1 * width), bi.reshape(n_seq * n2, n1 * width), c2.astype(BF16), s2.astype(BF16))
    return out.reshape(n_seq * length, width)


def _fnet_seq_kernel(c_ref, s_ref, y1_ref, y2_ref, o_ref, acc_ref, *, scale):
    k = pl.program_id(2)

    @pl.when(k == 0)
    def _():
        acc_ref[...] = jnp.zeros_like(acc_ref)

    acc_ref[...] += _dot(c_ref[...], y1_ref[...]) - _dot(s_ref[...], y2_ref[...])

    @pl.when(k == pl.num_programs(2) - 1)
    def _():
        o_ref[...] = (acc_ref[...] * scale).astype(o_ref.dtype)


def fnet_sequence_dft(y1, y2, row0, n_seq, length, cos_l, sin_l, gc, t_pref=512):
    width = y1.shape[1]
    tm = _tile(length, t_pref)
    tk = _tile(math.gcd(length, row0) if row0 else length, t_pref)
    nb = length // tm
    nk = length // tk
    scale = 1.0 / math.sqrt(length * gc)
    rb0 = row0 // tk
    y_spec = pl.BlockSpec((tk, width), lambda b, i, k: (rb0 + b * nk + k, 0))
    return pl.pallas_call(
        functools.partial(_fnet_seq_kernel, scale=scale),
        grid=(n_seq, nb, nk),
        in_specs=[pl.BlockSpec((tm, tk), lambda b, i, k: (i, k)),
                  pl.BlockSpec((tm, tk), lambda b, i, k: (i, k)),
                  y_spec, y_spec],
        out_specs=pl.BlockSpec((tm, width), lambda b, i, k: (b * nb + i, 0)),
        out_shape=jax.ShapeDtypeStruct((n_seq * length, width), BF16),
        scratch_shapes=[pltpu.VMEM((tm, width), F32)],
        compiler_params=_cparams(3), name="fnet_sequence_dft",
    )(cos_l, sin_l, y1, y2)


def _head_sum(x, ones_bd):
    hi = x.astype(BF16)
    lo = (x - hi.astype(F32)).astype(BF16)
    return _dot(hi, ones_bd) + _dot(lo, ones_bd)


def _shift(x, prev_row, next_row, mu, pos, length):
    rows = x.shape[0]
    ridx = lax.broadcasted_iota(jnp.int32, (rows, 1), 0)
    prev = jnp.where(ridx == 0, prev_row, pltpu.roll(x, 1, 0))
    nxt = jnp.where(ridx == rows - 1, next_row, pltpu.roll(x, rows - 1, 0))
    prev = jnp.where(pos == 0, 0.0, prev)
    nxt = jnp.where(pos == length - 1, 0.0, nxt)
    return x + mu * (0.5 * (prev + nxt) - x)


def _rwkv_terms_kernel(r_ref, k_ref, v_ref, rp_ref, kp_ref, vp_ref, rn_ref, kn_ref, vn_ref,
                       lo_ref, lop_ref, lon_ref, mur_ref, muk_ref, muv_ref, mulo_ref,
                       w0_ref, wup_ref, a0_ref, aup_ref, gup_ref, kk_ref, ka_ref, rk_ref, ones_ref,
                       r_o, v_o, kk_o, lw_o, kd_o, bd_o, g_o, bv_o, *, tm, n_ctx, ctx_len, seq, lora_pad):
    i = pl.program_id(0)
    r0 = i * tm
    ridx = lax.broadcasted_iota(jnp.int32, (tm, 1), 0) + r0
    in_ctx = r0 < n_ctx
    length = jnp.where(in_ctx, ctx_len, seq)
    pos = jnp.where(in_ctx, ridx % ctx_len, (ridx - n_ctx) % seq)

    def sh(ref, pref, nref, mu_ref):
        last = pref.shape[0] - 1
        return _shift(ref[...].astype(F32), pref[last:last + 1, :].astype(F32), nref[0:1, :].astype(F32),
                      mu_ref[...], pos, length)

    r = sh(r_ref, rp_ref, rn_ref, mur_ref)
    k = sh(k_ref, kp_ref, kn_ref, muk_ref)
    v = sh(v_ref, vp_ref, vn_ref, muv_ref)
    lo = sh(lo_ref, lop_ref, lon_ref, mulo_ref)
    w_in = jnp.tanh(lo[:, :lora_pad]).astype(BF16)
    a_in = lo[:, lora_pad:2 * lora_pad].astype(BF16)
    g_in = _sigmoid(lo[:, 2 * lora_pad:]).astype(BF16)
    ones_bd = ones_ref[...]

    kk = k * kk_ref[...]
    kk = kk * lax.rsqrt(_head_sum(kk * kk, ones_bd) + L2_EPS)
    ksum = jnp.zeros_like(k)
    for d in range(2):
        w_logit = w0_ref[d:d + 1, :] + _dot(w_in, wup_ref[d])
        lw_o[d] = -math.exp(-0.5) * _sigmoid(w_logit)
        a = _sigmoid(a0_ref[d:d + 1, :] + _dot(a_in, aup_ref[d]))
        k_d = k * (1.0 + (a - 1.0) * ka_ref[...])
        kd_o[d] = k_d.astype(kd_o.dtype)
        bd_o[d] = (kk * a).astype(bd_o.dtype)
        ksum = ksum + k_d
    g_o[...] = _dot(g_in, gup_ref[...]).astype(g_o.dtype)
    bv_o[...] = (_head_sum(r * ksum * rk_ref[...], ones_bd) * v).astype(bv_o.dtype)
    r_o[...] = r.astype(r_o.dtype)
    v_o[...] = v.astype(v_o.dtype)
    kk_o[...] = kk.astype(kk_o.dtype)


def rwkv_terms(prkv, plo, mu_rkv, mu_lo, w0, w_up, a0, a_up, g_up, k_k, k_a, r_k, dims, lora_pad, tm_pref=256):
    t = prkv.shape[0]
    rw = prkv.shape[1] // 3
    n_ctx, ctx_len, seq = dims
    tm = _tile(math.gcd(n_ctx, seq), tm_pref)
    tn = _tile(rw, 512)
    nj = rw // tn
    lo_w = plo.shape[1]
    hb = tm // 8
    last8 = t // 8 - 1
    hr = 8 * (4 // prkv.dtype.itemsize)
    hbr = tm // hr
    lastr = t // hr - 1

    def main(c):
        return pl.BlockSpec((tm, tn), lambda i, j: (i, c * nj + j))

    def prev(c):
        return pl.BlockSpec((hr, tn), lambda i, j: (jnp.maximum(i * hbr - 1, 0), c * nj + j))

    def nxt(c):
        return pl.BlockSpec((hr, tn), lambda i, j: (jnp.minimum((i + 1) * hbr, lastr), c * nj + j))

    def vec(c=0):
        return pl.BlockSpec((1, tn), lambda i, j: (0, c * nj + j))

    ones_bd = (jnp.arange(tn)[:, None] // HEAD == jnp.arange(tn)[None, :] // HEAD).astype(BF16)
    out_tok = pl.BlockSpec((tm, tn), lambda i, j: (i, j))
    out_dir = pl.BlockSpec((2, tm, tn), lambda i, j: (0, i, j))
    tok = jax.ShapeDtypeStruct((t, rw), BF16)
    tok2 = jax.ShapeDtypeStruct((2, t, rw), BF16)
    lw2 = jax.ShapeDtypeStruct((2, t, rw), F32)
    return pl.pallas_call(
        functools.partial(_rwkv_terms_kernel, tm=tm, n_ctx=n_ctx, ctx_len=ctx_len, seq=seq, lora_pad=lora_pad),
        grid=(t // tm, nj),
        in_specs=[main(0), main(1), main(2), prev(0), prev(1), prev(2), nxt(0), nxt(1), nxt(2),
                  pl.BlockSpec((tm, lo_w), lambda i, j: (i, 0)),
                  pl.BlockSpec((8, lo_w), lambda i, j: (jnp.maximum(i * hb - 1, 0), 0)),
                  pl.BlockSpec((8, lo_w), lambda i, j: (jnp.minimum((i + 1) * hb, last8), 0)),
                  vec(0), vec(1), vec(2),
                  pl.BlockSpec((1, lo_w), lambda i, j: (0, 0)),
                  pl.BlockSpec((2, tn), lambda i, j: (0, j)),
                  pl.BlockSpec((2, lora_pad, tn), lambda i, j: (0, 0, j)),
                  pl.BlockSpec((2, tn), lambda i, j: (0, j)),
                  pl.BlockSpec((2, lora_pad, tn), lambda i, j: (0, 0, j)),
                  pl.BlockSpec((g_up.shape[0], tn), lambda i, j: (0, j)),
                  vec(), vec(), vec(),
                  pl.BlockSpec((tn, tn), lambda i, j: (0, 0))],
        out_specs=[out_tok, out_tok, out_tok, out_dir, out_dir, out_dir, out_tok, out_tok],
        out_shape=[tok, tok, tok, lw2, tok2, tok2, tok, tok],
        compiler_params=_cparams(2), name="rwkv_terms",
    )(prkv, prkv, prkv, prkv, prkv, prkv, prkv, prkv, prkv, plo, plo, plo,
      mu_rkv, mu_rkv, mu_rkv, mu_lo, w0, w_up, a0, a_up, g_up, k_k, k_a, r_k, ones_bd)


def _rwkv_scan_kernel(r_ref, v_ref, kk_ref, lw_ref, kd_ref, bd_ref, o_ref,
                      s_scr, rt_s, at_s, kt_s, bt_s, bh_s, kh_s, vb_s, pd_s, *, n_heads, group):
    d = pl.program_id(1)
    step = pl.program_id(2)
    c = CHUNK

    @pl.when(step == 0)
    def _():
        s_scr[...] = jnp.zeros_like(s_scr)

    row = lax.broadcasted_iota(jnp.int32, (c, c), 0)
    col = lax.broadcasted_iota(jnp.int32, (c, c), 1)
    fwd = d == 0
    ahead = (row - col) * jnp.where(fwd, 1, -1)
    incl = ahead >= 0
    strict = ahead > 0
    eye = row == col
    eye_f = jnp.where(eye, 1.0, 0.0)

    tri = jnp.where(incl, 1.0, 0.0).astype(BF16)
    lw = lw_ref[...]
    lw_hi = lw.astype(BF16)
    lw_lo = (lw - lw_hi.astype(F32)).astype(BF16)
    l_inc = _dot(tri, lw_hi) + _dot(tri, lw_lo)
    l_tot = jnp.where(fwd, l_inc[c - 1:c, :], l_inc[0:1, :])
    e_neg = jnp.exp(-l_inc)
    e_rem = jnp.exp(l_tot - l_inc)
    kd = kd_ref[...]
    bd = bd_ref[...]
    rt_s[...] = (r_ref[...] * jnp.exp(l_inc)).astype(BF16)
    at_s[...] = (-kk_ref[...] * jnp.exp(l_inc - lw)).astype(BF16)
    kt_s[...] = (kd * e_neg).astype(BF16)
    bt_s[...] = (bd * e_neg).astype(BF16)
    bh_s[...] = (bd * e_rem).astype(BF16)
    kh_s[...] = (kd * e_rem).astype(BF16)
    vb_s[...] = v_ref[...].astype(BF16)
    pd_s[...] = jnp.exp(l_tot)

    gw = group * HEAD

    def group_body(gi, carry):
        sl = pl.ds(pl.multiple_of(gi * gw, gw), gw)
        rt_g, at_g, kt_g, bt_g, bh_g, kh_g, v_g = (s[:, sl] for s in (rt_s, at_s, kt_s, bt_s, bh_s, kh_s, vb_s))
        pd_g = pd_s[:, sl]
        heads = range(group)

        def hd(a, q):
            return a[:, q * HEAD:(q + 1) * HEAD]

        lhs = [jnp.concatenate([hd(at_g, q), hd(rt_g, q)], axis=0) for q in heads]
        gb = [_dot_t1(lhs[q], hd(bt_g, q)) for q in heads]
        gk = [_dot_t1(lhs[q], hd(kt_g, q)) for q in heads]
        a_ab = [jnp.where(strict, gb[q][:c], 0.0) for q in heads]
        a_rb = [jnp.where(incl, gb[q][c:], 0.0).astype(BF16) for q in heads]
        a_k = [jnp.concatenate([jnp.where(strict, gk[q][:c], 0.0), jnp.where(incl, gk[q][c:], 0.0)],
                               axis=0).astype(BF16) for q in heads]
        vv = [_dot(a_k[q], hd(v_g, q)) for q in heads]
        kv = [_dot_t0(hd(kh_g, q), hd(v_g, q)) for q in heads]
        tinv = [eye_f + a for a in a_ab]
        x = [_dot(a.astype(BF16), a.astype(BF16)).astype(BF16) for a in a_ab]
        n_sq = int(math.log2(c)) - 1
        for it in range(n_sq):
            if it < n_sq - 1:
                both = [_dot(jnp.concatenate([tinv[q].astype(BF16), x[q]], axis=0), x[q]) for q in heads]
                tinv = [tinv[q] + both[q][:c] for q in heads]
                x = [both[q][c:].astype(BF16) for q in heads]
            else:
                tinv = [tinv[q] + _dot(tinv[q].astype(BF16), x[q]) for q in heads]
        tb = [t.astype(BF16) for t in tinv]
        wt = [_dot(tb[q], hd(at_g, q)).astype(BF16) for q in heads]
        u0 = [_dot(tb[q], vv[q][:c].astype(BF16)).astype(BF16) for q in heads]
        qm = [hd(rt_g, q).astype(F32) + _dot(a_rb[q], wt[q]) for q in heads]
        o0 = [_dot(a_rb[q], u0[q]) + vv[q][c:] for q in heads]
        m = [eye_f * hd(pd_g, q) + _dot_t0(hd(bh_g, q), wt[q]) for q in heads]
        nn = [_dot_t0(hd(bh_g, q), u0[q]) + kv[q] for q in heads]
        h0 = gi * group
        st = s_scr[pl.ds(h0, group)]
        res = [_dot(jnp.concatenate([qm[q], m[q]], axis=0).astype(BF16), st[q].astype(BF16)) for q in heads]
        s_scr[pl.ds(h0, group)] = jnp.stack([res[q][c:] + nn[q] for q in heads])
        o_ref[:, sl] = jnp.concatenate([res[q][:c] + o0[q] for q in heads], axis=1)
        return carry

    lax.fori_loop(0, n_heads // group, group_body, 0)


def rwkv_scan(r, v, kk, lw, kd, bd, batch, dims):
    t, rw = r.shape
    n_ctx, ctx_len, seq = dims
    c = CHUNK
    n_cc = ctx_len // c
    n_lc = seq // c
    steps = n_cc + n_lc

    def blk(b, d, s):
        ctx_i = jnp.where(d == 0, s, n_cc - 1 - s)
        lat_i = jnp.where(d == 0, s - n_cc, n_lc - 1 - (s - n_cc))
        return jnp.where(s < n_cc, b * n_cc + ctx_i, n_ctx // c + b * n_lc + lat_i)

    shared = pl.BlockSpec((c, rw), lambda b, d, s: (blk(b, d, s), 0))
    per_dir = pl.BlockSpec((None, c, rw), lambda b, d, s: (d, blk(b, d, s), 0))
    n_heads = rw // HEAD
    group = 16 if n_heads % 16 == 0 else 2
    prep = pltpu.VMEM((c, rw), BF16)
    return pl.pallas_call(
        functools.partial(_rwkv_scan_kernel, n_heads=n_heads, group=group),
        grid=(batch, 2, steps),
        in_specs=[shared, shared, shared, per_dir, per_dir, per_dir],
        out_specs=per_dir,
        out_shape=jax.ShapeDtypeStruct((2, t, rw), F32),
        scratch_shapes=[pltpu.VMEM((n_heads, HEAD, HEAD), F32)] + [prep] * 7 + [pltpu.VMEM((1, rw), F32)],
        compiler_params=_cparams(3), name="rwkv_scan",
    )(r, v, kk, lw, kd, bd)


def _rwkv_readout_kernel(o_ref, bv_ref, g_ref, lg_ref, lb_ref, ones_ref, y_ref):
    o = o_ref[0] + o_ref[1]
    ones_bd = ones_ref[...]
    mu = _head_sum(o, ones_bd) * (1.0 / HEAD)
    dev = o - mu
    var = _head_sum(dev * dev, ones_bd) * (1.0 / HEAD)
    on = dev * lax.rsqrt(var + GN_EPS) * lg_ref[...] + lb_ref[...]
    y_ref[...] = ((on + bv_ref[...]) * g_ref[...]).astype(y_ref.dtype)


def rwkv_readout(o2, bv, g, lnx_g, lnx_b, tm_pref=512):
    _, t, rw = o2.shape
    tm = _tile(t, tm_pref)
    tn = _tile(rw, 512)
    ones_bd = (jnp.arange(tn)[:, None] // HEAD == jnp.arange(tn)[None, :] // HEAD).astype(BF16)
    tok = pl.BlockSpec((tm, tn), lambda i, j: (i, j))
    vec = pl.BlockSpec((1, tn), lambda i, j: (0, j))
    return pl.pallas_call(
        _rwkv_readout_kernel,
        grid=(t // tm, rw // tn),
        in_specs=[pl.BlockSpec((2, tm, tn), lambda i, j: (0, i, j)), tok, tok, vec, vec,
                  pl.BlockSpec((tn, tn), lambda i, j: (0, 0))],
        out_specs=tok,
        out_shape=jax.ShapeDtypeStruct((t, rw), BF16),
        compiler_params=_cparams(2), name="rwkv_readout",
    )(o2, bv, g, lnx_g.reshape(1, rw), lnx_b.reshape(1, rw), ones_bd)


def _attend_group(q_heads, k_all, v_all, bias, sinks):
    nq = bias.shape[0]
    s = _dot_t1(jnp.concatenate(q_heads, axis=0), k_all)
    probs, denoms = [], []
    for g, sink in enumerate(sinks):
        sg = s[g * nq:(g + 1) * nq] + bias
        m = jnp.maximum(jnp.max(sg, axis=-1, keepdims=True), sink)
        p = jnp.exp(sg - m)
        denoms.append(jnp.sum(p, axis=-1, keepdims=True) + jnp.exp(sink - m))
        probs.append(p.astype(BF16))
    o = _dot(jnp.concatenate(probs, axis=0), v_all)
    return [o[g * nq:(g + 1) * nq] / denoms[g] for g in range(len(sinks))]


def _attn_kernel(sink_ref, q_ref, kp_ref, kc_ref, kn_ref, vp_ref, vc_ref, vn_ref, kx_ref, vx_ref, o_ref,
                 *, seq, n_ctx_keys, n_ctx_blocks):
    pair = pl.program_id(2)
    qi = pl.program_id(1) - n_ctx_blocks
    span = QBLOCK + 2 * WINDOW
    rr = lax.broadcasted_iota(jnp.int32, (QBLOCK, span + n_ctx_keys), 0)
    cc = lax.broadcasted_iota(jnp.int32, (QBLOCK, span + n_ctx_keys), 1)
    key_pos = qi * QBLOCK - WINDOW + cc
    off = cc - WINDOW - rr
    in_band = (jnp.abs(off) <= WINDOW) & (key_pos >= 0) & (key_pos < seq) & (qi >= 0)
    bias = jnp.where(jnp.logical_or(cc >= span, in_band), 0.0, -jnp.inf)
    q_all = q_ref[...] * jnp.asarray(ATT_SCALE, q_ref.dtype)
    outs = []
    for kv in range(2):
        hs = slice(kv * HEAD, (kv + 1) * HEAD)
        k_all = jnp.concatenate([kp_ref[:, hs], kc_ref[:, hs], kn_ref[:, hs], kx_ref[:, hs]], axis=0)
        v_all = jnp.concatenate([vp_ref[:, hs], vc_ref[:, hs], vn_ref[:, hs], vx_ref[:, hs]], axis=0)
        heads = [kv * ATT_GROUP + g for g in range(ATT_GROUP)]
        outs += _attend_group([q_all[:, hq * HEAD:(hq + 1) * HEAD] for hq in heads], k_all, v_all, bias,
                              [sink_ref[pair * 2 * ATT_GROUP + hq] for hq in heads])
    o_ref[...] = jnp.concatenate(outs, axis=1).astype(o_ref.dtype)


def attention(qkv, sink, batch, dims, q_cols, kv_cols):
    t = qkv.shape[0]
    n_ctx, ctx_len, seq = dims
    n_pairs = kv_cols // (2 * HEAD)
    qw = 2 * ATT_GROUP * HEAD
    kw = 2 * HEAD
    nqb = seq // QBLOCK
    ncb = ctx_len // QBLOCK
    lat0 = n_ctx // QBLOCK
    kcol = q_cols // kw
    vcol = (q_cols + kv_cols) // kw
    smem = pl.BlockSpec(memory_space=pltpu.SMEM)

    def q_map(b, i, p):
        return (jnp.where(i < ncb, b * ncb + i, lat0 + b * nqb + i - ncb), p)

    def band(col0, shift):
        def imap(b, i, p):
            return (lat0 + b * nqb + jnp.clip(i - ncb + shift, 0, nqb - 1), col0 + p)
        return pl.BlockSpec((QBLOCK, kw), imap)

    return pl.pallas_call(
        functools.partial(_attn_kernel, seq=seq, n_ctx_keys=ctx_len, n_ctx_blocks=ncb),
        grid=(batch, ncb + nqb, n_pairs),
        in_specs=[smem,
                  pl.BlockSpec((QBLOCK, qw), q_map),
                  band(kcol, -1), band(kcol, 0), band(kcol, 1),
                  band(vcol, -1), band(vcol, 0), band(vcol, 1),
                  pl.BlockSpec((ctx_len, kw), lambda b, i, p: (b, kcol + p)),
                  pl.BlockSpec((ctx_len, kw), lambda b, i, p: (b, vcol + p))],
        out_specs=pl.BlockSpec((QBLOCK, qw), q_map),
        out_shape=jax.ShapeDtypeStruct((t, q_cols), BF16),
        compiler_params=_cparams(3), name="attention",
    )(sink, qkv, qkv, qkv, qkv, qkv, qkv, qkv, qkv, qkv)


def _rope_tables(seq, width):
    pos = jnp.arange(seq, dtype=jnp.int32)
    row_pos = (pos // GRID_W).astype(F32)
    col_pos = (pos % GRID_W).astype(F32)
    half = HEAD // 2
    inv_freq = ROPE_THETA ** (-jnp.arange(0, half, 2, dtype=F32) / half)
    ang_r = row_pos[:, None] * inv_freq
    ang_c = col_pos[:, None] * inv_freq
    cos64 = jnp.concatenate([jnp.cos(ang_r), jnp.cos(ang_r), jnp.cos(ang_c), jnp.cos(ang_c)], axis=-1)
    sin64 = jnp.concatenate([-jnp.sin(ang_r), jnp.sin(ang_r), -jnp.sin(ang_c), jnp.sin(ang_c)], axis=-1)
    reps = width // HEAD
    return jnp.tile(cos64, (1, reps)), jnp.tile(sin64, (1, reps))


def _router_kernel(x_ref, g_ref, sh_ref, sc_ref, rw_ref, rb_ref, h_ref, ids_ref, gts_ref, *, tm, n_ctx, seq):
    rid = _row_id(pl.program_id(0) * tm, n_ctx, seq)
    h = _norm_mod(x_ref[...], g_ref[...], sh_ref[pl.ds(rid, 1), :], sc_ref[pl.ds(rid, 1), :])
    h_ref[...] = h
    logits = _dot_hi(h, rw_ref[...]).T[:rb_ref.shape[0]]
    e = jnp.exp(logits - jnp.max(logits, axis=0, keepdims=True))
    probs = e / jnp.sum(e, axis=0, keepdims=True)
    sel = probs + rb_ref[...]
    n_g, per = N_EXPERT_GROUPS, EXPERTS_PER_GROUP

    def row(a, r):
        return a[r:r + 1, :]

    scores = []
    for gi in range(n_g):
        a, b, c, d = (row(sel, gi * per + r) for r in range(per))
        hi1, lo1 = jnp.maximum(a, b), jnp.minimum(a, b)
        hi2, lo2 = jnp.maximum(c, d), jnp.minimum(c, d)
        scores.append(jnp.maximum(hi1, hi2) + jnp.maximum(jnp.minimum(hi1, hi2), jnp.maximum(lo1, lo2)))
    best = scores[0]
    bg = jnp.zeros_like(best, dtype=jnp.int32)
    for gi in range(1, n_g):
        better = scores[gi] > best
        best = jnp.where(better, scores[gi], best)
        bg = jnp.where(better, gi, bg)
    in_sel = []
    in_prob = []
    for r in range(per):
        s_r = row(sel, r)
        p_r = row(probs, r)
        for gi in range(1, n_g):
            s_r = jnp.where(bg == gi, row(sel, gi * per + r), s_r)
            p_r = jnp.where(bg == gi, row(probs, gi * per + r), p_r)
        in_sel.append(s_r)
        in_prob.append(p_r)

    def argmax_first(vals, exclude):
        bv = None
        for r in range(per):
            v = vals[r] if exclude is None else jnp.where(exclude == r, -jnp.inf, vals[r])
            if bv is None:
                bv, bi = v, jnp.zeros_like(bg)
            else:
                better = v > bv
                bv = jnp.where(better, v, bv)
                bi = jnp.where(better, r, bi)
        return bi

    i1 = argmax_first(in_sel, None)
    i2 = argmax_first(in_sel, i1)

    def pick(vals, idx):
        out = vals[0]
        for r in range(1, per):
            out = jnp.where(idx == r, vals[r], out)
        return out

    p1 = pick(in_prob, i1)
    p2 = pick(in_prob, i2)
    tot = p1 + p2
    ids_ref[...] = jnp.concatenate([bg * per + i1, bg * per + i2], axis=0)
    gts_ref[...] = jnp.concatenate([p1 / tot, p2 / tot], axis=0)


def router(x, g, mods, layer, router_w_pad, router_b, dims, tm_pref=256):
    t, d = x.shape
    n_ctx, seq = dims
    n_e = router_b.shape[0]
    lanes = router_w_pad.shape[1]
    tm = _tile(math.gcd(n_ctx, seq), tm_pref)
    return pl.pallas_call(
        functools.partial(_router_kernel, tm=tm, n_ctx=n_ctx, seq=seq),
        grid=(t // tm,),
        in_specs=[pl.BlockSpec((tm, d), lambda i: (i, 0)),
                  pl.BlockSpec((1, d), lambda i: (0, 0)),
                  pl.BlockSpec((None, None, MOD_ROWS, d), lambda i: (layer, 3, 0, 0)),
                  pl.BlockSpec((None, None, MOD_ROWS, d), lambda i: (layer, 4, 0, 0)),
                  pl.BlockSpec((d, lanes), lambda i: (0, 0)),
                  pl.BlockSpec((n_e, 1), lambda i: (0, 0))],
        out_specs=[pl.BlockSpec((tm, d), lambda i: (i, 0)),
                   pl.BlockSpec((2, tm), lambda i: (0, i)),
                   pl.BlockSpec((2, tm), lambda i: (0, i))],
        out_shape=[jax.ShapeDtypeStruct((t, d), F32),
                   jax.ShapeDtypeStruct((2, t), jnp.int32),
                   jax.ShapeDtypeStruct((2, t), F32)],
        compiler_params=_cparams(1), name="router",
    )(x, g.reshape(1, d), mods, mods, router_w_pad, router_b.reshape(n_e, 1))


def _gather(src_hbm, idx_ref, base, n_rows, dst, sem):
    def copy(r):
        return pltpu.make_async_copy(src_hbm.at[pl.ds(idx_ref[base + r], 1), :], dst.at[pl.ds(r, 1), :], sem)

    def start_row(r, c):
        copy(r).start()
        return c

    def wait_row(r, c):
        copy(r).wait()
        return c

    def start(unroll=8):
        lax.fori_loop(0, n_rows, start_row, 0, unroll=unroll)

    def wait():
        lax.fori_loop(0, n_rows, wait_row, 0, unroll=8)

    return start, wait


def _expert_kernel(te_ref, act_ref, tok_ref, h_hbm, wg_ref, wu_ref, wd_ref, y_ref, xbuf, sems, *, tm):
    i = pl.program_id(0)
    slot = i % 2

    def gather(tile, s):
        return _gather(h_hbm, tok_ref, tile * tm, tm, xbuf.at[s], sems.at[s])

    @pl.when(jnp.logical_and(i == 0, act_ref[0] == 1))
    def _():
        gather(0, 0)[0]()

    @pl.when(act_ref[i] == 1)
    def _():
        gather(i, slot)[1]()
        gather(i + 1, 1 - slot)[0](unroll=True)
        x = xbuf[slot].astype(BF16)
        a = _dot(x, wg_ref[...])
        hid = (a * _sigmoid(a)) * _dot(x, wu_ref[...])
        y_ref[...] = _dot(hid.astype(BF16), wd_ref[...])

    @pl.when(act_ref[i] == 0)
    def _():
        @pl.when(jnp.logical_and(i > 0, act_ref[jnp.maximum(i - 1, 0)] == 1))
        def _():
            gather(i, slot)[1]()
        y_ref[...] = jnp.zeros_like(y_ref)


def expert_ffn(h, tile_expert, tile_active, slot_token, w_gate, w_up, w_down, layer, tm):
    t, d = h.shape
    _, n_e, _, f = w_gate.shape
    n_tiles = tile_expert.shape[0]
    single = pl.Buffered(1)
    grid_spec = pltpu.PrefetchScalarGridSpec(
        num_scalar_prefetch=3,
        grid=(n_tiles,),
        in_specs=[pl.BlockSpec(memory_space=pl.ANY),
                  pl.BlockSpec((None, None, d, f), lambda i, te, act, tok: (layer, te[i], 0, 0), pipeline_mode=single),
                  pl.BlockSpec((None, None, d, f), lambda i, te, act, tok: (layer, te[i], 0, 0), pipeline_mode=single),
                  pl.BlockSpec((None, None, f, d), lambda i, te, act, tok: (layer, te[i], 0, 0), pipeline_mode=single)],
        out_specs=pl.BlockSpec((tm, d), lambda i, te, act, tok: (i, 0)),
        scratch_shapes=[pltpu.VMEM((2, tm, d), F32), pltpu.SemaphoreType.DMA((2,))],
    )
    return pl.pallas_call(
        functools.partial(_expert_kernel, tm=tm),
        grid_spec=grid_spec,
        out_shape=jax.ShapeDtypeStruct((n_tiles * tm, d), F32),
        compiler_params=_cparams(1), name="expert_ffn",
    )(tile_expert, tile_active, slot_token, h, w_gate, w_up, w_down)


def _combine_kernel(slots_ref, x_ref, gate_ref, rw_ref, y_hbm, o_ref, ybuf, sems, *, tm, n_tok, n_ctx, seq):
    i = pl.program_id(0)
    n = pl.num_programs(0)
    slot = i % 2

    def gathers(tile, s):
        return [_gather(y_hbm, slots_ref, k * n_tok + tile * tm, tm, ybuf.at[s, k], sems.at[s]) for k in range(2)]

    @pl.when(i == 0)
    def _():
        for start, _ in gathers(0, 0):
            start()

    @pl.when(i + 1 < n)
    def _():
        for start, _ in gathers(i + 1, 1 - slot):
            start()

    for _, wait in gathers(i, slot):
        wait()
    rid = _row_id(i * tm, n_ctx, seq)
    mixed = rw_ref[:, 0:1] * ybuf[slot, 0] + rw_ref[:, 1:2] * ybuf[slot, 1]
    o_ref[...] = x_ref[...] + gate_ref[pl.ds(rid, 1), :] * mixed


def moe_combine(x, y_sorted, tok_slots, route_w, mods, layer, dims, tm_pref=256):
    t, d = x.shape
    n_ctx, seq = dims
    tm = _tile(math.gcd(n_ctx, seq), tm_pref)
    grid_spec = pltpu.PrefetchScalarGridSpec(
        num_scalar_prefetch=1,
        grid=(t // tm,),
        in_specs=[pl.BlockSpec((tm, d), lambda i, s: (i, 0)),
                  pl.BlockSpec((None, None, MOD_ROWS, d), lambda i, s: (layer, 5, 0, 0)),
                  pl.BlockSpec((tm, 2), lambda i, s: (i, 0)),
                  pl.BlockSpec(memory_space=pl.ANY)],
        out_specs=pl.BlockSpec((tm, d), lambda i, s: (i, 0)),
        scratch_shapes=[pltpu.VMEM((2, 2, tm, d), F32), pltpu.SemaphoreType.DMA((2,))],
    )
    return pl.pallas_call(
        functools.partial(_combine_kernel, tm=tm, n_tok=t, n_ctx=n_ctx, seq=seq),
        grid_spec=grid_spec,
        out_shape=jax.ShapeDtypeStruct((t, d), F32),
        input_output_aliases={1: 0},
        compiler_params=_cparams(1), name="moe_combine",
    )(tok_slots, x, mods, route_w, y_sorted)


def _moe_plan(ids, n_e, tm):
    n_tok = ids.shape[1]
    e_flat = ids.reshape(-1)
    n_pairs = e_flat.shape[0]
    onehot = (e_flat[:, None] == jnp.arange(n_e, dtype=jnp.int32)[None, :]).astype(jnp.int32)
    rank = jnp.take_along_axis(jnp.cumsum(onehot, axis=0), e_flat[:, None], axis=1)[:, 0] - 1
    counts = jnp.sum(onehot, axis=0)
    padded = (counts + tm - 1) // tm * tm
    ends = jnp.cumsum(padded)
    starts = ends - padded
    dest = starts[e_flat] + rank
    n_tiles = n_pairs // tm + n_e + 1
    n_slots = n_tiles * tm
    pair_tok = jnp.arange(n_pairs, dtype=jnp.int32) % n_tok
    slot_token = jnp.zeros((n_slots,), jnp.int32).at[dest].set(pair_tok, unique_indices=True)
    tile_start = jnp.arange(n_tiles, dtype=jnp.int32) * tm
    tile_expert = jnp.minimum(jnp.searchsorted(ends, tile_start, side="right"), n_e - 1).astype(jnp.int32)
    tile_active = (tile_start < ends[-1]).astype(jnp.int32)
    last_expert = tile_expert[jnp.maximum(ends[-1] // tm - 1, 0)]
    tile_expert = jnp.where(tile_active == 1, tile_expert, last_expert)
    return tile_expert, tile_active, slot_token, dest.astype(jnp.int32)


def moe_layer(x, norm_g, mods, layer, router_w_pad, router_b, w_gate, w_up, w_down, dims, tm_pref=256):
    n_e = w_gate.shape[1]
    h, ids, gates = router(x, norm_g, mods, layer, router_w_pad, router_b, dims)
    tm = _tile(2 * x.shape[0], tm_pref)
    tile_expert, tile_active, slot_token, dest = _moe_plan(ids, n_e, tm)
    y_sorted = expert_ffn(h, tile_expert, tile_active, slot_token, w_gate, w_up, w_down, layer, tm)
    return moe_combine(x, y_sorted, dest, gates.T, mods, layer, dims)


def fourier_rwkv_layer(x, mods, layer, j, norm_g, w_in, w_in_b, w_out_b, shift_mu, w0, w_up, a0, a_up, g_up,
                       k_k, k_a, r_k, lnx_g, lnx_b, batch, dims, fnet_tabs):
    n_ctx, ctx_len, seq = dims
    rw = w0.shape[1]
    fw = w_in.shape[1] - 3 * rw - w_up.shape[1] - a_up.shape[1] - g_up.shape[0]
    dl, il, gl = w_up.shape[1], a_up.shape[1], g_up.shape[0]
    lora_pad = 128
    rdims = (n_ctx, seq)

    def pad_cols(a, n):
        return jnp.pad(a, ((0, 0), (0, n - a.shape[1])))

    o_lo = fw + 3 * rw
    w_lo = jnp.concatenate([pad_cols(w_in[:, o_lo:o_lo + dl], lora_pad),
                            pad_cols(w_in[:, o_lo + dl:o_lo + dl + il], lora_pad),
                            w_in[:, o_lo + dl + il:]], axis=1).astype(BF16)
    mu = shift_mu.reshape(1, -1)
    mu_lo = jnp.concatenate([pad_cols(mu[:, 3 * rw:3 * rw + dl], lora_pad),
                             pad_cols(mu[:, 3 * rw + dl:3 * rw + dl + il], lora_pad),
                             mu[:, 3 * rw + dl + il:]], axis=1)
    w_up_p = jnp.pad(w_up, ((0, 0), (0, lora_pad - dl), (0, 0))).astype(BF16)
    a_up_p = jnp.pad(a_up, ((0, 0), (0, lora_pad - il), (0, 0))).astype(BF16)

    h = norm_mod(x, norm_g, mods, layer, 0, 1, rdims)
    f = matmul(h, w_in_b, BF16, rdims, cols=(0, fw), w_index=j)
    prkv = matmul(h, w_in_b, BF16, rdims, cols=(fw, 3 * rw), w_index=j)
    plo = matmul(h, w_lo, F32, rdims)

    r, v, kk, lw, kd, bd, g, bv = rwkv_terms(
        prkv, plo, mu[:, :3 * rw], mu_lo, w0, w_up_p, a0, a_up_p, g_up.astype(BF16),
        k_k.reshape(1, rw), k_a.reshape(1, rw), r_k.reshape(1, rw), dims, lora_pad)
    o2 = rwkv_scan(r, v, kk, lw, kd, bd, batch, dims)
    y_rw = rwkv_readout(o2, bv, g, lnx_g, lnx_b)

    cos_c, sin_c, cos_x, sin_x = fnet_tabs
    gc = fw // FNET_GROUPS
    y1c, y2c = fnet_channel_dft(f, cos_c, sin_c, 0, n_ctx)
    y1l, y2l = fnet_channel_dft(f, cos_c, sin_c, n_ctx, x.shape[0] - n_ctx)
    f_ctx = fnet_sequence_dft(y1c, y2c, 0, batch, ctx_len, cos_x, sin_x, gc)
    f_lat = fnet_sequence_fft(y1l, y2l, batch, seq, gc)
    f_mix = jnp.concatenate([f_ctx, f_lat], axis=0)
    return matmul_gated_residual([f_mix, y_rw], w_out_b, j, x, mods, layer, 2, rdims)


def attention_layer(x, mods, layer, j, norm_g, w_qkv_b, w_o_b, sink, batch, dims, rope_tabs):
    n_ctx, ctx_len, seq = dims
    q_cols = w_o_b.shape[1]
    kv_cols = (w_qkv_b.shape[2] - q_cols) // 2
    cos_t, sin_t = rope_tabs
    h = norm_mod(x, norm_g, mods, layer, 0, 1, (n_ctx, seq))
    qkv = matmul(h, w_qkv_b, BF16, (n_ctx, seq), rope=(q_cols + kv_cols, cos_t, sin_t), w_index=j,
                 tn_pref=cos_t.shape[1])
    att = attention(qkv, sink, batch, dims, q_cols, kv_cols)
    return matmul_gated_residual([att], w_o_b, j, x, mods, layer, 2, (n_ctx, seq))


def kernel(x, c, ctx, c_ctx, ada_down, ada_up, ada_bias, norm1_g, norm2_g, final_g, mix_w_in, mix_w_out, shift_mu, decay_w0, decay_up, iclr_a0, iclr_up, gate_up, k_k, k_a, r_k, lnx_g, lnx_b, att_w_qkv, att_w_o, att_sink, router_w, router_b, exp_w_gate, exp_w_up, exp_w_down):
    batch, seq, d = x.shape
    ctx_len = ctx.shape[1]
    depth = ada_down.shape[0]
    n_ctx = batch * ctx_len
    assert batch + 1 <= MOD_ROWS
    dims = (n_ctx, ctx_len, seq)
    xs = jnp.concatenate([ctx.reshape(n_ctx, d), x.reshape(batch * seq, d)], axis=0)

    cvec = jnp.concatenate([c_ctx[None, :], c, jnp.zeros((MOD_ROWS - 1 - batch, d), F32)], axis=0)
    mods = adaln_all(cvec, ada_down, ada_up, ada_bias)
    mods = mods.reshape(depth, MOD_ROWS, N_MOD, d).transpose(0, 2, 1, 3)

    fw = mix_w_in.shape[2] - 3 * decay_w0.shape[2] - decay_up.shape[2] - iclr_up.shape[2] - gate_up.shape[1]
    gc = fw // FNET_GROUPS
    fnet_tabs = _dft_tables(gc) + _dft_tables(ctx_len)
    kv_cols = (att_w_qkv.shape[2] - att_w_o.shape[1]) // 2
    rope_tabs = _rope_tables(seq, min(512, kv_cols))
    router_w_pad = jnp.pad(router_w, ((0, 0), (0, 128 - router_w.shape[1])))
    mix_w_in_b = mix_w_in.astype(BF16)
    mix_w_out_b = mix_w_out.astype(BF16)
    att_w_qkv_b = att_w_qkv.astype(BF16)
    att_w_o_b = att_w_o.astype(BF16)
    exp_w_gate_b = exp_w_gate.astype(BF16)
    exp_w_up_b = exp_w_up.astype(BF16)
    exp_w_down_b = exp_w_down.astype(BF16)

    for layer in range(depth):
        j = layer // 2
        if layer % 2 == 0:
            xs = fourier_rwkv_layer(xs, mods, layer, j, norm1_g[layer], mix_w_in[j], mix_w_in_b, mix_w_out_b, shift_mu[j],
                                    decay_w0[j], decay_up[j], iclr_a0[j], iclr_up[j], gate_up[j],
                                    k_k[j], k_a[j], r_k[j], lnx_g[j], lnx_b[j], batch, dims, fnet_tabs)
        else:
            xs = attention_layer(xs, mods, layer, j, norm1_g[layer], att_w_qkv_b, att_w_o_b, att_sink[j],
                                 batch, dims, rope_tabs)
        xs = moe_layer(xs, norm2_g[layer], mods, layer, router_w_pad, router_b,
                       exp_w_gate_b, exp_w_up_b, exp_w_down_b, (n_ctx, seq))
    return final_norm(xs, final_g, n_ctx).reshape(batch, seq, d)
```

```python
import functools
import math

import jax
import jax.numpy as jnp
from jax import lax
from jax.experimental import pallas as pl
from jax.experimental.pallas import tpu as pltpu

F32 = jnp.float32
BF16 = jnp.bfloat16
HIGHEST = lax.Precision.HIGHEST

GRID_W = 64
NORM_EPS = 1e-6
GN_EPS = 64e-5
L2_EPS = 1e-12
N_MOD = 6
FNET_GROUPS = 4
HEAD = 64
ATT_GROUP = 8
ATT_SCALE = 0.125
WINDOW = 128
QBLOCK = 128
ROPE_THETA = 10000.0
N_EXPERT_GROUPS = 4
EXPERTS_PER_GROUP = 4
CHUNK = 64
MOD_ROWS = 8

VMEM_LIMIT = 56 * 1024 * 1024


def _cparams(n_axes):
    return pltpu.CompilerParams(dimension_semantics=("arbitrary",) * n_axes,
                                vmem_limit_bytes=VMEM_LIMIT)


def _tile(n, pref):
    t = min(n, pref)
    while n % t:
        t //= 2
    return t


def _row_id(r0, n_ctx, seq):
    return jnp.where(r0 < n_ctx, 0, 1 + (r0 - n_ctx) // seq)


def _dot(a, b):
    return jnp.dot(a, b, preferred_element_type=F32)


def _dot_hi(a, b):
    return jnp.dot(a, b, preferred_element_type=F32, precision=HIGHEST)


def _dot_t0(a, b):
    return lax.dot_general(a, b, (((0,), (0,)), ((), ())), preferred_element_type=F32)


def _dot_t1(a, b):
    return lax.dot_general(a, b, (((1,), (1,)), ((), ())), preferred_element_type=F32)


def _dot_hi_t1(a, b):
    return lax.dot_general(a, b, (((1,), (1,)), ((), ())), preferred_element_type=F32, precision=HIGHEST)


def _sigmoid(x):
    return 1.0 / (1.0 + jnp.exp(-x))


def _norm_mod(x, g, sh, sc):
    ms = jnp.mean(x * x, axis=-1, keepdims=True)
    y = x * lax.rsqrt(ms + NORM_EPS) * g
    return y * (1.0 + sc) + sh


def _adaln_kernel(cv_ref, down_ref, up_ref, bias_ref, o_ref, t_scr):
    @pl.when(pl.program_id(1) == 0)
    def _():
        cv = cv_ref[...]
        s = cv * _sigmoid(cv)
        t_scr[...] = _dot(s.astype(BF16), down_ref[...].astype(BF16)).astype(BF16)
    o_ref[...] = _dot(t_scr[...], up_ref[...].astype(BF16)) + bias_ref[...]


def adaln_all(cvec, ada_down, ada_up, ada_bias):
    depth, d, rank = ada_down.shape
    n = ada_up.shape[2]
    tn = _tile(n, 2048)
    bias = ada_bias.reshape(depth, 1, n)
    return pl.pallas_call(
        _adaln_kernel,
        grid=(depth, n // tn),
        in_specs=[pl.BlockSpec((MOD_ROWS, d), lambda l, j: (0, 0)),
                  pl.BlockSpec((None, d, rank), lambda l, j: (l, 0, 0)),
                  pl.BlockSpec((None, rank, tn), lambda l, j: (l, 0, j)),
                  pl.BlockSpec((None, 1, tn), lambda l, j: (l, 0, j))],
        out_specs=pl.BlockSpec((None, MOD_ROWS, tn), lambda l, j: (l, 0, j)),
        out_shape=jax.ShapeDtypeStruct((depth, MOD_ROWS, n), F32),
        scratch_shapes=[pltpu.VMEM((MOD_ROWS, rank), BF16)],
        compiler_params=_cparams(2), name="adaln",
    )(cvec, ada_down, ada_up, bias)


def _norm_mod_kernel(x_ref, g_ref, sh_ref, sc_ref, h_ref, *, tm, n_ctx, seq):
    rid = _row_id(pl.program_id(0) * tm, n_ctx, seq)
    h = _norm_mod(x_ref[...], g_ref[...], sh_ref[pl.ds(rid, 1), :], sc_ref[pl.ds(rid, 1), :])
    h_ref[...] = h.astype(h_ref.dtype)


def norm_mod(x, g, mods, layer, k_shift, k_scale, dims, tm_pref=512):
    t, d = x.shape
    n_ctx, seq = dims
    tm = _tile(math.gcd(n_ctx, seq), tm_pref)
    return pl.pallas_call(
        functools.partial(_norm_mod_kernel, tm=tm, n_ctx=n_ctx, seq=seq),
        grid=(t // tm,),
        in_specs=[pl.BlockSpec((tm, d), lambda i: (i, 0)),
                  pl.BlockSpec((1, d), lambda i: (0, 0)),
                  pl.BlockSpec((None, None, MOD_ROWS, d), lambda i: (layer, k_shift, 0, 0)),
                  pl.BlockSpec((None, None, MOD_ROWS, d), lambda i: (layer, k_scale, 0, 0))],
        out_specs=pl.BlockSpec((tm, d), lambda i: (i, 0)),
        out_shape=jax.ShapeDtypeStruct((t, d), BF16),
        compiler_params=_cparams(1), name="norm_mod",
    )(x, g.reshape(1, d), mods, mods)


def _matmul_kernel(*refs, tm, n_ctx, rope_cols):
    if rope_cols:
        a_ref, w_ref, cos_ref, sin_ref, o_ref = refs
    else:
        a_ref, w_ref, o_ref = refs
    r0 = pl.program_id(0) * tm
    j = pl.program_id(1)
    acc = _dot(a_ref[...], w_ref[...])
    if rope_cols:
        tn = acc.shape[1]
        do_rope = jnp.logical_and(r0 >= n_ctx, j * tn < rope_cols)

        @pl.when(do_rope)
        def _():
            lane = lax.broadcasted_iota(jnp.int32, acc.shape, 1)
            swapped = jnp.where(lane % 32 < 16, pltpu.roll(acc, tn - 16, 1), pltpu.roll(acc, 16, 1))
            o_ref[...] = (acc * cos_ref[...] + swapped * sin_ref[...]).astype(o_ref.dtype)

        @pl.when(jnp.logical_not(do_rope))
        def _():
            o_ref[...] = acc.astype(o_ref.dtype)
    else:
        o_ref[...] = acc.astype(o_ref.dtype)


def matmul(a, w, out_dtype, dims, rope=None, cols=None, w_index=None, tm_pref=1024, tn_pref=512):
    t, d = a.shape
    col0, n = cols if cols is not None else (0, w.shape[-1])
    n_ctx, seq = dims
    tm = _tile(math.gcd(n_ctx, seq), tm_pref)
    tn = _tile(math.gcd(n, col0) if col0 else n, tn_pref)
    jb = col0 // tn
    if w_index is None:
        w_spec = pl.BlockSpec((d, tn), lambda i, j: (0, jb + j))
    else:
        w_spec = pl.BlockSpec((None, d, tn), lambda i, j: (w_index, 0, jb + j))
    in_specs = [pl.BlockSpec((tm, d), lambda i, j: (i, 0)), w_spec]
    args = [a, w]
    rope_cols = 0
    if rope is not None:
        rope_cols, cos_t, sin_t = rope
        assert rope_cols % tn == 0 and cos_t.shape == (seq, tn)
        pos_map = lambda i, j: (jnp.maximum(i * tm - n_ctx, 0) % seq // tm, 0)
        in_specs += [pl.BlockSpec((tm, tn), pos_map), pl.BlockSpec((tm, tn), pos_map)]
        args += [cos_t, sin_t]
    return pl.pallas_call(
        functools.partial(_matmul_kernel, tm=tm, n_ctx=n_ctx, rope_cols=rope_cols),
        grid=(t // tm, n // tn),
        in_specs=in_specs,
        out_specs=pl.BlockSpec((tm, tn), lambda i, j: (i, j)),
        out_shape=jax.ShapeDtypeStruct((t, n), out_dtype),
        compiler_params=_cparams(2), name="matmul",
    )(*args)


def _mmres_kernel(*refs, n_a, tm, n_ctx, seq):
    a_refs = refs[:n_a]
    w_refs = refs[n_a:2 * n_a]
    x_ref, gate_ref, o_ref = refs[2 * n_a:]
    acc = _dot(a_refs[0][...], w_refs[0][...])
    for a_ref, w_ref in zip(a_refs[1:], w_refs[1:]):
        acc += _dot(a_ref[...], w_ref[...])
    rid = _row_id(pl.program_id(0) * tm, n_ctx, seq)
    o_ref[...] = x_ref[...] + gate_ref[pl.ds(rid, 1), :] * acc


def matmul_gated_residual(a_list, w_stack, w_index, x, mods, layer, k_gate, dims, tm_pref=1024, tn_pref=512):
    t, d = x.shape
    n_ctx, seq = dims
    tm = _tile(math.gcd(n_ctx, seq), tm_pref)
    tn = _tile(d, tn_pref)
    n_a = len(a_list)
    assert len({a.shape[1] for a in a_list}) == 1 and n_a * a_list[0].shape[1] == w_stack.shape[1]
    w_list = [w_stack] * n_a
    in_specs = [pl.BlockSpec((tm, a.shape[1]), lambda i, j: (i, 0)) for a in a_list]
    in_specs += [pl.BlockSpec((None, a.shape[1], tn), lambda i, j, p=p: (w_index, p, j)) for p, a in enumerate(a_list)]
    in_specs += [pl.BlockSpec((tm, tn), lambda i, j: (i, j)),
                 pl.BlockSpec((None, None, MOD_ROWS, tn), lambda i, j: (layer, k_gate, 0, j))]
    return pl.pallas_call(
        functools.partial(_mmres_kernel, n_a=n_a, tm=tm, n_ctx=n_ctx, seq=seq),
        grid=(t // tm, d // tn),
        in_specs=in_specs,
        out_specs=pl.BlockSpec((tm, tn), lambda i, j: (i, j)),
        out_shape=jax.ShapeDtypeStruct((t, d), F32),
        input_output_aliases={2 * n_a: 0},
        compiler_params=_cparams(2), name="matmul_gated_residual",
    )(*a_list, *w_list, x, mods)


def _final_norm_kernel(x_ref, g_ref, o_ref):
    x = x_ref[...]
    ms = jnp.mean(x * x, axis=-1, keepdims=True)
    o_ref[...] = x * lax.rsqrt(ms + NORM_EPS) * g_ref[...]


def final_norm(x, g, n_ctx, tm_pref=512):
    t, d = x.shape
    tm = _tile(math.gcd(n_ctx, t - n_ctx), tm_pref)
    off = n_ctx // tm
    return pl.pallas_call(
        _final_norm_kernel,
        grid=((t - n_ctx) // tm,),
        in_specs=[pl.BlockSpec((tm, d), lambda i: (i + off, 0)),
                  pl.BlockSpec((1, d), lambda i: (0, 0))],
        out_specs=pl.BlockSpec((tm, d), lambda i: (i, 0)),
        out_shape=jax.ShapeDtypeStruct((t - n_ctx, d), F32),
        compiler_params=_cparams(1), name="final_norm",
    )(x, g.reshape(1, d))


def _dft_tables(n, split=64):
    k = jnp.arange(n, dtype=jnp.int32)

    def cs(rows, period):
        ang = ((rows[:, None] * k[None, :]) % period).astype(F32) * (2.0 * math.pi / period)
        return jnp.cos(ang), jnp.sin(ang)

    if n % split or n <= split:
        c, s = cs(k, n)
        return c.astype(BF16), s.astype(BF16)
    ca, sa = cs(jnp.arange(n // split, dtype=jnp.int32), n // split)
    cb, sb = cs(jnp.arange(split, dtype=jnp.int32), n)
    ca, sa, cb, sb = ca[:, None, :], sa[:, None, :], cb[None, :, :], sb[None, :, :]
    return ((ca * cb - sa * sb).reshape(n, n).astype(BF16), (sa * cb + ca * sb).reshape(n, n).astype(BF16))


def _fnet_ch_kernel(x_ref, c_ref, s_ref, y1_ref, y2_ref):
    x = x_ref[...]
    y1_ref[...] = _dot(x, c_ref[...]).astype(y1_ref.dtype)
    y2_ref[...] = _dot(x, s_ref[...]).astype(y2_ref.dtype)


def fnet_channel_dft(f, cos_c, sin_c, row0, n_rows, tm_pref=1024):
    width = f.shape[1]
    gc = width // FNET_GROUPS
    tm = _tile(math.gcd(row0, n_rows) if row0 else n_rows, tm_pref)
    ib = row0 // tm
    tab = pl.BlockSpec((gc, gc), lambda i, g: (0, 0))
    out = pl.BlockSpec((tm, gc), lambda i, g: (i, g))
    return pl.pallas_call(
        _fnet_ch_kernel,
        grid=(n_rows // tm, FNET_GROUPS),
        in_specs=[pl.BlockSpec((tm, gc), lambda i, g: (ib + i, g)), tab, tab],
        out_specs=[out, out],
        out_shape=[jax.ShapeDtypeStruct((n_rows, width), BF16)] * 2,
        compiler_params=_cparams(2), name="fnet_channel_dft",
    )(f, cos_c, sin_c)


def _fnet_fft_a_kernel(y1_ref, y2_ref, m1_ref, m2_ref, ct_ref, st_ref, br_ref, bi_ref):
    n1 = y1_ref.shape[0]
    a = _dot(m1_ref[...], y1_ref[...]) + _dot(m2_ref[...], y2_ref[...])
    nb, _, width = br_ref.shape
    for g in range(nb):
        ar = a[:n1, g * width:(g + 1) * width]
        ai = a[n1:, g * width:(g + 1) * width]
        ct, st = ct_ref[g], st_ref[g]
        br_ref[g] = (ar * ct + ai * st).astype(br_ref.dtype)
        bi_ref[g] = (ai * ct - ar * st).astype(bi_ref.dtype)


def _fnet_fft_b_kernel(br_ref, bi_ref, c_ref, s_ref, o_ref, *, scale):
    o_ref[...] = ((_dot(c_ref[...], br_ref[...]) + _dot(s_ref[...], bi_ref[...])) * scale).astype(o_ref.dtype)


def fnet_sequence_fft(y1, y2, n_seq, length, gc, n2=64):
    width = y1.shape[1]
    n1 = length // n2

    def cs(rows, cols, period):
        ang = ((rows[:, None] * cols[None, :]) % period).astype(F32) * (2.0 * math.pi / period)
        return jnp.cos(ang), jnp.sin(ang)

    i1 = jnp.arange(n1, dtype=jnp.int32)
    i2 = jnp.arange(n2, dtype=jnp.int32)
    c1, s1 = cs(i1, i1, n1)
    c2, s2 = cs(i2, i2, n2)
    ct, st = cs(i2, i1, length)
    m1 = jnp.concatenate([c1, -s1], axis=0).astype(BF16)
    m2 = jnp.concatenate([-s1, -c1], axis=0).astype(BF16)
    nb = 4
    y_spec = pl.BlockSpec((n1, nb * width), lambda s, j: (s, j))
    m_spec = pl.BlockSpec((2 * n1, n1), lambda s, j: (0, 0))
    t_spec = pl.BlockSpec((nb, n1, 1), lambda s, j: (j, 0, 0))
    b_spec = pl.BlockSpec((nb, n1, width), lambda s, j: (s * (n2 // nb) + j, 0, 0))
    b_shape = jax.ShapeDtypeStruct((n_seq * n2, n1, width), BF16)
    br, bi = pl.pallas_call(
        _fnet_fft_a_kernel,
        grid=(n_seq, n2 // nb),
        in_specs=[y_spec, y_spec, m_spec, m_spec, t_spec, t_spec],
        out_specs=[b_spec, b_spec],
        out_shape=[b_shape, b_shape],
        compiler_params=_cparams(2), name="fnet_fft_a",
    )(y1.reshape(n_seq * n1, n2 * width), y2.reshape(n_seq * n1, n2 * width), m1, m2,
      ct.reshape(n2, n1, 1), st.reshape(n2, n1, 1))
    tc = _tile(n1 * width, 8192)
    v_spec = pl.BlockSpec((n2, tc), lambda s, j: (s, j))
    w_spec = pl.BlockSpec((n2, n2), lambda s, j: (0, 0))
    out = pl.pallas_call(
        functools.partial(_fnet_fft_b_kernel, scale=1.0 / math.sqrt(length * gc)),
        grid=(n_seq, n1 * width // tc),
        in_specs=[v_spec, v_spec, w_spec, w_spec],
        out_specs=v_spec,
        out_shape=jax.ShapeDtypeStruct((n_seq * n2, n1 * width), BF16),
        compiler_params=_cparams(2), name="fnet_fft_b",
    )(br.reshape(n_seq * n2, n1 * width), bi.reshape(n_seq * n2, n1 * width), c2.astype(BF16), s2.astype(BF16))
    return out.reshape(n_seq * length, width)


def _fnet_seq_kernel(c_ref, s_ref, y1_ref, y2_ref, o_ref, acc_ref, *, scale):
    k = pl.program_id(2)

    @pl.when(k == 0)
    def _():
        acc_ref[...] = jnp.zeros_like(acc_ref)

    acc_ref[...] += _dot(c_ref[...], y1_ref[...]) - _dot(s_ref[...], y2_ref[...])

    @pl.when(k == pl.num_programs(2) - 1)
    def _():
        o_ref[...] = (acc_ref[...] * scale).astype(o_ref.dtype)


def fnet_sequence_dft(y1, y2, row0, n_seq, length, cos_l, sin_l, gc, t_pref=512):
    width = y1.shape[1]
    tm = _tile(length, t_pref)
    tk = _tile(math.gcd(length, row0) if row0 else length, t_pref)
    nb = length // tm
    nk = length // tk
    scale = 1.0 / math.sqrt(length * gc)
    rb0 = row0 // tk
    y_spec = pl.BlockSpec((tk, width), lambda b, i, k: (rb0 + b * nk + k, 0))
    return pl.pallas_call(
        functools.partial(_fnet_seq_kernel, scale=scale),
        grid=(n_seq, nb, nk),
        in_specs=[pl.BlockSpec((tm, tk), lambda b, i, k: (i, k)),
                  pl.BlockSpec((tm, tk), lambda b, i, k: (i, k)),
                  y_spec, y_spec],
        out_specs=pl.BlockSpec((tm, width), lambda b, i, k: (b * nb + i, 0)),
        out_shape=jax.ShapeDtypeStruct((n_seq * length, width), BF16),
        scratch_shapes=[pltpu.VMEM((tm, width), F32)],
        compiler_params=_cparams(3), name="fnet_sequence_dft",
    )(cos_l, sin_l, y1, y2)


def _head_sum(x, ones_bd):
    hi = x.astype(BF16)
    lo = (x - hi.astype(F32)).astype(BF16)
    return _dot(hi, ones_bd) + _dot(lo, ones_bd)


def _shift(x, prev_row, next_row, mu, pos, length):
    rows = x.shape[0]
    ridx = lax.broadcasted_iota(jnp.int32, (rows, 1), 0)
    prev = jnp.where(ridx == 0, prev_row, pltpu.roll(x, 1, 0))
    nxt = jnp.where(ridx == rows - 1, next_row, pltpu.roll(x, rows - 1, 0))
    prev = jnp.where(pos == 0, 0.0, prev)
    nxt = jnp.where(pos == length - 1, 0.0, nxt)
    return x + mu * (0.5 * (prev + nxt) - x)


def _rwkv_terms_kernel(r_ref, k_ref, v_ref, rp_ref, kp_ref, vp_ref, rn_ref, kn_ref, vn_ref,
                       lo_ref, lop_ref, lon_ref, mur_ref, muk_ref, muv_ref, mulo_ref,
                       w0_ref, wup_ref, a0_ref, aup_ref, gup_ref, kk_ref, ka_ref, rk_ref, ones_ref,
                       r_o, v_o, kk_o, lw_o, kd_o, bd_o, g_o, bv_o, *, tm, n_ctx, ctx_len, seq, lora_pad):
    i = pl.program_id(0)
    r0 = i * tm
    ridx = lax.broadcasted_iota(jnp.int32, (tm, 1), 0) + r0
    in_ctx = r0 < n_ctx
    length = jnp.where(in_ctx, ctx_len, seq)
    pos = jnp.where(in_ctx, ridx % ctx_len, (ridx - n_ctx) % seq)

    def sh(ref, pref, nref, mu_ref):
        last = pref.shape[0] - 1
        return _shift(ref[...].astype(F32), pref[last:last + 1, :].astype(F32), nref[0:1, :].astype(F32),
                      mu_ref[...], pos, length)

    r = sh(r_ref, rp_ref, rn_ref, mur_ref)
    k = sh(k_ref, kp_ref, kn_ref, muk_ref)
    v = sh(v_ref, vp_ref, vn_ref, muv_ref)
    lo = sh(lo_ref, lop_ref, lon_ref, mulo_ref)
    w_in = jnp.tanh(lo[:, :lora_pad]).astype(BF16)
    a_in = lo[:, lora_pad:2 * lora_pad].astype(BF16)
    g_in = _sigmoid(lo[:, 2 * lora_pad:]).astype(BF16)
    ones_bd = ones_ref[...]

    kk = k * kk_ref[...]
    kk = kk * lax.rsqrt(_head_sum(kk * kk, ones_bd) + L2_EPS)
    ksum = jnp.zeros_like(k)
    for d in range(2):
        w_logit = w0_ref[d:d + 1, :] + _dot(w_in, wup_ref[d])
        lw_o[d] = -math.exp(-0.5) * _sigmoid(w_logit)
        a = _sigmoid(a0_ref[d:d + 1, :] + _dot(a_in, aup_ref[d]))
        k_d = k * (1.0 + (a - 1.0) * ka_ref[...])
        kd_o[d] = k_d.astype(kd_o.dtype)
        bd_o[d] = (kk * a).astype(bd_o.dtype)
        ksum = ksum + k_d
    g_o[...] = _dot(g_in, gup_ref[...]).astype(g_o.dtype)
    bv_o[...] = (_head_sum(r * ksum * rk_ref[...], ones_bd) * v).astype(bv_o.dtype)
    r_o[...] = r.astype(r_o.dtype)
    v_o[...] = v.astype(v_o.dtype)
    kk_o[...] = kk.astype(kk_o.dtype)


def rwkv_terms(prkv, plo, mu_rkv, mu_lo, w0, w_up, a0, a_up, g_up, k_k, k_a, r_k, dims, lora_pad, tm_pref=256):
    t = prkv.shape[0]
    rw = prkv.shape[1] // 3
    n_ctx, ctx_len, seq = dims
    tm = _tile(math.gcd(n_ctx, seq), tm_pref)
    tn = _tile(rw, 512)
    nj = rw // tn
    lo_w = plo.shape[1]
    hb = tm // 8
    last8 = t // 8 - 1
    hr = 8 * (4 // prkv.dtype.itemsize)
    hbr = tm // hr
    lastr = t // hr - 1

    def main(c):
        return pl.BlockSpec((tm, tn), lambda i, j: (i, c * nj + j))

    def prev(c):
        return pl.BlockSpec((hr, tn), lambda i, j: (jnp.maximum(i * hbr - 1, 0), c * nj + j))

    def nxt(c):
        return pl.BlockSpec((hr, tn), lambda i, j: (jnp.minimum((i + 1) * hbr, lastr), c * nj + j))

    def vec(c=0):
        return pl.BlockSpec((1, tn), lambda i, j: (0, c * nj + j))

    ones_bd = (jnp.arange(tn)[:, None] // HEAD == jnp.arange(tn)[None, :] // HEAD).astype(BF16)
    out_tok = pl.BlockSpec((tm, tn), lambda i, j: (i, j))
    out_dir = pl.BlockSpec((2, tm, tn), lambda i, j: (0, i, j))
    tok = jax.ShapeDtypeStruct((t, rw), BF16)
    tok2 = jax.ShapeDtypeStruct((2, t, rw), BF16)
    lw2 = jax.ShapeDtypeStruct((2, t, rw), F32)
    return pl.pallas_call(
        functools.partial(_rwkv_terms_kernel, tm=tm, n_ctx=n_ctx, ctx_len=ctx_len, seq=seq, lora_pad=lora_pad),
        grid=(t // tm, nj),
        in_specs=[main(0), main(1), main(2), prev(0), prev(1), prev(2), nxt(0), nxt(1), nxt(2),
                  pl.BlockSpec((tm, lo_w), lambda i, j: (i, 0)),
                  pl.BlockSpec((8, lo_w), lambda i, j: (jnp.maximum(i * hb - 1, 0), 0)),
                  pl.BlockSpec((8, lo_w), lambda i, j: (jnp.minimum((i + 1) * hb, last8), 0)),
                  vec(0), vec(1), vec(2),
                  pl.BlockSpec((1, lo_w), lambda i, j: (0, 0)),
                  pl.BlockSpec((2, tn), lambda i, j: (0, j)),
                  pl.BlockSpec((2, lora_pad, tn), lambda i, j: (0, 0, j)),
                  pl.BlockSpec((2, tn), lambda i, j: (0, j)),
                  pl.BlockSpec((2, lora_pad, tn), lambda i, j: (0, 0, j)),
                  pl.BlockSpec((g_up.shape[0], tn), lambda i, j: (0, j)),
                  vec(), vec(), vec(),
                  pl.BlockSpec((tn, tn), lambda i, j: (0, 0))],
        out_specs=[out_tok, out_tok, out_tok, out_dir, out_dir, out_dir, out_tok, out_tok],
        out_shape=[tok, tok, tok, lw2, tok2, tok2, tok, tok],
        compiler_params=_cparams(2), name="rwkv_terms",
    )(prkv, prkv, prkv, prkv, prkv, prkv, prkv, prkv, prkv, plo, plo, plo,
      mu_rkv, mu_rkv, mu_rkv, mu_lo, w0, w_up, a0, a_up, g_up, k_k, k_a, r_k, ones_bd)


def _rwkv_scan_kernel(r_ref, v_ref, kk_ref, lw_ref, kd_ref, bd_ref, o_ref,
                      s_scr, rt_s, at_s, kt_s, bt_s, bh_s, kh_s, vb_s, pd_s, *, n_heads, group):
    d = pl.program_id(1)
    step = pl.program_id(2)
    c = CHUNK

    @pl.when(step == 0)
    def _():
        s_scr[...] = jnp.zeros_like(s_scr)

    row = lax.broadcasted_iota(jnp.int32, (c, c), 0)
    col = lax.broadcasted_iota(jnp.int32, (c, c), 1)
    fwd = d == 0
    ahead = (row - col) * jnp.where(fwd, 1, -1)
    incl = ahead >= 0
    strict = ahead > 0
    eye = row == col
    eye_f = jnp.where(eye, 1.0, 0.0)

    tri = jnp.where(incl, 1.0, 0.0).astype(BF16)
    lw = lw_ref[...]
    lw_hi = lw.astype(BF16)
    lw_lo = (lw - lw_hi.astype(F32)).astype(BF16)
    l_inc = _dot(tri, lw_hi) + _dot(tri, lw_lo)
    l_tot = jnp.where(fwd, l_inc[c - 1:c, :], l_inc[0:1, :])
    e_neg = jnp.exp(-l_inc)
    e_rem = jnp.exp(l_tot - l_inc)
    kd = kd_ref[...]
    bd = bd_ref[...]
    rt_s[...] = (r_ref[...] * jnp.exp(l_inc)).astype(BF16)
    at_s[...] = (-kk_ref[...] * jnp.exp(l_inc - lw)).astype(BF16)
    kt_s[...] = (kd * e_neg).astype(BF16)
    bt_s[...] = (bd * e_neg).astype(BF16)
    bh_s[...] = (bd * e_rem).astype(BF16)
    kh_s[...] = (kd * e_rem).astype(BF16)
    vb_s[...] = v_ref[...].astype(BF16)
    pd_s[...] = jnp.exp(l_tot)

    gw = group * HEAD

    def group_body(gi, carry):
        sl = pl.ds(pl.multiple_of(gi * gw, gw), gw)
        rt_g, at_g, kt_g, bt_g, bh_g, kh_g, v_g = (s[:, sl] for s in (rt_s, at_s, kt_s, bt_s, bh_s, kh_s, vb_s))
        pd_g = pd_s[:, sl]
        heads = range(group)

        def hd(a, q):
            return a[:, q * HEAD:(q + 1) * HEAD]

        lhs = [jnp.concatenate([hd(at_g, q), hd(rt_g, q)], axis=0) for q in heads]
        gb = [_dot_t1(lhs[q], hd(bt_g, q)) for q in heads]
        gk = [_dot_t1(lhs[q], hd(kt_g, q)) for q in heads]
        a_ab = [jnp.where(strict, gb[q][:c], 0.0) for q in heads]
        a_rb = [jnp.where(incl, gb[q][c:], 0.0).astype(BF16) for q in heads]
        a_k = [jnp.concatenate([jnp.where(strict, gk[q][:c], 0.0), jnp.where(incl, gk[q][c:], 0.0)],
                               axis=0).astype(BF16) for q in heads]
        vv = [_dot(a_k[q], hd(v_g, q)) for q in heads]
        kv = [_dot_t0(hd(kh_g, q), hd(v_g, q)) for q in heads]
        tinv = [eye_f + a for a in a_ab]
        x = [_dot(a.astype(BF16), a.astype(BF16)).astype(BF16) for a in a_ab]
        n_sq = int(math.log2(c)) - 1
        for it in range(n_sq):
            if it < n_sq - 1:
                both = [_dot(jnp.concatenate([tinv[q].astype(BF16), x[q]], axis=0), x[q]) for q in heads]
                tinv = [tinv[q] + both[q][:c] for q in heads]
                x = [both[q][c:].astype(BF16) for q in heads]
            else:
                tinv = [tinv[q] + _dot(tinv[q].astype(BF16), x[q]) for q in heads]
        tb = [t.astype(BF16) for t in tinv]
        wt = [_dot(tb[q], hd(at_g, q)).astype(BF16) for q in heads]
        u0 = [_dot(tb[q], vv[q][:c].astype(BF16)).astype(BF16) for q in heads]
        qm = [hd(rt_g, q).astype(F32) + _dot(a_rb[q], wt[q]) for q in heads]
        o0 = [_dot(a_rb[q], u0[q]) + vv[q][c:] for q in heads]
        m = [eye_f * hd(pd_g, q) + _dot_t0(hd(bh_g, q), wt[q]) for q in heads]
        nn = [_dot_t0(hd(bh_g, q), u0[q]) + kv[q] for q in heads]
        h0 = gi * group
        st = s_scr[pl.ds(h0, group)]
        res = [_dot(jnp.concatenate([qm[q], m[q]], axis=0).astype(BF16), st[q].astype(BF16)) for q in heads]
        s_scr[pl.ds(h0, group)] = jnp.stack([res[q][c:] + nn[q] for q in heads])
        o_ref[:, sl] = jnp.concatenate([res[q][:c] + o0[q] for q in heads], axis=1)
        return carry

    lax.fori_loop(0, n_heads // group, group_body, 0)


def rwkv_scan(r, v, kk, lw, kd, bd, batch, dims):
    t, rw = r.shape
    n_ctx, ctx_len, seq = dims
    c = CHUNK
    n_cc = ctx_len // c
    n_lc = seq // c
    steps = n_cc + n_lc

    def blk(b, d, s):
        ctx_i = jnp.where(d == 0, s, n_cc - 1 - s)
        lat_i = jnp.where(d == 0, s - n_cc, n_lc - 1 - (s - n_cc))
        return jnp.where(s < n_cc, b * n_cc + ctx_i, n_ctx // c + b * n_lc + lat_i)

    shared = pl.BlockSpec((c, rw), lambda b, d, s: (blk(b, d, s), 0))
    per_dir = pl.BlockSpec((None, c, rw), lambda b, d, s: (d, blk(b, d, s), 0))
    n_heads = rw // HEAD
    group = 32 if n_heads % 32 == 0 else 2
    prep = pltpu.VMEM((c, rw), BF16)
    return pl.pallas_call(
        functools.partial(_rwkv_scan_kernel, n_heads=n_heads, group=group),
        grid=(batch, 2, steps),
        in_specs=[shared, shared, shared, per_dir, per_dir, per_dir],
        out_specs=per_dir,
        out_shape=jax.ShapeDtypeStruct((2, t, rw), F32),
        scratch_shapes=[pltpu.VMEM((n_heads, HEAD, HEAD), F32)] + [prep] * 7 + [pltpu.VMEM((1, rw), F32)],
        compiler_params=_cparams(3), name="rwkv_scan",
    )(r, v, kk, lw, kd, bd)


def _rwkv_readout_kernel(o_ref, bv_ref, g_ref, lg_ref, lb_ref, ones_ref, y_ref):
    o = o_ref[0] + o_ref[1]
    ones_bd = ones_ref[...]
    mu = _head_sum(o, ones_bd) * (1.0 / HEAD)
    dev = o - mu
    var = _head_sum(dev * dev, ones_bd) * (1.0 / HEAD)
    on = dev * lax.rsqrt(var + GN_EPS) * lg_ref[...] + lb_ref[...]
    y_ref[...] = ((on + bv_ref[...]) * g_ref[...]).astype(y_ref.dtype)


def rwkv_readout(o2, bv, g, lnx_g, lnx_b, tm_pref=512):
    _, t, rw = o2.shape
    tm = _tile(t, tm_pref)
    tn = _tile(rw, 512)
    ones_bd = (jnp.arange(tn)[:, None] // HEAD == jnp.arange(tn)[None, :] // HEAD).astype(BF16)
    tok = pl.BlockSpec((tm, tn), lambda i, j: (i, j))
    vec = pl.BlockSpec((1, tn), lambda i, j: (0, j))
    return pl.pallas_call(
        _rwkv_readout_kernel,
        grid=(t // tm, rw // tn),
        in_specs=[pl.BlockSpec((2, tm, tn), lambda i, j: (0, i, j)), tok, tok, vec, vec,
                  pl.BlockSpec((tn, tn), lambda i, j: (0, 0))],
        out_specs=tok,
        out_shape=jax.ShapeDtypeStruct((t, rw), BF16),
        compiler_params=_cparams(2), name="rwkv_readout",
    )(o2, bv, g, lnx_g.reshape(1, rw), lnx_b.reshape(1, rw), ones_bd)


def _attend_group(q_heads, k_all, v_all, bias, sinks):
    nq = bias.shape[0]
    s = _dot_t1(jnp.concatenate(q_heads, axis=0), k_all)
    probs, denoms = [], []
    for g, sink in enumerate(sinks):
        sg = s[g * nq:(g + 1) * nq] + bias
        m = jnp.maximum(jnp.max(sg, axis=-1, keepdims=True), sink)
        p = jnp.exp(sg - m)
        denoms.append(jnp.sum(p, axis=-1, keepdims=True) + jnp.exp(sink - m))
        probs.append(p.astype(BF16))
    o = _dot(jnp.concatenate(probs, axis=0), v_all)
    return [o[g * nq:(g + 1) * nq] / denoms[g] for g in range(len(sinks))]


def _attn_kernel(sink_ref, q_ref, kp_ref, kc_ref, kn_ref, vp_ref, vc_ref, vn_ref, kx_ref, vx_ref, o_ref,
                 *, seq, n_ctx_keys, n_ctx_blocks):
    pair = pl.program_id(2)
    qi = pl.program_id(1) - n_ctx_blocks
    span = QBLOCK + 2 * WINDOW
    rr = lax.broadcasted_iota(jnp.int32, (QBLOCK, span + n_ctx_keys), 0)
    cc = lax.broadcasted_iota(jnp.int32, (QBLOCK, span + n_ctx_keys), 1)
    key_pos = qi * QBLOCK - WINDOW + cc
    off = cc - WINDOW - rr
    in_band = (jnp.abs(off) <= WINDOW) & (key_pos >= 0) & (key_pos < seq) & (qi >= 0)
    bias = jnp.where(jnp.logical_or(cc >= span, in_band), 0.0, -jnp.inf)
    q_all = q_ref[...] * jnp.asarray(ATT_SCALE, q_ref.dtype)
    outs = []
    for kv in range(2):
        hs = slice(kv * HEAD, (kv + 1) * HEAD)
        k_all = jnp.concatenate([kp_ref[:, hs], kc_ref[:, hs], kn_ref[:, hs], kx_ref[:, hs]], axis=0)
        v_all = jnp.concatenate([vp_ref[:, hs], vc_ref[:, hs], vn_ref[:, hs], vx_ref[:, hs]], axis=0)
        heads = [kv * ATT_GROUP + g for g in range(ATT_GROUP)]
        outs += _attend_group([q_all[:, hq * HEAD:(hq + 1) * HEAD] for hq in heads], k_all, v_all, bias,
                              [sink_ref[pair * 2 * ATT_GROUP + hq] for hq in heads])
    o_ref[...] = jnp.concatenate(outs, axis=1).astype(o_ref.dtype)


def attention(qkv, sink, batch, dims, q_cols, kv_cols):
    t = qkv.shape[0]
    n_ctx, ctx_len, seq = dims
    n_pairs = kv_cols // (2 * HEAD)
    qw = 2 * ATT_GROUP * HEAD
    kw = 2 * HEAD
    nqb = seq // QBLOCK
    ncb = ctx_len // QBLOCK
    lat0 = n_ctx // QBLOCK
    kcol = q_cols // kw
    vcol = (q_cols + kv_cols) // kw
    smem = pl.BlockSpec(memory_space=pltpu.SMEM)

    def q_map(b, i, p):
        return (jnp.where(i < ncb, b * ncb + i, lat0 + b * nqb + i - ncb), p)

    def band(col0, shift):
        def imap(b, i, p):
            return (lat0 + b * nqb + jnp.clip(i - ncb + shift, 0, nqb - 1), col0 + p)
        return pl.BlockSpec((QBLOCK, kw), imap)

    return pl.pallas_call(
        functools.partial(_attn_kernel, seq=seq, n_ctx_keys=ctx_len, n_ctx_blocks=ncb),
        grid=(batch, ncb + nqb, n_pairs),
        in_specs=[smem,
                  pl.BlockSpec((QBLOCK, qw), q_map),
                  band(kcol, -1), band(kcol, 0), band(kcol, 1),
                  band(vcol, -1), band(vcol, 0), band(vcol, 1),
                  pl.BlockSpec((ctx_len, kw), lambda b, i, p: (b, kcol + p)),
                  pl.BlockSpec((ctx_len, kw), lambda b, i, p: (b, vcol + p))],
        out_specs=pl.BlockSpec((QBLOCK, qw), q_map),
        out_shape=jax.ShapeDtypeStruct((t, q_cols), BF16),
        compiler_params=_cparams(3), name="attention",
    )(sink, qkv, qkv, qkv, qkv, qkv, qkv, qkv, qkv, qkv)


def _rope_tables(seq, width):
    pos = jnp.arange(seq, dtype=jnp.int32)
    row_pos = (pos // GRID_W).astype(F32)
    col_pos = (pos % GRID_W).astype(F32)
    half = HEAD // 2
    inv_freq = ROPE_THETA ** (-jnp.arange(0, half, 2, dtype=F32) / half)
    ang_r = row_pos[:, None] * inv_freq
    ang_c = col_pos[:, None] * inv_freq
    cos64 = jnp.concatenate([jnp.cos(ang_r), jnp.cos(ang_r), jnp.cos(ang_c), jnp.cos(ang_c)], axis=-1)
    sin64 = jnp.concatenate([-jnp.sin(ang_r), jnp.sin(ang_r), -jnp.sin(ang_c), jnp.sin(ang_c)], axis=-1)
    reps = width // HEAD
    return jnp.tile(cos64, (1, reps)), jnp.tile(sin64, (1, reps))


def _router_kernel(x_ref, g_ref, sh_ref, sc_ref, rw_ref, rb_ref, h_ref, ids_ref, gts_ref, *, tm, n_ctx, seq):
    rid = _row_id(pl.program_id(0) * tm, n_ctx, seq)
    h = _norm_mod(x_ref[...], g_ref[...], sh_ref[pl.ds(rid, 1), :], sc_ref[pl.ds(rid, 1), :])
    h_ref[...] = h
    logits = _dot_hi(h, rw_ref[...]).T[:rb_ref.shape[0]]
    e = jnp.exp(logits - jnp.max(logits, axis=0, keepdims=True))
    probs = e / jnp.sum(e, axis=0, keepdims=True)
    sel = probs + rb_ref[...]
    n_g, per = N_EXPERT_GROUPS, EXPERTS_PER_GROUP

    def row(a, r):
        return a[r:r + 1, :]

    scores = []
    for gi in range(n_g):
        a, b, c, d = (row(sel, gi * per + r) for r in range(per))
        hi1, lo1 = jnp.maximum(a, b), jnp.minimum(a, b)
        hi2, lo2 = jnp.maximum(c, d), jnp.minimum(c, d)
        scores.append(jnp.maximum(hi1, hi2) + jnp.maximum(jnp.minimum(hi1, hi2), jnp.maximum(lo1, lo2)))
    best = scores[0]
    bg = jnp.zeros_like(best, dtype=jnp.int32)
    for gi in range(1, n_g):
        better = scores[gi] > best
        best = jnp.where(better, scores[gi], best)
        bg = jnp.where(better, gi, bg)
    in_sel = []
    in_prob = []
    for r in range(per):
        s_r = row(sel, r)
        p_r = row(probs, r)
        for gi in range(1, n_g):
            s_r = jnp.where(bg == gi, row(sel, gi * per + r), s_r)
            p_r = jnp.where(bg == gi, row(probs, gi * per + r), p_r)
        in_sel.append(s_r)
        in_prob.append(p_r)

    def argmax_first(vals, exclude):
        bv = None
        for r in range(per):
            v = vals[r] if exclude is None else jnp.where(exclude == r, -jnp.inf, vals[r])
            if bv is None:
                bv, bi = v, jnp.zeros_like(bg)
            else:
                better = v > bv
                bv = jnp.where(better, v, bv)
                bi = jnp.where(better, r, bi)
        return bi

    i1 = argmax_first(in_sel, None)
    i2 = argmax_first(in_sel, i1)

    def pick(vals, idx):
        out = vals[0]
        for r in range(1, per):
            out = jnp.where(idx == r, vals[r], out)
        return out

    p1 = pick(in_prob, i1)
    p2 = pick(in_prob, i2)
    tot = p1 + p2
    ids_ref[...] = jnp.concatenate([bg * per + i1, bg * per + i2], axis=0)
    gts_ref[...] = jnp.concatenate([p1 / tot, p2 / tot], axis=0)


def router(x, g, mods, layer, router_w_pad, router_b, dims, tm_pref=256):
    t, d = x.shape
    n_ctx, seq = dims
    n_e = router_b.shape[0]
    lanes = router_w_pad.shape[1]
    tm = _tile(math.gcd(n_ctx, seq), tm_pref)
    return pl.pallas_call(
        functools.partial(_router_kernel, tm=tm, n_ctx=n_ctx, seq=seq),
        grid=(t // tm,),
        in_specs=[pl.BlockSpec((tm, d), lambda i: (i, 0)),
                  pl.BlockSpec((1, d), lambda i: (0, 0)),
                  pl.BlockSpec((None, None, MOD_ROWS, d), lambda i: (layer, 3, 0, 0)),
                  pl.BlockSpec((None, None, MOD_ROWS, d), lambda i: (layer, 4, 0, 0)),
                  pl.BlockSpec((d, lanes), lambda i: (0, 0)),
                  pl.BlockSpec((n_e, 1), lambda i: (0, 0))],
        out_specs=[pl.BlockSpec((tm, d), lambda i: (i, 0)),
                   pl.BlockSpec((2, tm), lambda i: (0, i)),
                   pl.BlockSpec((2, tm), lambda i: (0, i))],
        out_shape=[jax.ShapeDtypeStruct((t, d), F32),
                   jax.ShapeDtypeStruct((2, t), jnp.int32),
                   jax.ShapeDtypeStruct((2, t), F32)],
        compiler_params=_cparams(1), name="router",
    )(x, g.reshape(1, d), mods, mods, router_w_pad, router_b.reshape(n_e, 1))


def _gather(src_hbm, idx_ref, base, n_rows, dst, sem):
    def copy(r):
        return pltpu.make_async_copy(src_hbm.at[pl.ds(idx_ref[base + r], 1), :], dst.at[pl.ds(r, 1), :], sem)

    def start_row(r, c):
        copy(r).start()
        return c

    def wait_row(r, c):
        copy(r).wait()
        return c

    def start(unroll=8):
        lax.fori_loop(0, n_rows, start_row, 0, unroll=unroll)

    def wait():
        lax.fori_loop(0, n_rows, wait_row, 0, unroll=8)

    return start, wait


def _expert_kernel(te_ref, act_ref, tok_ref, h_hbm, wg_ref, wu_ref, wd_ref, y_ref, xbuf, sems, *, tm):
    i = pl.program_id(0)
    slot = i % 2

    def gather(tile, s):
        return _gather(h_hbm, tok_ref, tile * tm, tm, xbuf.at[s], sems.at[s])

    @pl.when(jnp.logical_and(i == 0, act_ref[0] == 1))
    def _():
        gather(0, 0)[0]()

    @pl.when(act_ref[i] == 1)
    def _():
        gather(i, slot)[1]()
        gather(i + 1, 1 - slot)[0](unroll=True)
        x = xbuf[slot].astype(BF16)
        a = _dot(x, wg_ref[...])
        hid = (a * _sigmoid(a)) * _dot(x, wu_ref[...])
        y_ref[...] = _dot(hid.astype(BF16), wd_ref[...])

    @pl.when(act_ref[i] == 0)
    def _():
        @pl.when(jnp.logical_and(i > 0, act_ref[jnp.maximum(i - 1, 0)] == 1))
        def _():
            gather(i, slot)[1]()
        y_ref[...] = jnp.zeros_like(y_ref)


def expert_ffn(h, tile_expert, tile_active, slot_token, w_gate, w_up, w_down, layer, tm):
    t, d = h.shape
    _, n_e, _, f = w_gate.shape
    n_tiles = tile_expert.shape[0]
    single = pl.Buffered(1)
    grid_spec = pltpu.PrefetchScalarGridSpec(
        num_scalar_prefetch=3,
        grid=(n_tiles,),
        in_specs=[pl.BlockSpec(memory_space=pl.ANY),
                  pl.BlockSpec((None, None, d, f), lambda i, te, act, tok: (layer, te[i], 0, 0), pipeline_mode=single),
                  pl.BlockSpec((None, None, d, f), lambda i, te, act, tok: (layer, te[i], 0, 0), pipeline_mode=single),
                  pl.BlockSpec((None, None, f, d), lambda i, te, act, tok: (layer, te[i], 0, 0), pipeline_mode=single)],
        out_specs=pl.BlockSpec((tm, d), lambda i, te, act, tok: (i, 0)),
        scratch_shapes=[pltpu.VMEM((2, tm, d), F32), pltpu.SemaphoreType.DMA((2,))],
    )
    return pl.pallas_call(
        functools.partial(_expert_kernel, tm=tm),
        grid_spec=grid_spec,
        out_shape=jax.ShapeDtypeStruct((n_tiles * tm, d), F32),
        compiler_params=_cparams(1), name="expert_ffn",
    )(tile_expert, tile_active, slot_token, h, w_gate, w_up, w_down)


def _combine_kernel(slots_ref, x_ref, gate_ref, rw_ref, y_hbm, o_ref, ybuf, sems, *, tm, n_tok, n_ctx, seq):
    i = pl.program_id(0)
    n = pl.num_programs(0)
    slot = i % 2

    def gathers(tile, s):
        return [_gather(y_hbm, slots_ref, k * n_tok + tile * tm, tm, ybuf.at[s, k], sems.at[s]) for k in range(2)]

    @pl.when(i == 0)
    def _():
        for start, _ in gathers(0, 0):
            start()

    @pl.when(i + 1 < n)
    def _():
        for start, _ in gathers(i + 1, 1 - slot):
            start()

    for _, wait in gathers(i, slot):
        wait()
    rid = _row_id(i * tm, n_ctx, seq)
    mixed = rw_ref[:, 0:1] * ybuf[slot, 0] + rw_ref[:, 1:2] * ybuf[slot, 1]
    o_ref[...] = x_ref[...] + gate_ref[pl.ds(rid, 1), :] * mixed


def moe_combine(x, y_sorted, tok_slots, route_w, mods, layer, dims, tm_pref=256):
    t, d = x.shape
    n_ctx, seq = dims
    tm = _tile(math.gcd(n_ctx, seq), tm_pref)
    grid_spec = pltpu.PrefetchScalarGridSpec(
        num_scalar_prefetch=1,
        grid=(t // tm,),
        in_specs=[pl.BlockSpec((tm, d), lambda i, s: (i, 0)),
                  pl.BlockSpec((None, None, MOD_ROWS, d), lambda i, s: (layer, 5, 0, 0)),
                  pl.BlockSpec((tm, 2), lambda i, s: (i, 0)),
                  pl.BlockSpec(memory_space=pl.ANY)],
        out_specs=pl.BlockSpec((tm, d), lambda i, s: (i, 0)),
        scratch_shapes=[pltpu.VMEM((2, 2, tm, d), F32), pltpu.SemaphoreType.DMA((2,))],
    )
    return pl.pallas_call(
        functools.partial(_combine_kernel, tm=tm, n_tok=t, n_ctx=n_ctx, seq=seq),
        grid_spec=grid_spec,
        out_shape=jax.ShapeDtypeStruct((t, d), F32),
        input_output_aliases={1: 0},
        compiler_params=_cparams(1), name="moe_combine",
    )(tok_slots, x, mods, route_w, y_sorted)


def _moe_plan(ids, n_e, tm):
    n_tok = ids.shape[1]
    e_flat = ids.reshape(-1)
    n_pairs = e_flat.shape[0]
    onehot = (e_flat[:, None] == jnp.arange(n_e, dtype=jnp.int32)[None, :]).astype(jnp.int32)
    rank = jnp.take_along_axis(jnp.cumsum(onehot, axis=0), e_flat[:, None], axis=1)[:, 0] - 1
    counts = jnp.sum(onehot, axis=0)
    padded = (counts + tm - 1) // tm * tm
    ends = jnp.cumsum(padded)
    starts = ends - padded
    dest = starts[e_flat] + rank
    n_tiles = n_pairs // tm + n_e + 1
    n_slots = n_tiles * tm
    pair_tok = jnp.arange(n_pairs, dtype=jnp.int32) % n_tok
    slot_token = jnp.zeros((n_slots,), jnp.int32).at[dest].set(pair_tok, unique_indices=True)
    tile_start = jnp.arange(n_tiles, dtype=jnp.int32) * tm
    tile_expert = jnp.minimum(jnp.searchsorted(ends, tile_start, side="right"), n_e - 1).astype(jnp.int32)
    tile_active = (tile_start < ends[-1]).astype(jnp.int32)
    last_expert = tile_expert[jnp.maximum(ends[-1] // tm - 1, 0)]
    tile_expert = jnp.where(tile_active == 1, tile_expert, last_expert)
    return tile_expert, tile_active, slot_token, dest.astype(jnp.int32)


def moe_layer(x, norm_g, mods, layer, router_w_pad, router_b, w_gate, w_up, w_down, dims, tm_pref=256):
    n_e = w_gate.shape[1]
    h, ids, gates = router(x, norm_g, mods, layer, router_w_pad, router_b, dims)
    tm = _tile(2 * x.shape[0], tm_pref)
    tile_expert, tile_active, slot_token, dest = _moe_plan(ids, n_e, tm)
    y_sorted = expert_ffn(h, tile_expert, tile_active, slot_token, w_gate, w_up, w_down, layer, tm)
    return moe_combine(x, y_sorted, dest, gates.T, mods, layer, dims)


def fourier_rwkv_layer(x, mods, layer, j, norm_g, w_in, w_in_b, w_out_b, shift_mu, w0, w_up, a0, a_up, g_up,
                       k_k, k_a, r_k, lnx_g, lnx_b, batch, dims, fnet_tabs):
    n_ctx, ctx_len, seq = dims
    rw = w0.shape[1]
    fw = w_in.shape[1] - 3 * rw - w_up.shape[1] - a_up.shape[1] - g_up.shape[0]
    dl, il, gl = w_up.shape[1], a_up.shape[1], g_up.shape[0]
    lora_pad = 128
    rdims = (n_ctx, seq)

    def pad_cols(a, n):
        return jnp.pad(a, ((0, 0), (0, n - a.shape[1])))

    o_lo = fw + 3 * rw
    w_lo = jnp.concatenate([pad_cols(w_in[:, o_lo:o_lo + dl], lora_pad),
                            pad_cols(w_in[:, o_lo + dl:o_lo + dl + il], lora_pad),
                            w_in[:, o_lo + dl + il:]], axis=1).astype(BF16)
    mu = shift_mu.reshape(1, -1)
    mu_lo = jnp.concatenate([pad_cols(mu[:, 3 * rw:3 * rw + dl], lora_pad),
                             pad_cols(mu[:, 3 * rw + dl:3 * rw + dl + il], lora_pad),
                             mu[:, 3 * rw + dl + il:]], axis=1)
    w_up_p = jnp.pad(w_up, ((0, 0), (0, lora_pad - dl), (0, 0))).astype(BF16)
    a_up_p = jnp.pad(a_up, ((0, 0), (0, lora_pad - il), (0, 0))).astype(BF16)

    h = norm_mod(x, norm_g, mods, layer, 0, 1, rdims)
    f = matmul(h, w_in_b, BF16, rdims, cols=(0, fw), w_index=j)
    prkv = matmul(h, w_in_b, BF16, rdims, cols=(fw, 3 * rw), w_index=j)
    plo = matmul(h, w_lo, F32, rdims)

    r, v, kk, lw, kd, bd, g, bv = rwkv_terms(
        prkv, plo, mu[:, :3 * rw], mu_lo, w0, w_up_p, a0, a_up_p, g_up.astype(BF16),
        k_k.reshape(1, rw), k_a.reshape(1, rw), r_k.reshape(1, rw), dims, lora_pad)
    o2 = rwkv_scan(r, v, kk, lw, kd, bd, batch, dims)
    y_rw = rwkv_readout(o2, bv, g, lnx_g, lnx_b)

    cos_c, sin_c, cos_x, sin_x = fnet_tabs
    gc = fw // FNET_GROUPS
    y1c, y2c = fnet_channel_dft(f, cos_c, sin_c, 0, n_ctx)
    y1l, y2l = fnet_channel_dft(f, cos_c, sin_c, n_ctx, x.shape[0] - n_ctx)
    f_ctx = fnet_sequence_dft(y1c, y2c, 0, batch, ctx_len, cos_x, sin_x, gc)
    f_lat = fnet_sequence_fft(y1l, y2l, batch, seq, gc)
    f_mix = jnp.concatenate([f_ctx, f_lat], axis=0)
    return matmul_gated_residual([f_mix, y_rw], w_out_b, j, x, mods, layer, 2, rdims)


def attention_layer(x, mods, layer, j, norm_g, w_qkv_b, w_o_b, sink, batch, dims, rope_tabs):
    n_ctx, ctx_len, seq = dims
    q_cols = w_o_b.shape[1]
    kv_cols = (w_qkv_b.shape[2] - q_cols) // 2
    cos_t, sin_t = rope_tabs
    h = norm_mod(x, norm_g, mods, layer, 0, 1, (n_ctx, seq))
    qkv = matmul(h, w_qkv_b, BF16, (n_ctx, seq), rope=(q_cols + kv_cols, cos_t, sin_t), w_index=j,
                 tn_pref=cos_t.shape[1])
    att = attention(qkv, sink, batch, dims, q_cols, kv_cols)
    return matmul_gated_residual([att], w_o_b, j, x, mods, layer, 2, (n_ctx, seq))


def kernel(x, c, ctx, c_ctx, ada_down, ada_up, ada_bias, norm1_g, norm2_g, final_g, mix_w_in, mix_w_out, shift_mu, decay_w0, decay_up, iclr_a0, iclr_up, gate_up, k_k, k_a, r_k, lnx_g, lnx_b, att_w_qkv, att_w_o, att_sink, router_w, router_b, exp_w_gate, exp_w_up, exp_w_down):
    batch, seq, d = x.shape
    ctx_len = ctx.shape[1]
    depth = ada_down.shape[0]
    n_ctx = batch * ctx_len
    assert batch + 1 <= MOD_ROWS
    dims = (n_ctx, ctx_len, seq)
    xs = jnp.concatenate([ctx.reshape(n_ctx, d), x.reshape(batch * seq, d)], axis=0)

    cvec = jnp.concatenate([c_ctx[None, :], c, jnp.zeros((MOD_ROWS - 1 - batch, d), F32)], axis=0)
    mods = adaln_all(cvec, ada_down, ada_up, ada_bias)
    mods = mods.reshape(depth, MOD_ROWS, N_MOD, d).transpose(0, 2, 1, 3)

    fw = mix_w_in.shape[2] - 3 * decay_w0.shape[2] - decay_up.shape[2] - iclr_up.shape[2] - gate_up.shape[1]
    gc = fw // FNET_GROUPS
    fnet_tabs = _dft_tables(gc) + _dft_tables(ctx_len)
    kv_cols = (att_w_qkv.shape[2] - att_w_o.shape[1]) // 2
    rope_tabs = _rope_tables(seq, min(512, kv_cols))
    router_w_pad = jnp.pad(router_w, ((0, 0), (0, 128 - router_w.shape[1])))
    mix_w_in_b = mix_w_in.astype(BF16)
    mix_w_out_b = mix_w_out.astype(BF16)
    att_w_qkv_b = att_w_qkv.astype(BF16)
    att_w_o_b = att_w_o.astype(BF16)
    exp_w_gate_b = exp_w_gate.astype(BF16)
    exp_w_up_b = exp_w_up.astype(BF16)
    exp_w_down_b = exp_w_down.astype(BF16)

    for layer in range(depth):
        j = layer // 2
        if layer % 2 == 0:
            xs = fourier_rwkv_layer(xs, mods, layer, j, norm1_g[layer], mix_w_in[j], mix_w_in_b, mix_w_out_b, shift_mu[j],
                                    decay_w0[j], decay_up[j], iclr_a0[j], iclr_up[j], gate_up[j],
                                    k_k[j], k_a[j], r_k[j], lnx_g[j], lnx_b[j], batch, dims, fnet_tabs)
        else:
            xs = attention_layer(xs, mods, layer, j, norm1_g[layer], att_w_qkv_b, att_w_o_b, att_sink[j],
                                 batch, dims, rope_tabs)
        xs = moe_layer(xs, norm2_g[layer], mods, layer, router_w_pad, router_b,
                       exp_w_gate_b, exp_w_up_b, exp_w_down_b, (n_ctx, seq))
    return final_norm(xs, final_g, n_ctx).reshape(batch, seq, d)
```

```python
import functools
import math

import jax
import jax.numpy as jnp
from jax import lax
from jax.experimental import pallas as pl
from jax.experimental.pallas import tpu as pltpu

F32 = jnp.float32
BF16 = jnp.bfloat16
HIGHEST = lax.Precision.HIGHEST

GRID_W = 64
NORM_EPS = 1e-6
GN_EPS = 64e-5
L2_EPS = 1e-12
N_MOD = 6
FNET_GROUPS = 4
HEAD = 64
ATT_GROUP = 8
ATT_SCALE = 0.125
WINDOW = 128
QBLOCK = 128
ROPE_THETA = 10000.0
N_EXPERT_GROUPS = 4
EXPERTS_PER_GROUP = 4
CHUNK = 64
MOD_ROWS = 8

VMEM_LIMIT = 56 * 1024 * 1024


def _cparams(n_axes):
    return pltpu.CompilerParams(dimension_semantics=("arbitrary",) * n_axes,
                                vmem_limit_bytes=VMEM_LIMIT)


def _tile(n, pref):
    t = min(n, pref)
    while n % t:
        t //= 2
    return t


def _row_id(r0, n_ctx, seq):
    return jnp.where(r0 < n_ctx, 0, 1 + (r0 - n_ctx) // seq)


def _dot(a, b):
    return jnp.dot(a, b, preferred_element_type=F32)


def _dot_hi(a, b):
    return jnp.dot(a, b, preferred_element_type=F32, precision=HIGHEST)


def _dot_t0(a, b):
    return lax.dot_general(a, b, (((0,), (0,)), ((), ())), preferred_element_type=F32)


def _dot_t1(a, b):
    return lax.dot_general(a, b, (((1,), (1,)), ((), ())), preferred_element_type=F32)


def _dot_hi_t1(a, b):
    return lax.dot_general(a, b, (((1,), (1,)), ((), ())), preferred_element_type=F32, precision=HIGHEST)


def _sigmoid(x):
    return 1.0 / (1.0 + jnp.exp(-x))


def _norm_mod(x, g, sh, sc):
    ms = jnp.mean(x * x, axis=-1, keepdims=True)
    y = x * lax.rsqrt(ms + NORM_EPS) * g
    return y * (1.0 + sc) + sh


def _adaln_kernel(cv_ref, down_ref, up_ref, bias_ref, o_ref, t_scr):
    @pl.when(pl.program_id(1) == 0)
    def _():
        cv = cv_ref[...]
        s = cv * _sigmoid(cv)
        t_scr[...] = _dot(s.astype(BF16), down_ref[...].astype(BF16)).astype(BF16)
    o_ref[...] = _dot(t_scr[...], up_ref[...].astype(BF16)) + bias_ref[...]


def adaln_all(cvec, ada_down, ada_up, ada_bias):
    depth, d, rank = ada_down.shape
    n = ada_up.shape[2]
    tn = _tile(n, 2048)
    bias = ada_bias.reshape(depth, 1, n)
    return pl.pallas_call(
        _adaln_kernel,
        grid=(depth, n // tn),
        in_specs=[pl.BlockSpec((MOD_ROWS, d), lambda l, j: (0, 0)),
                  pl.BlockSpec((None, d, rank), lambda l, j: (l, 0, 0)),
                  pl.BlockSpec((None, rank, tn), lambda l, j: (l, 0, j)),
                  pl.BlockSpec((None, 1, tn), lambda l, j: (l, 0, j))],
        out_specs=pl.BlockSpec((None, MOD_ROWS, tn), lambda l, j: (l, 0, j)),
        out_shape=jax.ShapeDtypeStruct((depth, MOD_ROWS, n), F32),
        scratch_shapes=[pltpu.VMEM((MOD_ROWS, rank), BF16)],
        compiler_params=_cparams(2), name="adaln",
    )(cvec, ada_down, ada_up, bias)


def _norm_mod_kernel(x_ref, g_ref, sh_ref, sc_ref, h_ref, *, tm, n_ctx, seq):
    rid = _row_id(pl.program_id(0) * tm, n_ctx, seq)
    h = _norm_mod(x_ref[...], g_ref[...], sh_ref[pl.ds(rid, 1), :], sc_ref[pl.ds(rid, 1), :])
    h_ref[...] = h.astype(h_ref.dtype)


def norm_mod(x, g, mods, layer, k_shift, k_scale, dims, tm_pref=512):
    t, d = x.shape
    n_ctx, seq = dims
    tm = _tile(math.gcd(n_ctx, seq), tm_pref)
    return pl.pallas_call(
        functools.partial(_norm_mod_kernel, tm=tm, n_ctx=n_ctx, seq=seq),
        grid=(t // tm,),
        in_specs=[pl.BlockSpec((tm, d), lambda i: (i, 0)),
                  pl.BlockSpec((1, d), lambda i: (0, 0)),
                  pl.BlockSpec((None, None, MOD_ROWS, d), lambda i: (layer, k_shift, 0, 0)),
                  pl.BlockSpec((None, None, MOD_ROWS, d), lambda i: (layer, k_scale, 0, 0))],
        out_specs=pl.BlockSpec((tm, d), lambda i: (i, 0)),
        out_shape=jax.ShapeDtypeStruct((t, d), BF16),
        compiler_params=_cparams(1), name="norm_mod",
    )(x, g.reshape(1, d), mods, mods)


def _matmul_kernel(*refs, tm, n_ctx, rope_cols):
    if rope_cols:
        a_ref, w_ref, cos_ref, sin_ref, o_ref = refs
    else:
        a_ref, w_ref, o_ref = refs
    r0 = pl.program_id(0) * tm
    j = pl.program_id(1)
    acc = _dot(a_ref[...], w_ref[...])
    if rope_cols:
        tn = acc.shape[1]
        do_rope = jnp.logical_and(r0 >= n_ctx, j * tn < rope_cols)

        @pl.when(do_rope)
        def _():
            lane = lax.broadcasted_iota(jnp.int32, acc.shape, 1)
            swapped = jnp.where(lane % 32 < 16, pltpu.roll(acc, tn - 16, 1), pltpu.roll(acc, 16, 1))
            o_ref[...] = (acc * cos_ref[...] + swapped * sin_ref[...]).astype(o_ref.dtype)

        @pl.when(jnp.logical_not(do_rope))
        def _():
            o_ref[...] = acc.astype(o_ref.dtype)
    else:
        o_ref[...] = acc.astype(o_ref.dtype)


def matmul(a, w, out_dtype, dims, rope=None, cols=None, w_index=None, tm_pref=1024, tn_pref=512):
    t, d = a.shape
    col0, n = cols if cols is not None else (0, w.shape[-1])
    n_ctx, seq = dims
    tm = _tile(math.gcd(n_ctx, seq), tm_pref)
    tn = _tile(math.gcd(n, col0) if col0 else n, tn_pref)
    jb = col0 // tn
    if w_index is None:
        w_spec = pl.BlockSpec((d, tn), lambda i, j: (0, jb + j))
    else:
        w_spec = pl.BlockSpec((None, d, tn), lambda i, j: (w_index, 0, jb + j))
    in_specs = [pl.BlockSpec((tm, d), lambda i, j: (i, 0)), w_spec]
    args = [a, w]
    rope_cols = 0
    if rope is not None:
        rope_cols, cos_t, sin_t = rope
        assert rope_cols % tn == 0 and cos_t.shape == (seq, tn)
        pos_map = lambda i, j: (jnp.maximum(i * tm - n_ctx, 0) % seq // tm, 0)
        in_specs += [pl.BlockSpec((tm, tn), pos_map), pl.BlockSpec((tm, tn), pos_map)]
        args += [cos_t, sin_t]
    return pl.pallas_call(
        functools.partial(_matmul_kernel, tm=tm, n_ctx=n_ctx, rope_cols=rope_cols),
        grid=(t // tm, n // tn),
        in_specs=in_specs,
        out_specs=pl.BlockSpec((tm, tn), lambda i, j: (i, j)),
        out_shape=jax.ShapeDtypeStruct((t, n), out_dtype),
        compiler_params=_cparams(2), name="matmul",
    )(*args)


def _mmres_kernel(*refs, n_a, tm, n_ctx, seq):
    a_refs = refs[:n_a]
    w_refs = refs[n_a:2 * n_a]
    x_ref, gate_ref, o_ref = refs[2 * n_a:]
    acc = _dot(a_refs[0][...], w_refs[0][...])
    for a_ref, w_ref in zip(a_refs[1:], w_refs[1:]):
        acc += _dot(a_ref[...], w_ref[...])
    rid = _row_id(pl.program_id(0) * tm, n_ctx, seq)
    o_ref[...] = x_ref[...] + gate_ref[pl.ds(rid, 1), :] * acc


def matmul_gated_residual(a_list, w_stack, w_index, x, mods, layer, k_gate, dims, tm_pref=1024, tn_pref=512):
    t, d = x.shape
    n_ctx, seq = dims
    tm = _tile(math.gcd(n_ctx, seq), tm_pref)
    tn = _tile(d, tn_pref)
    n_a = len(a_list)
    assert len({a.shape[1] for a in a_list}) == 1 and n_a * a_list[0].shape[1] == w_stack.shape[1]
    w_list = [w_stack] * n_a
    in_specs = [pl.BlockSpec((tm, a.shape[1]), lambda i, j: (i, 0)) for a in a_list]
    in_specs += [pl.BlockSpec((None, a.shape[1], tn), lambda i, j, p=p: (w_index, p, j)) for p, a in enumerate(a_list)]
    in_specs += [pl.BlockSpec((tm, tn), lambda i, j: (i, j)),
                 pl.BlockSpec((None, None, MOD_ROWS, tn), lambda i, j: (layer, k_gate, 0, j))]
    return pl.pallas_call(
        functools.partial(_mmres_kernel, n_a=n_a, tm=tm, n_ctx=n_ctx, seq=seq),
        grid=(t // tm, d // tn),
        in_specs=in_specs,
        out_specs=pl.BlockSpec((tm, tn), lambda i, j: (i, j)),
        out_shape=jax.ShapeDtypeStruct((t, d), F32),
        input_output_aliases={2 * n_a: 0},
        compiler_params=_cparams(2), name="matmul_gated_residual",
    )(*a_list, *w_list, x, mods)


def _final_norm_kernel(x_ref, g_ref, o_ref):
    x = x_ref[...]
    ms = jnp.mean(x * x, axis=-1, keepdims=True)
    o_ref[...] = x * lax.rsqrt(ms + NORM_EPS) * g_ref[...]


def final_norm(x, g, n_ctx, tm_pref=512):
    t, d = x.shape
    tm = _tile(math.gcd(n_ctx, t - n_ctx), tm_pref)
    off = n_ctx // tm
    return pl.pallas_call(
        _final_norm_kernel,
        grid=((t - n_ctx) // tm,),
        in_specs=[pl.BlockSpec((tm, d), lambda i: (i + off, 0)),
                  pl.BlockSpec((1, d), lambda i: (0, 0))],
        out_specs=pl.BlockSpec((tm, d), lambda i: (i, 0)),
        out_shape=jax.ShapeDtypeStruct((t - n_ctx, d), F32),
        compiler_params=_cparams(1), name="final_norm",
    )(x, g.reshape(1, d))


def _dft_tables(n, split=64):
    k = jnp.arange(n, dtype=jnp.int32)

    def cs(rows, period):
        ang = ((rows[:, None] * k[None, :]) % period).astype(F32) * (2.0 * math.pi / period)
        return jnp.cos(ang), jnp.sin(ang)

    if n % split or n <= split:
        c, s = cs(k, n)
        return c.astype(BF16), s.astype(BF16)
    ca, sa = cs(jnp.arange(n // split, dtype=jnp.int32), n // split)
    cb, sb = cs(jnp.arange(split, dtype=jnp.int32), n)
    ca, sa, cb, sb = ca[:, None, :], sa[:, None, :], cb[None, :, :], sb[None, :, :]
    return ((ca * cb - sa * sb).reshape(n, n).astype(BF16), (sa * cb + ca * sb).reshape(n, n).astype(BF16))


def _fnet_ch_kernel(x_ref, c_ref, s_ref, y1_ref, y2_ref):
    x = x_ref[...]
    y1_ref[...] = _dot(x, c_ref[...]).astype(y1_ref.dtype)
    y2_ref[...] = _dot(x, s_ref[...]).astype(y2_ref.dtype)


def fnet_channel_dft(f, cos_c, sin_c, row0, n_rows, tm_pref=1024):
    width = f.shape[1]
    gc = width // FNET_GROUPS
    tm = _tile(math.gcd(row0, n_rows) if row0 else n_rows, tm_pref)
    ib = row0 // tm
    tab = pl.BlockSpec((gc, gc), lambda i, g: (0, 0))
    out = pl.BlockSpec((tm, gc), lambda i, g: (i, g))
    return pl.pallas_call(
        _fnet_ch_kernel,
        grid=(n_rows // tm, FNET_GROUPS),
        in_specs=[pl.BlockSpec((tm, gc), lambda i, g: (ib + i, g)), tab, tab],
        out_specs=[out, out],
        out_shape=[jax.ShapeDtypeStruct((n_rows, width), BF16)] * 2,
        compiler_params=_cparams(2), name="fnet_channel_dft",
    )(f, cos_c, sin_c)


def _fnet_fft_a_kernel(y1_ref, y2_ref, m1_ref, m2_ref, ct_ref, st_ref, br_ref, bi_ref):
    n1 = y1_ref.shape[0]
    a = _dot(m1_ref[...], y1_ref[...]) + _dot(m2_ref[...], y2_ref[...])
    nb, _, width = br_ref.shape
    for g in range(nb):
        ar = a[:n1, g * width:(g + 1) * width]
        ai = a[n1:, g * width:(g + 1) * width]
        ct, st = ct_ref[g], st_ref[g]
        br_ref[g] = (ar * ct + ai * st).astype(br_ref.dtype)
        bi_ref[g] = (ai * ct - ar * st).astype(bi_ref.dtype)


def _fnet_fft_b_kernel(br_ref, bi_ref, c_ref, s_ref, o_ref, *, scale):
    o_ref[...] = ((_dot(c_ref[...], br_ref[...]) + _dot(s_ref[...], bi_ref[...])) * scale).astype(o_ref.dtype)


def fnet_sequence_fft(y1, y2, n_seq, length, gc, n2=64):
    width = y1.shape[1]
    n1 = length // n2

    def cs(rows, cols, period):
        ang = ((rows[:, None] * cols[None, :]) % period).astype(F32) * (2.0 * math.pi / period)
        return jnp.cos(ang), jnp.sin(ang)

    i1 = jnp.arange(n1, dtype=jnp.int32)
    i2 = jnp.arange(n2, dtype=jnp.int32)
    c1, s1 = cs(i1, i1, n1)
    c2, s2 = cs(i2, i2, n2)
    ct, st = cs(i2, i1, length)
    m1 = jnp.concatenate([c1, -s1], axis=0).astype(BF16)
    m2 = jnp.concatenate([-s1, -c1], axis=0).astype(BF16)
    nb = 4
    y_spec = pl.BlockSpec((n1, nb * width), lambda s, j: (s, j))
    m_spec = pl.BlockSpec((2 * n1, n1), lambda s, j: (0, 0))
    t_spec = pl.BlockSpec((nb, n1, 1), lambda s, j: (j, 0, 0))
    b_spec = pl.BlockSpec((nb, n1, width), lambda s, j: (s * (n2 // nb) + j, 0, 0))
    b_shape = jax.ShapeDtypeStruct((n_seq * n2, n1, width), BF16)
    br, bi = pl.pallas_call(
        _fnet_fft_a_kernel,
        grid=(n_seq, n2 // nb),
        in_specs=[y_spec, y_spec, m_spec, m_spec, t_spec, t_spec],
        out_specs=[b_spec, b_spec],
        out_shape=[b_shape, b_shape],
        compiler_params=_cparams(2), name="fnet_fft_a",
    )(y1.reshape(n_seq * n1, n2 * width), y2.reshape(n_seq * n1, n2 * width), m1, m2,
      ct.reshape(n2, n1, 1), st.reshape(n2, n1, 1))
    tc = _tile(n1 * width, 8192)
    v_spec = pl.BlockSpec((n2, tc), lambda s, j: (s, j))
    w_spec = pl.BlockSpec((n2, n2), lambda s, j: (0, 0))
    out = pl.pallas_call(
        functools.partial(_fnet_fft_b_kernel, scale=1.0 / math.sqrt(length * gc)),
        grid=(n_seq, n1 * width // tc),
        in_specs=[v_spec, v_spec, w_spec, w_spec],
        out_specs=v_spec,
        out_shape=jax.ShapeDtypeStruct((n_seq * n2, n1 * width), BF16),
        compiler_params=_cparams(2), name="fnet_fft_b",
    )(br.reshape(n_seq * n2, n1 * width), bi.reshape(n_seq * n2, n1 * width), c2.astype(BF16), s2.astype(BF16))
    return out.reshape(n_seq * length, width)


def _fnet_seq_kernel(c_ref, s_ref, y1_ref, y2_ref, o_ref, acc_ref, *, scale):
    k = pl.program_id(2)

    @pl.when(k == 0)
    def _():
        acc_ref[...] = jnp.zeros_like(acc_ref)

    acc_ref[...] += _dot(c_ref[...], y1_ref[...]) - _dot(s_ref[...], y2_ref[...])

    @pl.when(k == pl.num_programs(2) - 1)
    def _():
        o_ref[...] = (acc_ref[...] * scale).astype(o_ref.dtype)


def fnet_sequence_dft(y1, y2, row0, n_seq, length, cos_l, sin_l, gc, t_pref=512):
    width = y1.shape[1]
    tm = _tile(length, t_pref)
    tk = _tile(math.gcd(length, row0) if row0 else length, t_pref)
    nb = length // tm
    nk = length // tk
    scale = 1.0 / math.sqrt(length * gc)
    rb0 = row0 // tk
    y_spec = pl.BlockSpec((tk, width), lambda b, i, k: (rb0 + b * nk + k, 0))
    return pl.pallas_call(
        functools.partial(_fnet_seq_kernel, scale=scale),
        grid=(n_seq, nb, nk),
        in_specs=[pl.BlockSpec((tm, tk), lambda b, i, k: (i, k)),
                  pl.BlockSpec((tm, tk), lambda b, i, k: (i, k)),
                  y_spec, y_spec],
        out_specs=pl.BlockSpec((tm, width), lambda b, i, k: (b * nb + i, 0)),
        out_shape=jax.ShapeDtypeStruct((n_seq * length, width), BF16),
        scratch_shapes=[pltpu.VMEM((tm, width), F32)],
        compiler_params=_cparams(3), name="fnet_sequence_dft",
    )(cos_l, sin_l, y1, y2)


def _head_sum(x, ones_bd):
    hi = x.astype(BF16)
    lo = (x - hi.astype(F32)).astype(BF16)
    return _dot(hi, ones_bd) + _dot(lo, ones_bd)


def _shift(x, prev_row, next_row, mu, pos, length):
    rows = x.shape[0]
    ridx = lax.broadcasted_iota(jnp.int32, (rows, 1), 0)
    prev = jnp.where(ridx == 0, prev_row, pltpu.roll(x, 1, 0))
    nxt = jnp.where(ridx == rows - 1, next_row, pltpu.roll(x, rows - 1, 0))
    prev = jnp.where(pos == 0, 0.0, prev)
    nxt = jnp.where(pos == length - 1, 0.0, nxt)
    return x + mu * (0.5 * (prev + nxt) - x)


def _rwkv_terms_kernel(r_ref, k_ref, v_ref, rp_ref, kp_ref, vp_ref, rn_ref, kn_ref, vn_ref,
                       lo_ref, lop_ref, lon_ref, mur_ref, muk_ref, muv_ref, mulo_ref,
                       w0_ref, wup_ref, a0_ref, aup_ref, gup_ref, kk_ref, ka_ref, rk_ref, ones_ref,
                       r_o, v_o, kk_o, lw_o, kd_o, bd_o, g_o, bv_o, *, tm, n_ctx, ctx_len, seq, lora_pad):
    i = pl.program_id(0)
    r0 = i * tm
    ridx = lax.broadcasted_iota(jnp.int32, (tm, 1), 0) + r0
    in_ctx = r0 < n_ctx
    length = jnp.where(in_ctx, ctx_len, seq)
    pos = jnp.where(in_ctx, ridx % ctx_len, (ridx - n_ctx) % seq)

    def sh(ref, pref, nref, mu_ref):
        last = pref.shape[0] - 1
        return _shift(ref[...].astype(F32), pref[last:last + 1, :].astype(F32), nref[0:1, :].astype(F32),
                      mu_ref[...], pos, length)

    r = sh(r_ref, rp_ref, rn_ref, mur_ref)
    k = sh(k_ref, kp_ref, kn_ref, muk_ref)
    v = sh(v_ref, vp_ref, vn_ref, muv_ref)
    lo = sh(lo_ref, lop_ref, lon_ref, mulo_ref)
    w_in = jnp.tanh(lo[:, :lora_pad]).astype(BF16)
    a_in = lo[:, lora_pad:2 * lora_pad].astype(BF16)
    g_in = _sigmoid(lo[:, 2 * lora_pad:]).astype(BF16)
    ones_bd = ones_ref[...]

    kk = k * kk_ref[...]
    kk = kk * lax.rsqrt(_head_sum(kk * kk, ones_bd) + L2_EPS)
    ksum = jnp.zeros_like(k)
    for d in range(2):
        w_logit = w0_ref[d:d + 1, :] + _dot(w_in, wup_ref[d])
        lw_o[d] = -math.exp(-0.5) * _sigmoid(w_logit)
        a = _sigmoid(a0_ref[d:d + 1, :] + _dot(a_in, aup_ref[d]))
        k_d = k * (1.0 + (a - 1.0) * ka_ref[...])
        kd_o[d] = k_d.astype(kd_o.dtype)
        bd_o[d] = (kk * a).astype(bd_o.dtype)
        ksum = ksum + k_d
    g_o[...] = _dot(g_in, gup_ref[...]).astype(g_o.dtype)
    bv_o[...] = (_head_sum(r * ksum * rk_ref[...], ones_bd) * v).astype(bv_o.dtype)
    r_o[...] = r.astype(r_o.dtype)
    v_o[...] = v.astype(v_o.dtype)
    kk_o[...] = kk.astype(kk_o.dtype)


def rwkv_terms(prkv, plo, mu_rkv, mu_lo, w0, w_up, a0, a_up, g_up, k_k, k_a, r_k, dims, lora_pad, tm_pref=256):
    t = prkv.shape[0]
    rw = prkv.shape[1] // 3
    n_ctx, ctx_len, seq = dims
    tm = _tile(math.gcd(n_ctx, seq), tm_pref)
    tn = _tile(rw, 512)
    nj = rw // tn
    lo_w = plo.shape[1]
    hb = tm // 8
    last8 = t // 8 - 1
    hr = 8 * (4 // prkv.dtype.itemsize)
    hbr = tm // hr
    lastr = t // hr - 1

    def main(c):
        return pl.BlockSpec((tm, tn), lambda i, j: (i, c * nj + j))

    def prev(c):
        return pl.BlockSpec((hr, tn), lambda i, j: (jnp.maximum(i * hbr - 1, 0), c * nj + j))

    def nxt(c):
        return pl.BlockSpec((hr, tn), lambda i, j: (jnp.minimum((i + 1) * hbr, lastr), c * nj + j))

    def vec(c=0):
        return pl.BlockSpec((1, tn), lambda i, j: (0, c * nj + j))

    ones_bd = (jnp.arange(tn)[:, None] // HEAD == jnp.arange(tn)[None, :] // HEAD).astype(BF16)
    out_tok = pl.BlockSpec((tm, tn), lambda i, j: (i, j))
    out_dir = pl.BlockSpec((2, tm, tn), lambda i, j: (0, i, j))
    tok = jax.ShapeDtypeStruct((t, rw), BF16)
    tok2 = jax.ShapeDtypeStruct((2, t, rw), BF16)
    lw2 = jax.ShapeDtypeStruct((2, t, rw), F32)
    return pl.pallas_call(
        functools.partial(_rwkv_terms_kernel, tm=tm, n_ctx=n_ctx, ctx_len=ctx_len, seq=seq, lora_pad=lora_pad),
        grid=(t // tm, nj),
        in_specs=[main(0), main(1), main(2), prev(0), prev(1), prev(2), nxt(0), nxt(1), nxt(2),
                  pl.BlockSpec((tm, lo_w), lambda i, j: (i, 0)),
                  pl.BlockSpec((8, lo_w), lambda i, j: (jnp.maximum(i * hb - 1, 0), 0)),
                  pl.BlockSpec((8, lo_w), lambda i, j: (jnp.minimum((i + 1) * hb, last8), 0)),
                  vec(0), vec(1), vec(2),
                  pl.BlockSpec((1, lo_w), lambda i, j: (0, 0)),
                  pl.BlockSpec((2, tn), lambda i, j: (0, j)),
                  pl.BlockSpec((2, lora_pad, tn), lambda i, j: (0, 0, j)),
                  pl.BlockSpec((2, tn), lambda i, j: (0, j)),
                  pl.BlockSpec((2, lora_pad, tn), lambda i, j: (0, 0, j)),
                  pl.BlockSpec((g_up.shape[0], tn), lambda i, j: (0, j)),
                  vec(), vec(), vec(),
                  pl.BlockSpec((tn, tn), lambda i, j: (0, 0))],
        out_specs=[out_tok, out_tok, out_tok, out_dir, out_dir, out_dir, out_tok, out_tok],
        out_shape=[tok, tok, tok, lw2, tok2, tok2, tok, tok],
        compiler_params=_cparams(2), name="rwkv_terms",
    )(prkv, prkv, prkv, prkv, prkv, prkv, prkv, prkv, prkv, plo, plo, plo,
      mu_rkv, mu_rkv, mu_rkv, mu_lo, w0, w_up, a0, a_up, g_up, k_k, k_a, r_k, ones_bd)


def _rwkv_scan_kernel(r_ref, v_ref, kk_ref, lw_ref, kd_ref, bd_ref, o_ref,
                      s_scr, rt_s, at_s, kt_s, bt_s, bh_s, kh_s, vb_s, pd_s, *, n_heads, group):
    d = pl.program_id(1)
    step = pl.program_id(2)
    c = CHUNK

    @pl.when(step == 0)
    def _():
        s_scr[...] = jnp.zeros_like(s_scr)

    row = lax.broadcasted_iota(jnp.int32, (c, c), 0)
    col = lax.broadcasted_iota(jnp.int32, (c, c), 1)
    fwd = d == 0
    ahead = (row - col) * jnp.where(fwd, 1, -1)
    incl = ahead >= 0
    strict = ahead > 0
    eye = row == col
    eye_f = jnp.where(eye, 1.0, 0.0)

    tri = jnp.where(incl, 1.0, 0.0).astype(BF16)
    lw = lw_ref[...]
    lw_hi = lw.astype(BF16)
    lw_lo = (lw - lw_hi.astype(F32)).astype(BF16)
    l_inc = _dot(tri, lw_hi) + _dot(tri, lw_lo)
    l_tot = jnp.where(fwd, l_inc[c - 1:c, :], l_inc[0:1, :])
    e_neg = jnp.exp(-l_inc)
    e_rem = jnp.exp(l_tot - l_inc)
    kd = kd_ref[...]
    bd = bd_ref[...]
    rt_s[...] = (r_ref[...] * jnp.exp(l_inc)).astype(BF16)
    at_s[...] = (-kk_ref[...] * jnp.exp(l_inc - lw)).astype(BF16)
    kt_s[...] = (kd * e_neg).astype(BF16)
    bt_s[...] = (bd * e_neg).astype(BF16)
    bh_s[...] = (bd * e_rem).astype(BF16)
    kh_s[...] = (kd * e_rem).astype(BF16)
    vb_s[...] = v_ref[...].astype(BF16)
    pd_s[...] = jnp.exp(l_tot)

    gw = group * HEAD

    def group_body(gi, carry):
        sl = pl.ds(pl.multiple_of(gi * gw, gw), gw)
        rt_g, at_g, kt_g, bt_g, bh_g, kh_g, v_g = (s[:, sl] for s in (rt_s, at_s, kt_s, bt_s, bh_s, kh_s, vb_s))
        pd_g = pd_s[:, sl]
        heads = range(group)

        def hd(a, q):
            return a[:, q * HEAD:(q + 1) * HEAD]

        lhs = [jnp.concatenate([hd(at_g, q), hd(rt_g, q)], axis=0) for q in heads]
        gb = [_dot_t1(lhs[q], hd(bt_g, q)) for q in heads]
        gk = [_dot_t1(lhs[q], hd(kt_g, q)) for q in heads]
        a_ab = [jnp.where(strict, gb[q][:c], 0.0) for q in heads]
        a_rb = [jnp.where(incl, gb[q][c:], 0.0).astype(BF16) for q in heads]
        a_k = [jnp.concatenate([jnp.where(strict, gk[q][:c], 0.0), jnp.where(incl, gk[q][c:], 0.0)],
                               axis=0).astype(BF16) for q in heads]
        vv = [_dot(a_k[q], hd(v_g, q)) for q in heads]
        kv = [_dot_t0(hd(kh_g, q), hd(v_g, q)) for q in heads]
        tinv = [eye_f + a for a in a_ab]
        x = [_dot(a.astype(BF16), a.astype(BF16)).astype(BF16) for a in a_ab]
        n_sq = int(math.log2(c)) - 1
        for it in range(n_sq):
            if it < n_sq - 1:
                both = [_dot(jnp.concatenate([tinv[q].astype(BF16), x[q]], axis=0), x[q]) for q in heads]
                tinv = [tinv[q] + both[q][:c] for q in heads]
                x = [both[q][c:].astype(BF16) for q in heads]
            else:
                tinv = [tinv[q] + _dot(tinv[q].astype(BF16), x[q]) for q in heads]
        tb = [t.astype(BF16) for t in tinv]
        wt = [_dot(tb[q], hd(at_g, q)).astype(BF16) for q in heads]
        u0 = [_dot(tb[q], vv[q][:c].astype(BF16)).astype(BF16) for q in heads]
        qm = [hd(rt_g, q).astype(F32) + _dot(a_rb[q], wt[q]) for q in heads]
        o0 = [_dot(a_rb[q], u0[q]) + vv[q][c:] for q in heads]
        m = [eye_f * hd(pd_g, q) + _dot_t0(hd(bh_g, q), wt[q]) for q in heads]
        nn = [_dot_t0(hd(bh_g, q), u0[q]) + kv[q] for q in heads]
        h0 = gi * group
        st = s_scr[pl.ds(h0, group)]
        res = [_dot(jnp.concatenate([qm[q], m[q]], axis=0).astype(BF16), st[q].astype(BF16)) for q in heads]
        s_scr[pl.ds(h0, group)] = jnp.stack([res[q][c:] + nn[q] for q in heads])
        o_ref[:, sl] = jnp.concatenate([res[q][:c] + o0[q] for q in heads], axis=1)
        return carry

    lax.fori_loop(0, n_heads // group, group_body, 0)


def rwkv_scan(r, v, kk, lw, kd, bd, batch, dims):
    t, rw = r.shape
    n_ctx, ctx_len, seq = dims
    c = CHUNK
    n_cc = ctx_len // c
    n_lc = seq // c
    steps = n_cc + n_lc

    def blk(b, d, s):
        ctx_i = jnp.where(d == 0, s, n_cc - 1 - s)
        lat_i = jnp.where(d == 0, s - n_cc, n_lc - 1 - (s - n_cc))
        return jnp.where(s < n_cc, b * n_cc + ctx_i, n_ctx // c + b * n_lc + lat_i)

    shared = pl.BlockSpec((c, rw), lambda b, d, s: (blk(b, d, s), 0))
    per_dir = pl.BlockSpec((None, c, rw), lambda b, d, s: (d, blk(b, d, s), 0))
    n_heads = rw // HEAD
    group = 32 if n_heads % 32 == 0 else 2
    prep = pltpu.VMEM((c, rw), BF16)
    return pl.pallas_call(
        functools.partial(_rwkv_scan_kernel, n_heads=n_heads, group=group),
        grid=(batch, 2, steps),
        in_specs=[shared, shared, shared, per_dir, per_dir, per_dir],
        out_specs=per_dir,
        out_shape=jax.ShapeDtypeStruct((2, t, rw), F32),
        scratch_shapes=[pltpu.VMEM((n_heads, HEAD, HEAD), F32)] + [prep] * 7 + [pltpu.VMEM((1, rw), F32)],
        compiler_params=_cparams(3), name="rwkv_scan",
    )(r, v, kk, lw, kd, bd)


def _rwkv_readout_kernel(o_ref, bv_ref, g_ref, lg_ref, lb_ref, ones_ref, y_ref):
    o = o_ref[0] + o_ref[1]
    ones_bd = ones_ref[...]
    mu = _head_sum(o, ones_bd) * (1.0 / HEAD)
    dev = o - mu
    var = _head_sum(dev * dev, ones_bd) * (1.0 / HEAD)
    on = dev * lax.rsqrt(var + GN_EPS) * lg_ref[...] + lb_ref[...]
    y_ref[...] = ((on + bv_ref[...]) * g_ref[...]).astype(y_ref.dtype)


def rwkv_readout(o2, bv, g, lnx_g, lnx_b, tm_pref=512):
    _, t, rw = o2.shape
    tm = _tile(t, tm_pref)
    tn = _tile(rw, 512)
    ones_bd = (jnp.arange(tn)[:, None] // HEAD == jnp.arange(tn)[None, :] // HEAD).astype(BF16)
    tok = pl.BlockSpec((tm, tn), lambda i, j: (i, j))
    vec = pl.BlockSpec((1, tn), lambda i, j: (0, j))
    return pl.pallas_call(
        _rwkv_readout_kernel,
        grid=(t // tm, rw // tn),
        in_specs=[pl.BlockSpec((2, tm, tn), lambda i, j: (0, i, j)), tok, tok, vec, vec,
                  pl.BlockSpec((tn, tn), lambda i, j: (0, 0))],
        out_specs=tok,
        out_shape=jax.ShapeDtypeStruct((t, rw), BF16),
        compiler_params=_cparams(2), name="rwkv_readout",
    )(o2, bv, g, lnx_g.reshape(1, rw), lnx_b.reshape(1, rw), ones_bd)


def _attend_group(q_heads, k_all, v_all, bias, sinks):
    return _attend_finish(_attend_scores(q_heads, k_all), v_all, bias, sinks)


def _attend_scores(q_heads, k_all):
    return _dot_t1(jnp.concatenate(q_heads, axis=0), k_all)


def _attend_finish(s, v_all, bias, sinks):
    nq = bias.shape[0]
    probs, denoms = [], []
    for g, sink in enumerate(sinks):
        sg = s[g * nq:(g + 1) * nq] + bias
        m = jnp.maximum(jnp.max(sg, axis=-1, keepdims=True), sink)
        p = jnp.exp(sg - m)
        denoms.append(jnp.sum(p, axis=-1, keepdims=True) + jnp.exp(sink - m))
        probs.append(p.astype(BF16))
    o = _dot(jnp.concatenate(probs, axis=0), v_all)
    return [o[g * nq:(g + 1) * nq] / denoms[g] for g in range(len(sinks))]


def _attn_kernel(sink_ref, q_ref, kp_ref, kc_ref, kn_ref, vp_ref, vc_ref, vn_ref, kx_ref, vx_ref, o_ref,
                 *, seq, n_ctx_keys, n_ctx_blocks):
    pair = pl.program_id(2)
    qi = pl.program_id(1) - n_ctx_blocks
    span = QBLOCK + 2 * WINDOW
    rr = lax.broadcasted_iota(jnp.int32, (QBLOCK, span + n_ctx_keys), 0)
    cc = lax.broadcasted_iota(jnp.int32, (QBLOCK, span + n_ctx_keys), 1)
    key_pos = qi * QBLOCK - WINDOW + cc
    off = cc - WINDOW - rr
    in_band = (jnp.abs(off) <= WINDOW) & (key_pos >= 0) & (key_pos < seq) & (qi >= 0)
    bias = jnp.where(jnp.logical_or(cc >= span, in_band), 0.0, -jnp.inf)
    q_all = q_ref[...] * jnp.asarray(ATT_SCALE, q_ref.dtype)
    scores = []
    for kv in range(2):
        hs = slice(kv * HEAD, (kv + 1) * HEAD)
        k_all = jnp.concatenate([kp_ref[:, hs], kc_ref[:, hs], kn_ref[:, hs], kx_ref[:, hs]], axis=0)
        heads = [kv * ATT_GROUP + g for g in range(ATT_GROUP)]
        scores.append(_attend_scores([q_all[:, hq * HEAD:(hq + 1) * HEAD] for hq in heads], k_all))
    outs = []
    for kv in range(2):
        hs = slice(kv * HEAD, (kv + 1) * HEAD)
        v_all = jnp.concatenate([vp_ref[:, hs], vc_ref[:, hs], vn_ref[:, hs], vx_ref[:, hs]], axis=0)
        heads = [kv * ATT_GROUP + g for g in range(ATT_GROUP)]
        outs += _attend_finish(scores[kv], v_all, bias, [sink_ref[pair * 2 * ATT_GROUP + hq] for hq in heads])
    o_ref[...] = jnp.concatenate(outs, axis=1).astype(o_ref.dtype)


def attention(qkv, sink, batch, dims, q_cols, kv_cols):
    t = qkv.shape[0]
    n_ctx, ctx_len, seq = dims
    n_pairs = kv_cols // (2 * HEAD)
    qw = 2 * ATT_GROUP * HEAD
    kw = 2 * HEAD
    nqb = seq // QBLOCK
    ncb = ctx_len // QBLOCK
    lat0 = n_ctx // QBLOCK
    kcol = q_cols // kw
    vcol = (q_cols + kv_cols) // kw
    smem = pl.BlockSpec(memory_space=pltpu.SMEM)

    def q_map(b, i, p):
        return (jnp.where(i < ncb, b * ncb + i, lat0 + b * nqb + i - ncb), p)

    def band(col0, shift):
        def imap(b, i, p):
            return (lat0 + b * nqb + jnp.clip(i - ncb + shift, 0, nqb - 1), col0 + p)
        return pl.BlockSpec((QBLOCK, kw), imap)

    return pl.pallas_call(
        functools.partial(_attn_kernel, seq=seq, n_ctx_keys=ctx_len, n_ctx_blocks=ncb),
        grid=(batch, ncb + nqb, n_pairs),
        in_specs=[smem,
                  pl.BlockSpec((QBLOCK, qw), q_map),
                  band(kcol, -1), band(kcol, 0), band(kcol, 1),
                  band(vcol, -1), band(vcol, 0), band(vcol, 1),
                  pl.BlockSpec((ctx_len, kw), lambda b, i, p: (b, kcol + p)),
                  pl.BlockSpec((ctx_len, kw), lambda b, i, p: (b, vcol + p))],
        out_specs=pl.BlockSpec((QBLOCK, qw), q_map),
        out_shape=jax.ShapeDtypeStruct((t, q_cols), BF16),
        compiler_params=_cparams(3), name="attention",
    )(sink, qkv, qkv, qkv, qkv, qkv, qkv, qkv, qkv, qkv)


def _rope_tables(seq, width):
    pos = jnp.arange(seq, dtype=jnp.int32)
    row_pos = (pos // GRID_W).astype(F32)
    col_pos = (pos % GRID_W).astype(F32)
    half = HEAD // 2
    inv_freq = ROPE_THETA ** (-jnp.arange(0, half, 2, dtype=F32) / half)
    ang_r = row_pos[:, None] * inv_freq
    ang_c = col_pos[:, None] * inv_freq
    cos64 = jnp.concatenate([jnp.cos(ang_r), jnp.cos(ang_r), jnp.cos(ang_c), jnp.cos(ang_c)], axis=-1)
    sin64 = jnp.concatenate([-jnp.sin(ang_r), jnp.sin(ang_r), -jnp.sin(ang_c), jnp.sin(ang_c)], axis=-1)
    reps = width // HEAD
    return jnp.tile(cos64, (1, reps)), jnp.tile(sin64, (1, reps))


def _router_kernel(x_ref, g_ref, sh_ref, sc_ref, rw_ref, rb_ref, h_ref, ids_ref, gts_ref, *, tm, n_ctx, seq):
    rid = _row_id(pl.program_id(0) * tm, n_ctx, seq)
    h = _norm_mod(x_ref[...], g_ref[...], sh_ref[pl.ds(rid, 1), :], sc_ref[pl.ds(rid, 1), :])
    h_ref[...] = h
    logits = _dot_hi(h, rw_ref[...]).T[:rb_ref.shape[0]]
    e = jnp.exp(logits - jnp.max(logits, axis=0, keepdims=True))
    probs = e / jnp.sum(e, axis=0, keepdims=True)
    sel = probs + rb_ref[...]
    n_g, per = N_EXPERT_GROUPS, EXPERTS_PER_GROUP

    def row(a, r):
        return a[r:r + 1, :]

    scores = []
    for gi in range(n_g):
        a, b, c, d = (row(sel, gi * per + r) for r in range(per))
        hi1, lo1 = jnp.maximum(a, b), jnp.minimum(a, b)
        hi2, lo2 = jnp.maximum(c, d), jnp.minimum(c, d)
        scores.append(jnp.maximum(hi1, hi2) + jnp.maximum(jnp.minimum(hi1, hi2), jnp.maximum(lo1, lo2)))
    best = scores[0]
    bg = jnp.zeros_like(best, dtype=jnp.int32)
    for gi in range(1, n_g):
        better = scores[gi] > best
        best = jnp.where(better, scores[gi], best)
        bg = jnp.where(better, gi, bg)
    in_sel = []
    in_prob = []
    for r in range(per):
        s_r = row(sel, r)
        p_r = row(probs, r)
        for gi in range(1, n_g):
            s_r = jnp.where(bg == gi, row(sel, gi * per + r), s_r)
            p_r = jnp.where(bg == gi, row(probs, gi * per + r), p_r)
        in_sel.append(s_r)
        in_prob.append(p_r)

    def argmax_first(vals, exclude):
        bv = None
        for r in range(per):
            v = vals[r] if exclude is None else jnp.where(exclude == r, -jnp.inf, vals[r])
            if bv is None:
                bv, bi = v, jnp.zeros_like(bg)
            else:
                better = v > bv
                bv = jnp.where(better, v, bv)
                bi = jnp.where(better, r, bi)
        return bi

    i1 = argmax_first(in_sel, None)
    i2 = argmax_first(in_sel, i1)

    def pick(vals, idx):
        out = vals[0]
        for r in range(1, per):
            out = jnp.where(idx == r, vals[r], out)
        return out

    p1 = pick(in_prob, i1)
    p2 = pick(in_prob, i2)
    tot = p1 + p2
    ids_ref[...] = jnp.concatenate([bg * per + i1, bg * per + i2], axis=0)
    gts_ref[...] = jnp.concatenate([p1 / tot, p2 / tot], axis=0)


def router(x, g, mods, layer, router_w_pad, router_b, dims, tm_pref=256):
    t, d = x.shape
    n_ctx, seq = dims
    n_e = router_b.shape[0]
    lanes = router_w_pad.shape[1]
    tm = _tile(math.gcd(n_ctx, seq), tm_pref)
    return pl.pallas_call(
        functools.partial(_router_kernel, tm=tm, n_ctx=n_ctx, seq=seq),
        grid=(t // tm,),
        in_specs=[pl.BlockSpec((tm, d), lambda i: (i, 0)),
                  pl.BlockSpec((1, d), lambda i: (0, 0)),
                  pl.BlockSpec((None, None, MOD_ROWS, d), lambda i: (layer, 3, 0, 0)),
                  pl.BlockSpec((None, None, MOD_ROWS, d), lambda i: (layer, 4, 0, 0)),
                  pl.BlockSpec((d, lanes), lambda i: (0, 0)),
                  pl.BlockSpec((n_e, 1), lambda i: (0, 0))],
        out_specs=[pl.BlockSpec((tm, d), lambda i: (i, 0)),
                   pl.BlockSpec((2, tm), lambda i: (0, i)),
                   pl.BlockSpec((2, tm), lambda i: (0, i))],
        out_shape=[jax.ShapeDtypeStruct((t, d), F32),
                   jax.ShapeDtypeStruct((2, t), jnp.int32),
                   jax.ShapeDtypeStruct((2, t), F32)],
        compiler_params=_cparams(1), name="router",
    )(x, g.reshape(1, d), mods, mods, router_w_pad, router_b.reshape(n_e, 1))


def _gather(src_hbm, idx_ref, base, n_rows, dst, sem):
    def copy(r):
        return pltpu.make_async_copy(src_hbm.at[pl.ds(idx_ref[base + r], 1), :], dst.at[pl.ds(r, 1), :], sem)

    def start_row(r, c):
        copy(r).start()
        return c

    def wait_row(r, c):
        copy(r).wait()
        return c

    def start(unroll=8):
        lax.fori_loop(0, n_rows, start_row, 0, unroll=unroll)

    def wait():
        lax.fori_loop(0, n_rows, wait_row, 0, unroll=8)

    return start, wait


def _expert_kernel(te_ref, act_ref, tok_ref, h_hbm, wg_ref, wu_ref, wd_ref, y_ref, xbuf, sems, *, tm):
    i = pl.program_id(0)
    slot = i % 2

    def gather(tile, s):
        return _gather(h_hbm, tok_ref, tile * tm, tm, xbuf.at[s], sems.at[s])

    @pl.when(jnp.logical_and(i == 0, act_ref[0] == 1))
    def _():
        gather(0, 0)[0]()

    @pl.when(act_ref[i] == 1)
    def _():
        gather(i, slot)[1]()
        gather(i + 1, 1 - slot)[0](unroll=True)
        x = xbuf[slot].astype(BF16)
        a = _dot(x, wg_ref[...])
        hid = (a * _sigmoid(a)) * _dot(x, wu_ref[...])
        y_ref[...] = _dot(hid.astype(BF16), wd_ref[...])

    @pl.when(act_ref[i] == 0)
    def _():
        @pl.when(jnp.logical_and(i > 0, act_ref[jnp.maximum(i - 1, 0)] == 1))
        def _():
            gather(i, slot)[1]()
        y_ref[...] = jnp.zeros_like(y_ref)


def expert_ffn(h, tile_expert, tile_active, slot_token, w_gate, w_up, w_down, layer, tm):
    t, d = h.shape
    _, n_e, _, f = w_gate.shape
    n_tiles = tile_expert.shape[0]
    single = pl.Buffered(1)
    grid_spec = pltpu.PrefetchScalarGridSpec(
        num_scalar_prefetch=3,
        grid=(n_tiles,),
        in_specs=[pl.BlockSpec(memory_space=pl.ANY),
                  pl.BlockSpec((None, None, d, f), lambda i, te, act, tok: (layer, te[i], 0, 0), pipeline_mode=single),
                  pl.BlockSpec((None, None, d, f), lambda i, te, act, tok: (layer, te[i], 0, 0), pipeline_mode=single),
                  pl.BlockSpec((None, None, f, d), lambda i, te, act, tok: (layer, te[i], 0, 0), pipeline_mode=single)],
        out_specs=pl.BlockSpec((tm, d), lambda i, te, act, tok: (i, 0)),
        scratch_shapes=[pltpu.VMEM((2, tm, d), F32), pltpu.SemaphoreType.DMA((2,))],
    )
    return pl.pallas_call(
        functools.partial(_expert_kernel, tm=tm),
        grid_spec=grid_spec,
        out_shape=jax.ShapeDtypeStruct((n_tiles * tm, d), F32),
        compiler_params=_cparams(1), name="expert_ffn",
    )(tile_expert, tile_active, slot_token, h, w_gate, w_up, w_down)


def _combine_kernel(slots_ref, x_ref, gate_ref, rw_ref, y_hbm, o_ref, ybuf, sems, *, tm, n_tok, n_ctx, seq):
    i = pl.program_id(0)
    n = pl.num_programs(0)
    slot = i % 2

    def gathers(tile, s):
        return [_gather(y_hbm, slots_ref, k * n_tok + tile * tm, tm, ybuf.at[s, k], sems.at[s]) for k in range(2)]

    @pl.when(i == 0)
    def _():
        for start, _ in gathers(0, 0):
            start()

    @pl.when(i + 1 < n)
    def _():
        for start, _ in gathers(i + 1, 1 - slot):
            start()

    for _, wait in gathers(i, slot):
        wait()
    rid = _row_id(i * tm, n_ctx, seq)
    mixed = rw_ref[:, 0:1] * ybuf[slot, 0] + rw_ref[:, 1:2] * ybuf[slot, 1]
    o_ref[...] = x_ref[...] + gate_ref[pl.ds(rid, 1), :] * mixed


def moe_combine(x, y_sorted, tok_slots, route_w, mods, layer, dims, tm_pref=256):
    t, d = x.shape
    n_ctx, seq = dims
    tm = _tile(math.gcd(n_ctx, seq), tm_pref)
    grid_spec = pltpu.PrefetchScalarGridSpec(
        num_scalar_prefetch=1,
        grid=(t // tm,),
        in_specs=[pl.BlockSpec((tm, d), lambda i, s: (i, 0)),
                  pl.BlockSpec((None, None, MOD_ROWS, d), lambda i, s: (layer, 5, 0, 0)),
                  pl.BlockSpec((tm, 2), lambda i, s: (i, 0)),
                  pl.BlockSpec(memory_space=pl.ANY)],
        out_specs=pl.BlockSpec((tm, d), lambda i, s: (i, 0)),
        scratch_shapes=[pltpu.VMEM((2, 2, tm, d), F32), pltpu.SemaphoreType.DMA((2,))],
    )
    return pl.pallas_call(
        functools.partial(_combine_kernel, tm=tm, n_tok=t, n_ctx=n_ctx, seq=seq),
        grid_spec=grid_spec,
        out_shape=jax.ShapeDtypeStruct((t, d), F32),
        input_output_aliases={1: 0},
        compiler_params=_cparams(1), name="moe_combine",
    )(tok_slots, x, mods, route_w, y_sorted)


def _moe_plan(ids, n_e, tm):
    n_tok = ids.shape[1]
    e_flat = ids.reshape(-1)
    n_pairs = e_flat.shape[0]
    onehot = (e_flat[:, None] == jnp.arange(n_e, dtype=jnp.int32)[None, :]).astype(jnp.int32)
    rank = jnp.take_along_axis(jnp.cumsum(onehot, axis=0), e_flat[:, None], axis=1)[:, 0] - 1
    counts = jnp.sum(onehot, axis=0)
    padded = (counts + tm - 1) // tm * tm
    ends = jnp.cumsum(padded)
    starts = ends - padded
    dest = starts[e_flat] + rank
    n_tiles = n_pairs // tm + n_e + 1
    n_slots = n_tiles * tm
    pair_tok = jnp.arange(n_pairs, dtype=jnp.int32) % n_tok
    slot_token = jnp.zeros((n_slots,), jnp.int32).at[dest].set(pair_tok, unique_indices=True)
    tile_start = jnp.arange(n_tiles, dtype=jnp.int32) * tm
    tile_expert = jnp.minimum(jnp.searchsorted(ends, tile_start, side="right"), n_e - 1).astype(jnp.int32)
    tile_active = (tile_start < ends[-1]).astype(jnp.int32)
    last_expert = tile_expert[jnp.maximum(ends[-1] // tm - 1, 0)]
    tile_expert = jnp.where(tile_active == 1, tile_expert, last_expert)
    return tile_expert, tile_active, slot_token, dest.astype(jnp.int32)


def moe_layer(x, norm_g, mods, layer, router_w_pad, router_b, w_gate, w_up, w_down, dims, tm_pref=256):
    n_e = w_gate.shape[1]
    h, ids, gates = router(x, norm_g, mods, layer, router_w_pad, router_b, dims)
    tm = _tile(2 * x.shape[0], tm_pref)
    tile_expert, tile_active, slot_token, dest = _moe_plan(ids, n_e, tm)
    y_sorted = expert_ffn(h, tile_expert, tile_active, slot_token, w_gate, w_up, w_down, layer, tm)
    return moe_combine(x, y_sorted, dest, gates.T, mods, layer, dims)


def fourier_rwkv_layer(x, mods, layer, j, norm_g, w_in, w_in_b, w_out_b, shift_mu, w0, w_up, a0, a_up, g_up,
                       k_k, k_a, r_k, lnx_g, lnx_b, batch, dims, fnet_tabs):
    n_ctx, ctx_len, seq = dims
    rw = w0.shape[1]
    fw = w_in.shape[1] - 3 * rw - w_up.shape[1] - a_up.shape[1] - g_up.shape[0]
    dl, il, gl = w_up.shape[1], a_up.shape[1], g_up.shape[0]
    lora_pad = 128
    rdims = (n_ctx, seq)

    def pad_cols(a, n):
        return jnp.pad(a, ((0, 0), (0, n - a.shape[1])))

    o_lo = fw + 3 * rw
    w_lo = jnp.concatenate([pad_cols(w_in[:, o_lo:o_lo + dl], lora_pad),
                            pad_cols(w_in[:, o_lo + dl:o_lo + dl + il], lora_pad),
                            w_in[:, o_lo + dl + il:]], axis=1).astype(BF16)
    mu = shift_mu.reshape(1, -1)
    mu_lo = jnp.concatenate([pad_cols(mu[:, 3 * rw:3 * rw + dl], lora_pad),
                             pad_cols(mu[:, 3 * rw + dl:3 * rw + dl + il], lora_pad),
                             mu[:, 3 * rw + dl + il:]], axis=1)
    w_up_p = jnp.pad(w_up, ((0, 0), (0, lora_pad - dl), (0, 0))).astype(BF16)
    a_up_p = jnp.pad(a_up, ((0, 0), (0, lora_pad - il), (0, 0))).astype(BF16)

    h = norm_mod(x, norm_g, mods, layer, 0, 1, rdims)
    f = matmul(h, w_in_b, BF16, rdims, cols=(0, fw), w_index=j)
    prkv = matmul(h, w_in_b, BF16, rdims, cols=(fw, 3 * rw), w_index=j)
    plo = matmul(h, w_lo, F32, rdims)

    r, v, kk, lw, kd, bd, g, bv = rwkv_terms(
        prkv, plo, mu[:, :3 * rw], mu_lo, w0, w_up_p, a0, a_up_p, g_up.astype(BF16),
        k_k.reshape(1, rw), k_a.reshape(1, rw), r_k.reshape(1, rw), dims, lora_pad)
    o2 = rwkv_scan(r, v, kk, lw, kd, bd, batch, dims)
    y_rw = rwkv_readout(o2, bv, g, lnx_g, lnx_b)

    cos_c, sin_c, cos_x, sin_x = fnet_tabs
    gc = fw // FNET_GROUPS
    y1c, y2c = fnet_channel_dft(f, cos_c, sin_c, 0, n_ctx)
    y1l, y2l = fnet_channel_dft(f, cos_c, sin_c, n_ctx, x.shape[0] - n_ctx)
    f_ctx = fnet_sequence_dft(y1c, y2c, 0, batch, ctx_len, cos_x, sin_x, gc)
    f_lat = fnet_sequence_fft(y1l, y2l, batch, seq, gc)
    f_mix = jnp.concatenate([f_ctx, f_lat], axis=0)
    return matmul_gated_residual([f_mix, y_rw], w_out_b, j, x, mods, layer, 2, rdims)


def attention_layer(x, mods, layer, j, norm_g, w_qkv_b, w_o_b, sink, batch, dims, rope_tabs):
    n_ctx, ctx_len, seq = dims
    q_cols = w_o_b.shape[1]
    kv_cols = (w_qkv_b.shape[2] - q_cols) // 2
    cos_t, sin_t = rope_tabs
    h = norm_mod(x, norm_g, mods, layer, 0, 1, (n_ctx, seq))
    qkv = matmul(h, w_qkv_b, BF16, (n_ctx, seq), rope=(q_cols + kv_cols, cos_t, sin_t), w_index=j,
                 tn_pref=cos_t.shape[1])
    att = attention(qkv, sink, batch, dims, q_cols, kv_cols)
    return matmul_gated_residual([att], w_o_b, j, x, mods, layer, 2, (n_ctx, seq))


def kernel(x, c, ctx, c_ctx, ada_down, ada_up, ada_bias, norm1_g, norm2_g, final_g, mix_w_in, mix_w_out, shift_mu, decay_w0, decay_up, iclr_a0, iclr_up, gate_up, k_k, k_a, r_k, lnx_g, lnx_b, att_w_qkv, att_w_o, att_sink, router_w, router_b, exp_w_gate, exp_w_up, exp_w_down):
    batch, seq, d = x.shape
    ctx_len = ctx.shape[1]
    depth = ada_down.shape[0]
    n_ctx = batch * ctx_len
    assert batch + 1 <= MOD_ROWS
    dims = (n_ctx, ctx_len, seq)
    xs = jnp.concatenate([ctx.reshape(n_ctx, d), x.reshape(batch * seq, d)], axis=0)

    cvec = jnp.concatenate([c_ctx[None, :], c, jnp.zeros((MOD_ROWS - 1 - batch, d), F32)], axis=0)
    mods = adaln_all(cvec, ada_down, ada_up, ada_bias)
    mods = mods.reshape(depth, MOD_ROWS, N_MOD, d).transpose(0, 2, 1, 3)

    fw = mix_w_in.shape[2] - 3 * decay_w0.shape[2] - decay_up.shape[2] - iclr_up.shape[2] - gate_up.shape[1]
    gc = fw // FNET_GROUPS
    fnet_tabs = _dft_tables(gc) + _dft_tables(ctx_len)
    kv_cols = (att_w_qkv.shape[2] - att_w_o.shape[1]) // 2
    rope_tabs = _rope_tables(seq, min(512, kv_cols))
    router_w_pad = jnp.pad(router_w, ((0, 0), (0, 128 - router_w.shape[1])))
    mix_w_in_b = mix_w_in.astype(BF16)
    mix_w_out_b = mix_w_out.astype(BF16)
    att_w_qkv_b = att_w_qkv.astype(BF16)
    att_w_o_b = att_w_o.astype(BF16)
    exp_w_gate_b = exp_w_gate.astype(BF16)
    exp_w_up_b = exp_w_up.astype(BF16)
    exp_w_down_b = exp_w_down.astype(BF16)

    for layer in range(depth):
        j = layer // 2
        if layer % 2 == 0:
            xs = fourier_rwkv_layer(xs, mods, layer, j, norm1_g[layer], mix_w_in[j], mix_w_in_b, mix_w_out_b, shift_mu[j],
                                    decay_w0[j], decay_up[j], iclr_a0[j], iclr_up[j], gate_up[j],
                                    k_k[j], k_a[j], r_k[j], lnx_g[j], lnx_b[j], batch, dims, fnet_tabs)
        else:
            xs = attention_layer(xs, mods, layer, j, norm1_g[layer], att_w_qkv_b, att_w_o_b, att_sink[j],
                                 batch, dims, rope_tabs)
        xs = moe_layer(xs, norm2_g[layer], mods, layer, router_w_pad, router_b,
                       exp_w_gate_b, exp_w_up_b, exp_w_down_b, (n_ctx, seq))
    return final_norm(xs, final_g, n_ctx).reshape(batch, seq, d)
```
